```python
import jax, jax.numpy as jnp
from jax import lax
import numpy as np

D_MODEL = 1024
BATCH = 8
SEQ = 8192
DEPTH = 1

MIX_WIDTH = D_MODEL
ATT_WIDTH = MIX_WIDTH // 2
GMLP_WIDTH = MIX_WIDTH - ATT_WIDTH
ATT_HEAD_DIM = 64
N_ATT_HEADS = ATT_WIDTH // ATT_HEAD_DIM
N_GMLP_GROUPS = 8
GMLP_GROUP_DIM = GMLP_WIDTH // N_GMLP_GROUPS
CHUNK = 128
Q_BLOCK = 128
D_FF = 2816
CONV_WIDTH = 3
EPS = 1e-6

Q_END = ATT_WIDTH
K_END = 2 * ATT_WIDTH
V_END = 3 * ATT_WIDTH
U_END = V_END + GMLP_WIDTH
G_END = U_END + GMLP_WIDTH
IN_COLS = G_END + N_ATT_HEADS

kernel_name = "fox_gmlp_convffn_hybrid"


def rmsnorm(x, gain):
    xf = x.astype(jnp.float32)
    y = xf * lax.rsqrt(jnp.mean(xf * xf, axis=-1, keepdims=True) + EPS)
    return y.astype(x.dtype) * gain


def forgetting_attention(q, k, v, log_f):
    B, S, H, Dh = q.shape
    nb = S // Q_BLOCK
    c = jnp.cumsum(log_f, axis=1)
    c_k = c.transpose(0, 2, 1)[:, :, None, :]
    key_pos = jnp.arange(S)
    qb = q.reshape(B, nb, Q_BLOCK, H, Dh).transpose(1, 0, 2, 3, 4)
    cb = c.reshape(B, nb, Q_BLOCK, H).transpose(1, 0, 3, 2)
    pb = key_pos.reshape(nb, Q_BLOCK)
    scale = Dh ** -0.5

    def block(args):
        qi, ci, pi = args
        logits = jnp.einsum('bqhd,bkhd->bhqk', qi, k).astype(jnp.float32) * scale
        logits = logits + ci[..., None] - c_k
        causal = pi[:, None] >= key_pos[None, :]
        logits = jnp.where(causal, logits, -jnp.inf)
        p = jax.nn.softmax(logits, axis=-1).astype(v.dtype)
        return jnp.einsum('bhqk,bkhd->bqhd', p, v)

    out = lax.map(block, (qb, cb, pb))
    return out.transpose(1, 0, 2, 3, 4).reshape(B, S, H * Dh)


def chunked_spatial_gating(u, vg, v_gain, w_s, b_s):
    B, S, G, Dg = u.shape
    nc = S // CHUNK
    vg = rmsnorm(vg, v_gain.reshape(G, Dg))
    vc = vg.reshape(B, nc, CHUNK, G, Dg)
    w = w_s * jnp.tril(jnp.ones((CHUNK, CHUNK), dtype=w_s.dtype))
    mixed = jnp.einsum('gij,bnjgd->bnigd', w, vc) + b_s.T[None, None, :, :, None]
    return (u * mixed.reshape(B, S, G, Dg)).reshape(B, S, G * Dg)


def conv_ffn(x, w_up, conv_w, conv_b, w_down):
    h = x @ w_up
    h = lax.conv_general_dilated(
        h, conv_w[:, None, :], window_strides=(1,),
        padding=[(CONV_WIDTH - 1, 0)],
        dimension_numbers=('NWC', 'WIO', 'NWC'),
        feature_group_count=h.shape[-1]) + conv_b
    a, g = jnp.split(h, 2, axis=-1)
    return (jax.nn.silu(g) * a) @ w_down


def setup_inputs(seed: int = 0) -> dict:
    key = jax.random.key(seed)
    ks = jax.random.split(key, 16)
    f32 = jnp.float32
    x = jax.random.normal(ks[0], (BATCH, SEQ, D_MODEL), f32)
    norm_mix_g = 1.0 + 0.02 * jax.random.normal(ks[1], (DEPTH, D_MODEL), f32)
    w_in = jax.random.normal(ks[2], (DEPTH, D_MODEL, IN_COLS), f32) * D_MODEL ** -0.5
    b_forget = 2.0 + 0.5 * jax.random.normal(ks[3], (DEPTH, N_ATT_HEADS), f32)
    gmlp_norm_g = 1.0 + 0.02 * jax.random.normal(ks[4], (DEPTH, GMLP_WIDTH), f32)
    w_spatial = jax.random.normal(ks[5], (DEPTH, N_GMLP_GROUPS, CHUNK, CHUNK), f32) * CHUNK ** -0.5
    b_spatial = 1.0 + 0.1 * jax.random.normal(ks[6], (DEPTH, N_GMLP_GROUPS, CHUNK), f32)
    w_out = jax.random.normal(ks[7], (DEPTH, MIX_WIDTH, D_MODEL), f32) * MIX_WIDTH ** -0.5
    norm_ffn_g = 1.0 + 0.02 * jax.random.normal(ks[8], (DEPTH, D_MODEL), f32)
    w_up = jax.random.normal(ks[9], (DEPTH, D_MODEL, 2 * D_FF), f32) * D_MODEL ** -0.5
    conv_w = jax.random.normal(ks[10], (DEPTH, CONV_WIDTH, 2 * D_FF), f32) * CONV_WIDTH ** -0.5
    conv_b = 0.01 * jax.random.normal(ks[11], (DEPTH, 2 * D_FF), f32)
    w_down = jax.random.normal(ks[12], (DEPTH, D_FF, D_MODEL), f32) * D_FF ** -0.5
    norm_final_g = 1.0 + 0.02 * jax.random.normal(ks[13], (D_MODEL,), f32)
    return {"x": x, "norm_mix_g": norm_mix_g, "w_in": w_in, "b_forget": b_forget,
            "gmlp_norm_g": gmlp_norm_g, "w_spatial": w_spatial, "b_spatial": b_spatial,
            "w_out": w_out, "norm_ffn_g": norm_ffn_g, "w_up": w_up, "conv_w": conv_w,
            "conv_b": conv_b, "w_down": w_down, "norm_final_g": norm_final_g}


def reference(x, norm_mix_g, w_in, b_forget, gmlp_norm_g, w_spatial, b_spatial,
              w_out, norm_ffn_g, w_up, conv_w, conv_b, w_down, norm_final_g):
    B, S, _ = x.shape
    h = x
    for layer in range(DEPTH):
        xn = rmsnorm(h, norm_mix_g[layer])
        proj = xn @ w_in[layer]
        q = proj[..., :Q_END].reshape(B, S, N_ATT_HEADS, ATT_HEAD_DIM)
        k = proj[..., Q_END:K_END].reshape(B, S, N_ATT_HEADS, ATT_HEAD_DIM)
        v = proj[..., K_END:V_END].reshape(B, S, N_ATT_HEADS, ATT_HEAD_DIM)
        log_f = jax.nn.log_sigmoid(
            (proj[..., G_END:] + b_forget[layer]).astype(jnp.float32))
        att = forgetting_attention(q, k, v, log_f)
        uv = jax.nn.gelu(proj[..., V_END:G_END])
        u = uv[..., :GMLP_WIDTH].reshape(B, S, N_GMLP_GROUPS, GMLP_GROUP_DIM)
        vg = uv[..., GMLP_WIDTH:].reshape(B, S, N_GMLP_GROUPS, GMLP_GROUP_DIM)
        sg = chunked_spatial_gating(u, vg, gmlp_norm_g[layer], w_spatial[layer], b_spatial[layer])
        mix = jnp.concatenate([att, sg], axis=-1)
        h = h + mix @ w_out[layer]
        h = h + conv_ffn(rmsnorm(h, norm_ffn_g[layer]), w_up[layer], conv_w[layer],
                         conv_b[layer], w_down[layer])
    return rmsnorm(h, norm_final_g)
```

```python
import functools

import jax
import jax.numpy as jnp
from jax import lax
from jax.experimental import pallas as pl
from jax.experimental.pallas import tpu as pltpu

EPS = 1e-6
HEAD_DIM = 64
GROUP_DIM = 64
CHUNK = 128
CONV_WIDTH = 3

LANES = 128
SUBLANES = 8
MXU_DIM = 256
VMEM_LIMIT_BYTES = 56 * 1024 * 1024

AUG_W = 8
TOKEN_TILE = 512
ATT_TILE = 512
FF_CHUNK = 256

F32 = jnp.float32
BF16 = jnp.bfloat16


def _split3(x):
    hi = x.astype(BF16)
    r1 = x - hi.astype(F32)
    mid = r1.astype(BF16)
    lo = (r1 - mid.astype(F32)).astype(BF16)
    return hi, mid, lo


def _const_spec(shape):
    zeros = (0,) * len(shape)
    return pl.BlockSpec(shape, lambda *_: zeros, pipeline_mode=pl.Buffered(1))


def _proj_kernel(x_ref, g_ref, wqkv_ref, wuv_ref, wg_ref, bf_ref, gn_ref, ws_ref, bs_ref,
                 gsum_ref, q_ref, k_ref, v_ref, qa_ref, ka_ref, sg_ref, carry_ref, *, att_w, gm_w):
    tm = x_ref.shape[1]

    @pl.when(pl.program_id(1) == 0)
    def _():
        carry_ref[...] = jnp.zeros_like(carry_ref)

    x = x_ref[0]
    ms = jnp.mean(x * x, axis=-1, keepdims=True)
    xn = ((x * lax.rsqrt(ms + EPS)) * g_ref[...]).astype(BF16)

    qkv = jnp.dot(xn, wqkv_ref[...], preferred_element_type=F32)
    q_ref[0] = qkv[:, :att_w].astype(BF16)
    k_ref[0] = qkv[:, att_w:2 * att_w].astype(BF16)
    v_ref[0] = qkv[:, 2 * att_w:].astype(BF16)

    z = jnp.dot(xn, wg_ref[...], preferred_element_type=F32) + bf_ref[...]
    log_f = -(jnp.maximum(-z, 0.0) + jnp.log1p(jnp.exp(-jnp.abs(z))))

    row = lax.broadcasted_iota(jnp.int32, (CHUNK, CHUNK), 0)
    col = lax.broadcasted_iota(jnp.int32, (CHUNK, CHUNK), 1)
    causal = row >= col
    tri = jnp.where(causal, 1.0, 0.0).astype(BF16)
    offset = carry_ref[0:1, :]
    c_blocks = []
    for r in range(tm // CHUNK):
        blk = log_f[r * CHUNK:(r + 1) * CHUNK]
        s = offset
        for piece in _split3(blk):
            s = s + jnp.dot(tri, piece, preferred_element_type=F32)
        c_blocks.append(s)
        offset = s[CHUNK - 1:CHUNK, :]
    carry_ref[0:1, :] = offset
    c = jnp.concatenate(c_blocks, axis=0)

    c_hi, c_mid, c_lo = [piece.astype(F32) for piece in _split3(c)]
    slot = lax.broadcasted_iota(jnp.int32, (tm, LANES), 1) % AUG_W
    qa = jnp.where(slot == 0, c_hi, jnp.where(slot == 1, c_mid, jnp.where(slot == 2, c_lo,
         jnp.where(slot < 6, 1.0, 0.0))))
    ka = jnp.where(slot < 3, 1.0, jnp.where(slot == 3, -c_hi, jnp.where(slot == 4, -c_mid,
         jnp.where(slot == 5, -c_lo, 0.0))))
    qa_ref[0] = qa.astype(BF16)
    ka_ref[0] = ka.astype(BF16)

    uv = jax.nn.gelu(jnp.dot(xn, wuv_ref[...], preferred_element_type=F32))
    u = uv[:, :gm_w]
    vg = uv[:, gm_w:]
    sq = (vg * vg).astype(BF16)
    gsum = gsum_ref[...]
    msg = jnp.concatenate(
        [jnp.dot(sq[:, s * MXU_DIM:(s + 1) * MXU_DIM], gsum, preferred_element_type=F32)
         for s in range(gm_w // MXU_DIM)], axis=1)
    vgn = ((vg * lax.rsqrt(msg + EPS)) * gn_ref[...]).astype(BF16)

    groups_per_slab = MXU_DIM // GROUP_DIM
    lane_grp = lax.broadcasted_iota(jnp.int32, (CHUNK, MXU_DIM), 1) // GROUP_DIM
    w_masked = [jnp.where(causal, ws_ref[g], 0.0).astype(BF16) for g in range(ws_ref.shape[0])]
    mixed_rows = []
    for r in range(tm // CHUNK):
        slabs = []
        for s in range(gm_w // MXU_DIM):
            vs = vgn[r * CHUNK:(r + 1) * CHUNK, s * MXU_DIM:(s + 1) * MXU_DIM]
            m = None
            for j in range(groups_per_slab):
                mj = jnp.dot(w_masked[s * groups_per_slab + j], vs, preferred_element_type=F32)
                m = mj if m is None else jnp.where(lane_grp == j, mj, m)
            slabs.append(m)
        mixed_rows.append(jnp.concatenate(slabs, axis=1) + bs_ref[...])
    mixed = jnp.concatenate(mixed_rows, axis=0)
    sg_ref[0] = (u * mixed).astype(BF16)


def _proj_call(x, g, wqkv, wuv, wg, bfr, gn, ws, bs, gsum):
    b, s, d = x.shape
    tm = TOKEN_TILE
    att_w = wqkv.shape[1] // 3
    gm_w = wuv.shape[1] // 2
    tile = lambda w: pl.BlockSpec((1, tm, w), lambda i, j: (i, j, 0))
    out_shape = ([jax.ShapeDtypeStruct((b, s, att_w), BF16)] * 3
                 + [jax.ShapeDtypeStruct((b, s, LANES), BF16)] * 2
                 + [jax.ShapeDtypeStruct((b, s, gm_w), BF16)])
    return pl.pallas_call(
        functools.partial(_proj_kernel, att_w=att_w, gm_w=gm_w),
        grid=(b, s // tm),
        in_specs=[tile(d), _const_spec(g.shape), _const_spec(wqkv.shape), _const_spec(wuv.shape),
                  _const_spec(wg.shape), _const_spec(bfr.shape), _const_spec(gn.shape),
                  _const_spec(ws.shape), _const_spec(bs.shape), _const_spec(gsum.shape)],
        out_specs=[tile(att_w)] * 3 + [tile(LANES)] * 2 + [tile(gm_w)],
        out_shape=out_shape,
        scratch_shapes=[pltpu.VMEM((SUBLANES, LANES), F32)],
        compiler_params=pltpu.CompilerParams(
            dimension_semantics=("arbitrary", "arbitrary"), vmem_limit_bytes=VMEM_LIMIT_BYTES),
        name="proj",
    )(x, g, wqkv, wuv, wg, bfr, gn, ws, bs, gsum)


def _attn_kernel(q_ref, qa_ref, k_ref, ka_ref, v_ref, o_ref, m_ref, acc_ref):
    t = q_ref.shape[1]
    pair = pl.program_id(1)
    qi = pl.program_id(2)
    heads = LANES // HEAD_DIM

    lane = lax.broadcasted_iota(jnp.int32, (t, LANES), 1)
    q = q_ref[0].astype(F32)
    qa = qa_ref[0].astype(F32)
    qf = []
    for i in range(heads):
        qm = jnp.where(lane // HEAD_DIM == i, q, 0.0)
        qam = jnp.where(lane // AUG_W == pair * heads + i, qa, 0.0)
        qf.append(jnp.concatenate([qm, qam], axis=1).astype(BF16))

    m_ref[...] = jnp.full_like(m_ref, -jnp.inf)
    acc_ref[...] = jnp.zeros_like(acc_ref)
    ones = jnp.ones((t, LANES), BF16)
    row = lax.broadcasted_iota(jnp.int32, (t, t), 0)
    col = lax.broadcasted_iota(jnp.int32, (t, t), 1)

    def kv_step(j, diagonal):
        start = pl.multiple_of(j * t, t)
        kf = jnp.concatenate([k_ref[0, pl.ds(start, t), :], ka_ref[0, pl.ds(start, t), :]], axis=1)
        vf = jnp.concatenate([v_ref[0, pl.ds(start, t), :], ones], axis=1)
        for i in range(heads):
            s = lax.dot_general(qf[i], kf, (((1,), (1,)), ((), ())), preferred_element_type=F32)
            if diagonal:
                s = jnp.where(row >= col, s, -jnp.inf)
            m_prev = m_ref[i]
            m_next = jnp.maximum(m_prev, jnp.max(s, axis=1, keepdims=True))
            alpha = jnp.exp(m_prev - m_next)
            p = jnp.exp(s - m_next[:, 0:1]).astype(BF16)
            pv = jnp.dot(p, vf, preferred_element_type=F32)
            acc_ref[i] = acc_ref[i] * jnp.concatenate([alpha, alpha], axis=1) + pv
            m_ref[i] = m_next

    def body(j, carry):
        kv_step(j, False)
        return carry

    lax.fori_loop(0, qi, body, 0)
    kv_step(qi, True)

    out = None
    for i in range(heads):
        a = acc_ref[i]
        o = a[:, :LANES] / a[:, LANES:]
        out = o if out is None else jnp.where(lane // HEAD_DIM == i, o, out)
    o_ref[0] = out.astype(o_ref.dtype)


def _attn_call(q, qa, k, ka, v):
    b, s, w = q.shape
    t = ATT_TILE
    qspec = pl.BlockSpec((1, t, LANES), lambda i, p, j: (i, j, p))
    aspec = pl.BlockSpec((1, t, LANES), lambda i, p, j: (i, j, 0))
    kspec = pl.BlockSpec((1, s, LANES), lambda i, p, j: (i, 0, p))
    kaspec = pl.BlockSpec((1, s, LANES), lambda i, p, j: (i, 0, 0))
    return pl.pallas_call(
        _attn_kernel,
        grid=(b, w // LANES, s // t),
        in_specs=[qspec, aspec, kspec, kaspec, kspec],
        out_specs=qspec,
        out_shape=jax.ShapeDtypeStruct((b, s, w), BF16),
        scratch_shapes=[pltpu.VMEM((LANES // HEAD_DIM, t, LANES), F32),
                        pltpu.VMEM((LANES // HEAD_DIM, t, 2 * LANES), F32)],
        compiler_params=pltpu.CompilerParams(
            dimension_semantics=("arbitrary", "arbitrary", "arbitrary"),
            vmem_limit_bytes=VMEM_LIMIT_BYTES),
        name="attn",
    )(q, qa, k, ka, v)


def _ffn_kernel(x_ref, att_ref, sg_ref, wo_ref, g2_ref, wup_ref, cw_ref, cb_ref, wdn_ref, g3_ref,
                o_ref, hn_ref, hbuf_ref, act_ref, carry_ref, *, final_norm):
    tm = x_ref.shape[1]
    n_chunks = wup_ref.shape[0]
    halo = SUBLANES

    @pl.when(pl.program_id(1) == 0)
    def _():
        carry_ref[...] = jnp.zeros_like(carry_ref)

    mix = jnp.concatenate([att_ref[0], sg_ref[0]], axis=1)
    h1 = x_ref[0] + jnp.dot(mix, wo_ref[...], preferred_element_type=F32)
    ms = jnp.mean(h1 * h1, axis=-1, keepdims=True)
    hn_ref[...] = ((h1 * lax.rsqrt(ms + EPS)) * g2_ref[...]).astype(BF16)

    for c in range(n_chunks):
        hbuf_ref[0:halo, :] = carry_ref[c]
        hbuf_ref[halo:, :] = jnp.dot(hn_ref[...], wup_ref[c], preferred_element_type=F32)
        carry_ref[c] = hbuf_ref[tm:tm + halo, :]
        cw = cw_ref[c]
        y = cb_ref[c]
        for tap in range(CONV_WIDTH):
            shift = CONV_WIDTH - 1 - tap
            y = y + hbuf_ref[halo - shift:halo - shift + tm, :] * cw[tap:tap + 1, :]
        a = y[:, :FF_CHUNK]
        g = y[:, FF_CHUNK:]
        act_ref[:, c * FF_CHUNK:(c + 1) * FF_CHUNK] = (jax.nn.silu(g) * a).astype(BF16)

    h2 = h1 + jnp.dot(act_ref[...], wdn_ref[...], preferred_element_type=F32)
    if final_norm:
        ms2 = jnp.mean(h2 * h2, axis=-1, keepdims=True)
        h2 = (h2 * lax.rsqrt(ms2 + EPS)) * g3_ref[...]
    o_ref[0] = h2


def _ffn_call(x, att, sg, wo, g2, wup, cw, cb, wdn, g3, final_norm):
    b, s, d = x.shape
    tm = TOKEN_TILE
    n_chunks = wup.shape[0]
    d_ff = wdn.shape[0]
    tile = lambda w: pl.BlockSpec((1, tm, w), lambda i, j: (i, j, 0))
    return pl.pallas_call(
        functools.partial(_ffn_kernel, final_norm=final_norm),
        grid=(b, s // tm),
        in_specs=[tile(d), tile(att.shape[2]), tile(sg.shape[2]), _const_spec(wo.shape),
                  _const_spec(g2.shape), _const_spec(wup.shape), _const_spec(cw.shape),
                  _const_spec(cb.shape), _const_spec(wdn.shape), _const_spec(g3.shape)],
        out_specs=tile(d),
        out_shape=jax.ShapeDtypeStruct((b, s, d), F32),
        scratch_shapes=[pltpu.VMEM((tm, d), BF16),
                        pltpu.VMEM((tm + SUBLANES, 2 * FF_CHUNK), F32),
                        pltpu.VMEM((tm, d_ff), BF16),
                        pltpu.VMEM((n_chunks, SUBLANES, 2 * FF_CHUNK), F32)],
        compiler_params=pltpu.CompilerParams(
            dimension_semantics=("arbitrary", "arbitrary"), vmem_limit_bytes=VMEM_LIMIT_BYTES),
        name="ffn",
    )(x, att, sg, wo, g2, wup, cw, cb, wdn, g3)


def kernel(x, norm_mix_g, w_in, b_forget, gmlp_norm_g, w_spatial, b_spatial, w_out, norm_ffn_g,
           w_up, conv_w, conv_b, w_down, norm_final_g):
    depth, d_model, _ = w_in.shape
    n_heads = b_forget.shape[1]
    att_w = n_heads * HEAD_DIM
    gm_w = gmlp_norm_g.shape[1]
    n_groups = w_spatial.shape[1]
    d_ff = w_down.shape[1]
    assert gm_w == n_groups * GROUP_DIM and w_spatial.shape[2] == CHUNK
    assert n_heads * AUG_W <= LANES and att_w % LANES == 0 and gm_w % MXU_DIM == 0
    assert d_ff % FF_CHUNK == 0 and x.shape[1] % TOKEN_TILE == 0 and x.shape[1] % ATT_TILE == 0
    n_chunks = d_ff // FF_CHUNK
    scale = HEAD_DIM ** -0.5

    gi = jnp.arange(MXU_DIM) // GROUP_DIM
    gsum = jnp.where(gi[:, None] == gi[None, :], 1.0 / GROUP_DIM, 0.0).astype(BF16)

    h = x
    for layer in range(depth):
        w = w_in[layer]
        wq = w[:, :att_w] * scale
        wqkv = jnp.concatenate([wq, w[:, att_w:3 * att_w]], axis=1).astype(BF16)
        wuv = w[:, 3 * att_w:3 * att_w + 2 * gm_w].astype(BF16)
        wgate = w[:, 3 * att_w + 2 * gm_w:]
        wg = jnp.zeros((d_model, LANES), F32).at[:, :n_heads * AUG_W].set(
            jnp.repeat(wgate, AUG_W, axis=1)).astype(BF16)
        bfr = jnp.zeros((1, LANES), F32).at[0, :n_heads * AUG_W].set(
            jnp.repeat(b_forget[layer], AUG_W))
        bs = jnp.repeat(b_spatial[layer].T, GROUP_DIM, axis=1)

        q, k, v, qa, ka, sg = _proj_call(
            h, norm_mix_g[layer][None, :], wqkv, wuv, wg, bfr, gmlp_norm_g[layer][None, :],
            w_spatial[layer], bs, gsum)
        att = _attn_call(q, qa, k, ka, v)

        def chunked(m):
            a = m[..., :d_ff].reshape(m.shape[:-1] + (n_chunks, FF_CHUNK))
            g = m[..., d_ff:].reshape(m.shape[:-1] + (n_chunks, FF_CHUNK))
            return jnp.moveaxis(jnp.concatenate([a, g], axis=-1), -2, 0)
        wup = chunked(w_up[layer]).astype(BF16)
        cw = chunked(conv_w[layer])
        cb = chunked(conv_b[layer][None, :])
        h = _ffn_call(h, att, sg, w_out[layer].astype(BF16), norm_ffn_g[layer][None, :], wup, cw, cb,
                      w_down[layer].astype(BF16), norm_final_g[None, :], layer == depth - 1)
    return h
```

```python
import functools

import jax
import jax.numpy as jnp
from jax import lax
from jax.experimental import pallas as pl
from jax.experimental.pallas import tpu as pltpu

EPS = 1e-6
HEAD_DIM = 64
GROUP_DIM = 64
CHUNK = 128
CONV_WIDTH = 3

LANES = 128
SUBLANES = 8
MXU_DIM = 256
VMEM_LIMIT_BYTES = 56 * 1024 * 1024

AUG_W = 8
TOKEN_TILE = 512
ATT_TILE = 512
FF_CHUNK = 256

F32 = jnp.float32
BF16 = jnp.bfloat16


def _split3(x):
    hi = x.astype(BF16)
    r1 = x - hi.astype(F32)
    mid = r1.astype(BF16)
    lo = (r1 - mid.astype(F32)).astype(BF16)
    return hi, mid, lo


def _const_spec(shape):
    zeros = (0,) * len(shape)
    return pl.BlockSpec(shape, lambda *_: zeros, pipeline_mode=pl.Buffered(1))


def _proj_kernel(x_ref, g_ref, wqkv_ref, wuv_ref, wg_ref, bf_ref, gn_ref, ws_ref, bs_ref,
                 gsum_ref, q_ref, k_ref, v_ref, qa_ref, ka_ref, sg_ref, carry_ref, *, att_w, gm_w):
    tm = x_ref.shape[1]

    @pl.when(pl.program_id(1) == 0)
    def _():
        carry_ref[...] = jnp.zeros_like(carry_ref)

    x = x_ref[0]
    ms = jnp.mean(x * x, axis=-1, keepdims=True)
    xn = ((x * lax.rsqrt(ms + EPS)) * g_ref[...]).astype(BF16)

    qkv = jnp.dot(xn, wqkv_ref[...], preferred_element_type=F32)
    q_ref[0] = qkv[:, :att_w].astype(BF16)
    k_ref[0] = qkv[:, att_w:2 * att_w].astype(BF16)
    v_ref[0] = qkv[:, 2 * att_w:].astype(BF16)

    z = jnp.dot(xn, wg_ref[...], preferred_element_type=F32) + bf_ref[...]
    log_f = -(jnp.maximum(-z, 0.0) + jnp.log1p(jnp.exp(-jnp.abs(z))))

    row = lax.broadcasted_iota(jnp.int32, (CHUNK, CHUNK), 0)
    col = lax.broadcasted_iota(jnp.int32, (CHUNK, CHUNK), 1)
    causal = row >= col
    tri = jnp.where(causal, 1.0, 0.0).astype(BF16)
    offset = carry_ref[0:1, :]
    c_blocks = []
    for r in range(tm // CHUNK):
        blk = log_f[r * CHUNK:(r + 1) * CHUNK]
        s = offset
        for piece in _split3(blk):
            s = s + jnp.dot(tri, piece, preferred_element_type=F32)
        c_blocks.append(s)
        offset = s[CHUNK - 1:CHUNK, :]
    carry_ref[0:1, :] = offset
    c = jnp.concatenate(c_blocks, axis=0)

    c_hi, c_mid, c_lo = [piece.astype(F32) for piece in _split3(c)]
    slot = lax.broadcasted_iota(jnp.int32, (tm, LANES), 1) % AUG_W
    qa = jnp.where(slot == 0, c_hi, jnp.where(slot == 1, c_mid, jnp.where(slot == 2, c_lo,
         jnp.where(slot < 6, 1.0, 0.0))))
    ka = jnp.where(slot < 3, 1.0, jnp.where(slot == 3, -c_hi, jnp.where(slot == 4, -c_mid,
         jnp.where(slot == 5, -c_lo, 0.0))))
    qa_ref[0] = qa.astype(BF16)
    ka_ref[0] = ka.astype(BF16)

    uv = jax.nn.gelu(jnp.dot(xn, wuv_ref[...], preferred_element_type=F32))
    u = uv[:, :gm_w]
    vg = uv[:, gm_w:]
    sq = (vg * vg).astype(BF16)
    gsum = gsum_ref[...]
    msg = jnp.concatenate(
        [jnp.dot(sq[:, s * MXU_DIM:(s + 1) * MXU_DIM], gsum, preferred_element_type=F32)
         for s in range(gm_w // MXU_DIM)], axis=1)
    vgn = ((vg * lax.rsqrt(msg + EPS)) * gn_ref[...]).astype(BF16)

    groups_per_slab = MXU_DIM // GROUP_DIM
    lane_grp = lax.broadcasted_iota(jnp.int32, (CHUNK, MXU_DIM), 1) // GROUP_DIM
    w_masked = [jnp.where(causal, ws_ref[g], 0.0).astype(BF16) for g in range(ws_ref.shape[0])]
    mixed_rows = []
    for r in range(tm // CHUNK):
        slabs = []
        for s in range(gm_w // MXU_DIM):
            vs = vgn[r * CHUNK:(r + 1) * CHUNK, s * MXU_DIM:(s + 1) * MXU_DIM]
            m = None
            for j in range(groups_per_slab):
                mj = jnp.dot(w_masked[s * groups_per_slab + j], vs, preferred_element_type=F32)
                m = mj if m is None else jnp.where(lane_grp == j, mj, m)
            slabs.append(m)
        mixed_rows.append(jnp.concatenate(slabs, axis=1) + bs_ref[...])
    mixed = jnp.concatenate(mixed_rows, axis=0)
    sg_ref[0] = (u * mixed).astype(BF16)


def _proj_call(x, g, wqkv, wuv, wg, bfr, gn, ws, bs, gsum):
    b, s, d = x.shape
    tm = TOKEN_TILE
    att_w = wqkv.shape[1] // 3
    gm_w = wuv.shape[1] // 2
    tile = lambda w: pl.BlockSpec((1, tm, w), lambda i, j: (i, j, 0))
    out_shape = ([jax.ShapeDtypeStruct((b, s, att_w), BF16)] * 3
                 + [jax.ShapeDtypeStruct((b, s, LANES), BF16)] * 2
                 + [jax.ShapeDtypeStruct((b, s, gm_w), BF16)])
    return pl.pallas_call(
        functools.partial(_proj_kernel, att_w=att_w, gm_w=gm_w),
        grid=(b, s // tm),
        in_specs=[tile(d), _const_spec(g.shape), _const_spec(wqkv.shape), _const_spec(wuv.shape),
                  _const_spec(wg.shape), _const_spec(bfr.shape), _const_spec(gn.shape),
                  _const_spec(ws.shape), _const_spec(bs.shape), _const_spec(gsum.shape)],
        out_specs=[tile(att_w)] * 3 + [tile(LANES)] * 2 + [tile(gm_w)],
        out_shape=out_shape,
        scratch_shapes=[pltpu.VMEM((SUBLANES, LANES), F32)],
        compiler_params=pltpu.CompilerParams(
            dimension_semantics=("arbitrary", "arbitrary"), vmem_limit_bytes=VMEM_LIMIT_BYTES),
        name="proj",
    )(x, g, wqkv, wuv, wg, bfr, gn, ws, bs, gsum)


def _attn_kernel(q_ref, qa_ref, k_ref, ka_ref, v_ref, o_ref, m_ref, acc_ref, sa_ref, sb_ref):
    t = q_ref.shape[1]
    pair = pl.program_id(1)
    qi = pl.program_id(2)
    heads = LANES // HEAD_DIM

    lane = lax.broadcasted_iota(jnp.int32, (t, LANES), 1)
    q = q_ref[0].astype(F32)
    qa = qa_ref[0].astype(F32)
    qf = []
    for i in range(heads):
        qm = jnp.where(lane // HEAD_DIM == i, q, 0.0)
        qam = jnp.where(lane // AUG_W == pair * heads + i, qa, 0.0)
        qf.append(jnp.concatenate([qm, qam], axis=1).astype(BF16))

    m_ref[...] = jnp.full_like(m_ref, -jnp.inf)
    acc_ref[...] = jnp.zeros_like(acc_ref)
    tk = sa_ref.shape[2]
    ones = jnp.ones((tk, LANES), BF16)
    row = lax.broadcasted_iota(jnp.int32, (t, tk), 0)
    col = lax.broadcasted_iota(jnp.int32, (t, tk), 1)

    def scores(tile, s_ref):
        start = pl.multiple_of(tile * tk, tk)
        kf = jnp.concatenate([k_ref[0, pl.ds(start, tk), :], ka_ref[0, pl.ds(start, tk), :]], axis=1)
        for i in range(heads):
            s_ref[i] = lax.dot_general(qf[i], kf, (((1,), (1,)), ((), ())),
                                       preferred_element_type=F32)

    def softmax_pv(tile, s_ref, diag_col0=None):
        start = pl.multiple_of(tile * tk, tk)
        vf = jnp.concatenate([v_ref[0, pl.ds(start, tk), :], ones], axis=1)
        for i in range(heads):
            s = s_ref[i]
            if diag_col0 is not None:
                s = jnp.where(row >= col + diag_col0, s, -jnp.inf)
            m_prev = m_ref[i]
            m_next = jnp.maximum(m_prev, jnp.max(s, axis=1, keepdims=True))
            alpha = jnp.exp(m_prev - m_next)
            p = jnp.exp(s - m_next[:, 0:1]).astype(BF16)
            pv = jnp.dot(p, vf, preferred_element_type=F32)
            acc_ref[i] = acc_ref[i] * jnp.concatenate([alpha, alpha], axis=1) + pv
            m_ref[i] = m_next

    scores(0, sa_ref)

    def body(i, carry):
        scores(2 * i + 1, sb_ref)
        softmax_pv(2 * i, sa_ref)
        scores(2 * i + 2, sa_ref)
        softmax_pv(2 * i + 1, sb_ref)
        return carry

    lax.fori_loop(0, qi, body, 0)
    scores(2 * qi + 1, sb_ref)
    softmax_pv(2 * qi, sa_ref, diag_col0=0)
    softmax_pv(2 * qi + 1, sb_ref, diag_col0=tk)

    out = None
    for i in range(heads):
        a = acc_ref[i]
        o = a[:, :LANES] / a[:, LANES:]
        out = o if out is None else jnp.where(lane // HEAD_DIM == i, o, out)
    o_ref[0] = out.astype(o_ref.dtype)


def _attn_call(q, qa, k, ka, v):
    b, s, w = q.shape
    t = ATT_TILE
    qspec = pl.BlockSpec((1, t, LANES), lambda i, p, j: (i, j, p))
    aspec = pl.BlockSpec((1, t, LANES), lambda i, p, j: (i, j, 0))
    kspec = pl.BlockSpec((1, s, LANES), lambda i, p, j: (i, 0, p))
    kaspec = pl.BlockSpec((1, s, LANES), lambda i, p, j: (i, 0, 0))
    return pl.pallas_call(
        _attn_kernel,
        grid=(b, w // LANES, s // t),
        in_specs=[qspec, aspec, kspec, kaspec, kspec],
        out_specs=qspec,
        out_shape=jax.ShapeDtypeStruct((b, s, w), BF16),
        scratch_shapes=[pltpu.VMEM((LANES // HEAD_DIM, t, LANES), F32),
                        pltpu.VMEM((LANES // HEAD_DIM, t, 2 * LANES), F32),
                        pltpu.VMEM((LANES // HEAD_DIM, t, t // 2), F32),
                        pltpu.VMEM((LANES // HEAD_DIM, t, t // 2), F32)],
        compiler_params=pltpu.CompilerParams(
            dimension_semantics=("arbitrary", "arbitrary", "arbitrary"),
            vmem_limit_bytes=VMEM_LIMIT_BYTES),
        name="attn",
    )(q, qa, k, ka, v)


def _ffn_kernel(x_ref, att_ref, sg_ref, wo_ref, g2_ref, wup_ref, cw_ref, cb_ref, wdn_ref, g3_ref,
                o_ref, hn_ref, hbuf_ref, act_ref, carry_ref, *, final_norm):
    tm = x_ref.shape[1]
    n_chunks = wup_ref.shape[0]
    halo = SUBLANES

    @pl.when(pl.program_id(1) == 0)
    def _():
        carry_ref[...] = jnp.zeros_like(carry_ref)

    mix = jnp.concatenate([att_ref[0], sg_ref[0]], axis=1)
    h1 = x_ref[0] + jnp.dot(mix, wo_ref[...], preferred_element_type=F32)
    ms = jnp.mean(h1 * h1, axis=-1, keepdims=True)
    hn_ref[...] = ((h1 * lax.rsqrt(ms + EPS)) * g2_ref[...]).astype(BF16)

    for c in range(n_chunks):
        hbuf_ref[0:halo, :] = carry_ref[c]
        hbuf_ref[halo:, :] = jnp.dot(hn_ref[...], wup_ref[c], preferred_element_type=F32)
        carry_ref[c] = hbuf_ref[tm:tm + halo, :]
        cw = cw_ref[c]
        y = cb_ref[c]
        for tap in range(CONV_WIDTH):
            shift = CONV_WIDTH - 1 - tap
            y = y + hbuf_ref[halo - shift:halo - shift + tm, :] * cw[tap:tap + 1, :]
        a = y[:, :FF_CHUNK]
        g = y[:, FF_CHUNK:]
        act_ref[:, c * FF_CHUNK:(c + 1) * FF_CHUNK] = (jax.nn.silu(g) * a).astype(BF16)

    h2 = h1 + jnp.dot(act_ref[...], wdn_ref[...], preferred_element_type=F32)
    if final_norm:
        ms2 = jnp.mean(h2 * h2, axis=-1, keepdims=True)
        h2 = (h2 * lax.rsqrt(ms2 + EPS)) * g3_ref[...]
    o_ref[0] = h2


def _ffn_call(x, att, sg, wo, g2, wup, cw, cb, wdn, g3, final_norm):
    b, s, d = x.shape
    tm = TOKEN_TILE
    n_chunks = wup.shape[0]
    d_ff = wdn.shape[0]
    tile = lambda w: pl.BlockSpec((1, tm, w), lambda i, j: (i, j, 0))
    return pl.pallas_call(
        functools.partial(_ffn_kernel, final_norm=final_norm),
        grid=(b, s // tm),
        in_specs=[tile(d), tile(att.shape[2]), tile(sg.shape[2]), _const_spec(wo.shape),
                  _const_spec(g2.shape), _const_spec(wup.shape), _const_spec(cw.shape),
                  _const_spec(cb.shape), _const_spec(wdn.shape), _const_spec(g3.shape)],
        out_specs=tile(d),
        out_shape=jax.ShapeDtypeStruct((b, s, d), F32),
        scratch_shapes=[pltpu.VMEM((tm, d), BF16),
                        pltpu.VMEM((tm + SUBLANES, 2 * FF_CHUNK), F32),
                        pltpu.VMEM((tm, d_ff), BF16),
                        pltpu.VMEM((n_chunks, SUBLANES, 2 * FF_CHUNK), F32)],
        compiler_params=pltpu.CompilerParams(
            dimension_semantics=("arbitrary", "arbitrary"), vmem_limit_bytes=VMEM_LIMIT_BYTES),
        name="ffn",
    )(x, att, sg, wo, g2, wup, cw, cb, wdn, g3)


def kernel(x, norm_mix_g, w_in, b_forget, gmlp_norm_g, w_spatial, b_spatial, w_out, norm_ffn_g,
           w_up, conv_w, conv_b, w_down, norm_final_g):
    depth, d_model, _ = w_in.shape
    n_heads = b_forget.shape[1]
    att_w = n_heads * HEAD_DIM
    gm_w = gmlp_norm_g.shape[1]
    n_groups = w_spatial.shape[1]
    d_ff = w_down.shape[1]
    assert gm_w == n_groups * GROUP_DIM and w_spatial.shape[2] == CHUNK
    assert n_heads * AUG_W <= LANES and att_w % LANES == 0 and gm_w % MXU_DIM == 0
    assert d_ff % FF_CHUNK == 0 and x.shape[1] % TOKEN_TILE == 0 and x.shape[1] % ATT_TILE == 0
    n_chunks = d_ff // FF_CHUNK
    scale = HEAD_DIM ** -0.5

    gi = jnp.arange(MXU_DIM) // GROUP_DIM
    gsum = jnp.where(gi[:, None] == gi[None, :], 1.0 / GROUP_DIM, 0.0).astype(BF16)

    h = x
    for layer in range(depth):
        w = w_in[layer]
        wq = w[:, :att_w] * scale
        wqkv = jnp.concatenate([wq, w[:, att_w:3 * att_w]], axis=1).astype(BF16)
        wuv = w[:, 3 * att_w:3 * att_w + 2 * gm_w].astype(BF16)
        wgate = w[:, 3 * att_w + 2 * gm_w:]
        wg = jnp.zeros((d_model, LANES), F32).at[:, :n_heads * AUG_W].set(
            jnp.repeat(wgate, AUG_W, axis=1)).astype(BF16)
        bfr = jnp.zeros((1, LANES), F32).at[0, :n_heads * AUG_W].set(
            jnp.repeat(b_forget[layer], AUG_W))
        bs = jnp.repeat(b_spatial[layer].T, GROUP_DIM, axis=1)

        q, k, v, qa, ka, sg = _proj_call(
            h, norm_mix_g[layer][None, :], wqkv, wuv, wg, bfr, gmlp_norm_g[layer][None, :],
            w_spatial[layer], bs, gsum)
        att = _attn_call(q, qa, k, ka, v)

        def chunked(m):
            a = m[..., :d_ff].reshape(m.shape[:-1] + (n_chunks, FF_CHUNK))
            g = m[..., d_ff:].reshape(m.shape[:-1] + (n_chunks, FF_CHUNK))
            return jnp.moveaxis(jnp.concatenate([a, g], axis=-1), -2, 0)
        wup = chunked(w_up[layer]).astype(BF16)
        cw = chunked(conv_w[layer])
        cb = chunked(conv_b[layer][None, :])
        h = _ffn_call(h, att, sg, w_out[layer].astype(BF16), norm_ffn_g[layer][None, :], wup, cw, cb,
                      w_down[layer].astype(BF16), norm_final_g[None, :], layer == depth - 1)
    return h
```

```python
import functools

import jax
import jax.numpy as jnp
from jax import lax
from jax.experimental import pallas as pl
from jax.experimental.pallas import tpu as pltpu

EPS = 1e-6
HEAD_DIM = 64
GROUP_DIM = 64
CHUNK = 128
CONV_WIDTH = 3

LANES = 128
SUBLANES = 8
MXU_DIM = 256
VMEM_LIMIT_BYTES = 56 * 1024 * 1024

AUG_W = 8
TOKEN_TILE = 512
ATT_TILE = 512
FF_CHUNK = 256

F32 = jnp.float32
BF16 = jnp.bfloat16


def _split3(x):
    hi = x.astype(BF16)
    r1 = x - hi.astype(F32)
    mid = r1.astype(BF16)
    lo = (r1 - mid.astype(F32)).astype(BF16)
    return hi, mid, lo


def _const_spec(shape):
    zeros = (0,) * len(shape)
    return pl.BlockSpec(shape, lambda *_: zeros, pipeline_mode=pl.Buffered(1))


def _proj_kernel(x_ref, g_ref, wqkv_ref, wuv_ref, wg_ref, bf_ref, gn_ref, ws_ref, bs_ref,
                 gsum_ref, q_ref, k_ref, v_ref, qa_ref, ka_ref, sg_ref, carry_ref, *, att_w, gm_w):
    tm = x_ref.shape[1]

    @pl.when(pl.program_id(1) == 0)
    def _():
        carry_ref[...] = jnp.zeros_like(carry_ref)

    x = x_ref[0]
    ms = jnp.mean(x * x, axis=-1, keepdims=True)
    xn = ((x * lax.rsqrt(ms + EPS)) * g_ref[...]).astype(BF16)

    qkv = jnp.dot(xn, wqkv_ref[...], preferred_element_type=F32)
    q_ref[0] = qkv[:, :att_w].astype(BF16)
    k_ref[0] = qkv[:, att_w:2 * att_w].astype(BF16)
    v = qkv[:, 2 * att_w:]
    head_lane = lax.broadcasted_iota(jnp.int32, (tm, LANES), 1)
    for h in range(att_w // HEAD_DIM):
        blk = v[:, (h // 2) * LANES:(h // 2 + 1) * LANES]
        if h % 2:
            blk = pltpu.roll(blk, HEAD_DIM, axis=1)
        v_ref[0, :, h * LANES:(h + 1) * LANES] = jnp.where(head_lane < HEAD_DIM, blk, 1.0).astype(BF16)

    z = jnp.dot(xn, wg_ref[...], preferred_element_type=F32) + bf_ref[...]
    log_f = -(jnp.maximum(-z, 0.0) + jnp.log1p(jnp.exp(-jnp.abs(z))))

    row = lax.broadcasted_iota(jnp.int32, (CHUNK, CHUNK), 0)
    col = lax.broadcasted_iota(jnp.int32, (CHUNK, CHUNK), 1)
    causal = row >= col
    tri = jnp.where(causal, 1.0, 0.0).astype(BF16)
    offset = carry_ref[0:1, :]
    c_blocks = []
    for r in range(tm // CHUNK):
        blk = log_f[r * CHUNK:(r + 1) * CHUNK]
        s = offset
        for piece in _split3(blk):
            s = s + jnp.dot(tri, piece, preferred_element_type=F32)
        c_blocks.append(s)
        offset = s[CHUNK - 1:CHUNK, :]
    carry_ref[0:1, :] = offset
    c = jnp.concatenate(c_blocks, axis=0)

    c_hi, c_mid, c_lo = [piece.astype(F32) for piece in _split3(c)]
    slot = lax.broadcasted_iota(jnp.int32, (tm, LANES), 1) % AUG_W
    qa = jnp.where(slot == 0, c_hi, jnp.where(slot == 1, c_mid, jnp.where(slot == 2, c_lo,
         jnp.where(slot < 6, 1.0, 0.0))))
    ka = jnp.where(slot < 3, 1.0, jnp.where(slot == 3, -c_hi, jnp.where(slot == 4, -c_mid,
         jnp.where(slot == 5, -c_lo, 0.0))))
    qa_ref[0] = qa.astype(BF16)
    ka_ref[0] = ka.astype(BF16)

    uv = jax.nn.gelu(jnp.dot(xn, wuv_ref[...], preferred_element_type=F32))
    u = uv[:, :gm_w]
    vg = uv[:, gm_w:]
    sq = (vg * vg).astype(BF16)
    gsum = gsum_ref[...]
    msg = jnp.concatenate(
        [jnp.dot(sq[:, s * MXU_DIM:(s + 1) * MXU_DIM], gsum, preferred_element_type=F32)
         for s in range(gm_w // MXU_DIM)], axis=1)
    vgn = ((vg * lax.rsqrt(msg + EPS)) * gn_ref[...]).astype(BF16)

    groups_per_slab = MXU_DIM // GROUP_DIM
    lane_grp = lax.broadcasted_iota(jnp.int32, (CHUNK, MXU_DIM), 1) // GROUP_DIM
    w_masked = [jnp.where(causal, ws_ref[g], 0.0).astype(BF16) for g in range(ws_ref.shape[0])]
    mixed_rows = []
    for r in range(tm // CHUNK):
        slabs = []
        for s in range(gm_w // MXU_DIM):
            vs = vgn[r * CHUNK:(r + 1) * CHUNK, s * MXU_DIM:(s + 1) * MXU_DIM]
            m = None
            for j in range(groups_per_slab):
                mj = jnp.dot(w_masked[s * groups_per_slab + j], vs, preferred_element_type=F32)
                m = mj if m is None else jnp.where(lane_grp == j, mj, m)
            slabs.append(m)
        mixed_rows.append(jnp.concatenate(slabs, axis=1) + bs_ref[...])
    mixed = jnp.concatenate(mixed_rows, axis=0)
    sg_ref[0] = (u * mixed).astype(BF16)


def _proj_call(x, g, wqkv, wuv, wg, bfr, gn, ws, bs, gsum):
    b, s, d = x.shape
    tm = TOKEN_TILE
    att_w = wqkv.shape[1] // 3
    gm_w = wuv.shape[1] // 2
    tile = lambda w: pl.BlockSpec((1, tm, w), lambda i, j: (i, j, 0))
    out_shape = ([jax.ShapeDtypeStruct((b, s, att_w), BF16)] * 2
                 + [jax.ShapeDtypeStruct((b, s, 2 * att_w), BF16)]
                 + [jax.ShapeDtypeStruct((b, s, LANES), BF16)] * 2
                 + [jax.ShapeDtypeStruct((b, s, gm_w), BF16)])
    return pl.pallas_call(
        functools.partial(_proj_kernel, att_w=att_w, gm_w=gm_w),
        grid=(b, s // tm),
        in_specs=[tile(d), _const_spec(g.shape), _const_spec(wqkv.shape), _const_spec(wuv.shape),
                  _const_spec(wg.shape), _const_spec(bfr.shape), _const_spec(gn.shape),
                  _const_spec(ws.shape), _const_spec(bs.shape), _const_spec(gsum.shape)],
        out_specs=[tile(att_w)] * 2 + [tile(2 * att_w)] + [tile(LANES)] * 2 + [tile(gm_w)],
        out_shape=out_shape,
        scratch_shapes=[pltpu.VMEM((SUBLANES, LANES), F32)],
        compiler_params=pltpu.CompilerParams(
            dimension_semantics=("arbitrary", "arbitrary"), vmem_limit_bytes=VMEM_LIMIT_BYTES),
        name="proj",
    )(x, g, wqkv, wuv, wg, bfr, gn, ws, bs, gsum)


def _attn_kernel(q_ref, qa_ref, k_ref, ka_ref, v_ref, o_ref, m_ref, acc_ref, qf_ref, s_ref):
    t = q_ref.shape[1]
    pair = pl.program_id(1)
    qi = pl.program_id(2)
    heads = LANES // HEAD_DIM

    lane = lax.broadcasted_iota(jnp.int32, (t, LANES), 1)
    q = q_ref[0].astype(F32)
    qa = qa_ref[0].astype(F32)
    for i in range(heads):
        qm = jnp.where(lane // HEAD_DIM == i, q, 0.0)
        qam = jnp.where(lane // AUG_W == pair * heads + i, qa, 0.0)
        qf_ref[i] = jnp.concatenate([qm, qam], axis=1).astype(BF16)

    m_ref[...] = jnp.full_like(m_ref, -jnp.inf)
    acc_ref[...] = jnp.zeros_like(acc_ref)
    row = lax.broadcasted_iota(jnp.int32, (t, t), 0)
    col = lax.broadcasted_iota(jnp.int32, (t, t), 1)

    def scores(tile, par):
        start = pl.multiple_of(tile * t, t)
        kf = jnp.concatenate([k_ref[0, pl.ds(start, t), :], ka_ref[0, pl.ds(start, t), :]], axis=1)
        for i in range(heads):
            s_ref[par, i] = lax.dot_general(qf_ref[i], kf, (((1,), (1,)), ((), ())),
                                            preferred_element_type=F32)

    def softmax_pv(tile, par, diagonal=False):
        start = pl.multiple_of(tile * t, t)
        for i in range(heads):
            vf = v_ref[0, pl.ds(start, t), i * LANES:(i + 1) * LANES]
            s = s_ref[par, i]
            if diagonal:
                s = jnp.where(row >= col, s, -jnp.inf)
            m_prev = m_ref[i]
            m_next = jnp.maximum(m_prev, jnp.max(s, axis=1, keepdims=True))
            alpha = jnp.exp(m_prev - m_next)
            p = jnp.exp(s - jnp.concatenate([m_next] * (t // LANES), axis=1)).astype(BF16)
            pv = jnp.dot(p, vf, preferred_element_type=F32)
            acc_ref[i] = acc_ref[i] * alpha + pv
            m_ref[i] = m_next

    scores(0, 0)
    n_pairs = qi // 2

    def body(i, carry):
        scores(2 * i + 1, 1)
        softmax_pv(2 * i, 0)
        scores(2 * i + 2, 0)
        softmax_pv(2 * i + 1, 1)
        return carry

    lax.fori_loop(0, n_pairs, body, 0)

    @pl.when(qi % 2 == 1)
    def _():
        scores(qi, 1)
        softmax_pv(qi - 1, 0)
        softmax_pv(qi, 1, diagonal=True)

    @pl.when(qi % 2 == 0)
    def _():
        softmax_pv(qi, 0, diagonal=True)


    assert heads == 2
    o0, o1 = [acc_ref[i] / pltpu.roll(acc_ref[i], HEAD_DIM, axis=1) for i in range(heads)]
    out = jnp.where(lane < HEAD_DIM, o0, pltpu.roll(o1, HEAD_DIM, axis=1))
    o_ref[0] = out.astype(o_ref.dtype)


def _attn_call(q, qa, k, ka, v):
    b, s, w = q.shape
    t = ATT_TILE
    heads = LANES // HEAD_DIM
    qspec = pl.BlockSpec((1, t, LANES), lambda i, p, j: (i, j, p))
    aspec = pl.BlockSpec((1, t, LANES), lambda i, p, j: (i, j, 0))
    kspec = pl.BlockSpec((1, s, LANES), lambda i, p, j: (i, 0, p))
    kaspec = pl.BlockSpec((1, s, LANES), lambda i, p, j: (i, 0, 0))
    vspec = pl.BlockSpec((1, s, 2 * LANES), lambda i, p, j: (i, 0, p))
    return pl.pallas_call(
        _attn_kernel,
        grid=(b, w // LANES, s // t),
        in_specs=[qspec, aspec, kspec, kaspec, vspec],
        out_specs=qspec,
        out_shape=jax.ShapeDtypeStruct((b, s, w), BF16),
        scratch_shapes=[pltpu.VMEM((heads, t, LANES), F32),
                        pltpu.VMEM((heads, t, LANES), F32),
                        pltpu.VMEM((heads, t, 2 * LANES), BF16),
                        pltpu.VMEM((2, heads, t, t), F32)],
        compiler_params=pltpu.CompilerParams(
            dimension_semantics=("arbitrary", "arbitrary", "arbitrary"),
            vmem_limit_bytes=VMEM_LIMIT_BYTES),
        name="attn",
    )(q, qa, k, ka, v)


def _ffn_kernel(x_ref, att_ref, sg_ref, wo_ref, g2_ref, wup_ref, cw_ref, cb_ref, wdn_ref, g3_ref,
                o_ref, hn_ref, hbuf_ref, act_ref, carry_ref, *, final_norm):
    tm = x_ref.shape[1]
    n_chunks = wup_ref.shape[0]
    halo = SUBLANES

    @pl.when(pl.program_id(1) == 0)
    def _():
        carry_ref[...] = jnp.zeros_like(carry_ref)

    mix = jnp.concatenate([att_ref[0], sg_ref[0]], axis=1)
    h1 = x_ref[0] + jnp.dot(mix, wo_ref[...], preferred_element_type=F32)
    ms = jnp.mean(h1 * h1, axis=-1, keepdims=True)
    hn_ref[...] = ((h1 * lax.rsqrt(ms + EPS)) * g2_ref[...]).astype(BF16)

    for c in range(n_chunks):
        hbuf_ref[0:halo, :] = carry_ref[c]
        hbuf_ref[halo:, :] = jnp.dot(hn_ref[...], wup_ref[c], preferred_element_type=F32)
        carry_ref[c] = hbuf_ref[tm:tm + halo, :]
        cw = cw_ref[c]
        y = cb_ref[c]
        for tap in range(CONV_WIDTH):
            shift = CONV_WIDTH - 1 - tap
            y = y + hbuf_ref[halo - shift:halo - shift + tm, :] * cw[tap:tap + 1, :]
        a = y[:, :FF_CHUNK]
        g = y[:, FF_CHUNK:]
        act_ref[:, c * FF_CHUNK:(c + 1) * FF_CHUNK] = (jax.nn.silu(g) * a).astype(BF16)

    h2 = h1 + jnp.dot(act_ref[...], wdn_ref[...], preferred_element_type=F32)
    if final_norm:
        ms2 = jnp.mean(h2 * h2, axis=-1, keepdims=True)
        h2 = (h2 * lax.rsqrt(ms2 + EPS)) * g3_ref[...]
    o_ref[0] = h2


def _ffn_call(x, att, sg, wo, g2, wup, cw, cb, wdn, g3, final_norm):
    b, s, d = x.shape
    tm = TOKEN_TILE
    n_chunks = wup.shape[0]
    d_ff = wdn.shape[0]
    tile = lambda w: pl.BlockSpec((1, tm, w), lambda i, j: (i, j, 0))
    return pl.pallas_call(
        functools.partial(_ffn_kernel, final_norm=final_norm),
        grid=(b, s // tm),
        in_specs=[tile(d), tile(att.shape[2]), tile(sg.shape[2]), _const_spec(wo.shape),
                  _const_spec(g2.shape), _const_spec(wup.shape), _const_spec(cw.shape),
                  _const_spec(cb.shape), _const_spec(wdn.shape), _const_spec(g3.shape)],
        out_specs=tile(d),
        out_shape=jax.ShapeDtypeStruct((b, s, d), F32),
        scratch_shapes=[pltpu.VMEM((tm, d), BF16),
                        pltpu.VMEM((tm + SUBLANES, 2 * FF_CHUNK), F32),
                        pltpu.VMEM((tm, d_ff), BF16),
                        pltpu.VMEM((n_chunks, SUBLANES, 2 * FF_CHUNK), F32)],
        compiler_params=pltpu.CompilerParams(
            dimension_semantics=("arbitrary", "arbitrary"), vmem_limit_bytes=VMEM_LIMIT_BYTES),
        name="ffn",
    )(x, att, sg, wo, g2, wup, cw, cb, wdn, g3)


def kernel(x, norm_mix_g, w_in, b_forget, gmlp_norm_g, w_spatial, b_spatial, w_out, norm_ffn_g,
           w_up, conv_w, conv_b, w_down, norm_final_g):
    depth, d_model, _ = w_in.shape
    n_heads = b_forget.shape[1]
    att_w = n_heads * HEAD_DIM
    gm_w = gmlp_norm_g.shape[1]
    n_groups = w_spatial.shape[1]
    d_ff = w_down.shape[1]
    assert gm_w == n_groups * GROUP_DIM and w_spatial.shape[2] == CHUNK
    assert n_heads * AUG_W <= LANES and att_w % LANES == 0 and gm_w % MXU_DIM == 0
    assert d_ff % FF_CHUNK == 0 and x.shape[1] % TOKEN_TILE == 0 and x.shape[1] % ATT_TILE == 0
    n_chunks = d_ff // FF_CHUNK
    scale = HEAD_DIM ** -0.5

    gi = jnp.arange(MXU_DIM) // GROUP_DIM
    gsum = jnp.where(gi[:, None] == gi[None, :], 1.0 / GROUP_DIM, 0.0).astype(BF16)

    h = x
    for layer in range(depth):
        w = w_in[layer]
        wq = w[:, :att_w] * scale
        wqkv = jnp.concatenate([wq, w[:, att_w:3 * att_w]], axis=1).astype(BF16)
        wuv = w[:, 3 * att_w:3 * att_w + 2 * gm_w].astype(BF16)
        wgate = w[:, 3 * att_w + 2 * gm_w:]
        wg = jnp.zeros((d_model, LANES), F32).at[:, :n_heads * AUG_W].set(
            jnp.repeat(wgate, AUG_W, axis=1)).astype(BF16)
        bfr = jnp.zeros((1, LANES), F32).at[0, :n_heads * AUG_W].set(
            jnp.repeat(b_forget[layer], AUG_W))
        bs = jnp.repeat(b_spatial[layer].T, GROUP_DIM, axis=1)

        q, k, v, qa, ka, sg = _proj_call(
            h, norm_mix_g[layer][None, :], wqkv, wuv, wg, bfr, gmlp_norm_g[layer][None, :],
            w_spatial[layer], bs, gsum)
        att = _attn_call(q, qa, k, ka, v)

        def chunked(m):
            a = m[..., :d_ff].reshape(m.shape[:-1] + (n_chunks, FF_CHUNK))
            g = m[..., d_ff:].reshape(m.shape[:-1] + (n_chunks, FF_CHUNK))
            return jnp.moveaxis(jnp.concatenate([a, g], axis=-1), -2, 0)
        wup = chunked(w_up[layer]).astype(BF16)
        cw = chunked(conv_w[layer])
        cb = chunked(conv_b[layer][None, :])
        h = _ffn_call(h, att, sg, w_out[layer].astype(BF16), norm_ffn_g[layer][None, :], wup, cw, cb,
                      w_down[layer].astype(BF16), norm_final_g[None, :], layer == depth - 1)
    return h
```

```python
import functools

import jax
import jax.numpy as jnp
from jax import lax
from jax.experimental import pallas as pl
from jax.experimental.pallas import tpu as pltpu

EPS = 1e-6
LOG2E = 1.4426950408889634
HEAD_DIM = 64
GROUP_DIM = 64
CHUNK = 128
CONV_WIDTH = 3

LANES = 128
SUBLANES = 8
MXU_DIM = 256
VMEM_LIMIT_BYTES = 56 * 1024 * 1024

AUG_W = 8
TOKEN_TILE = 512
ATT_TILE = 512
FF_CHUNK = 256

F32 = jnp.float32
BF16 = jnp.bfloat16


def _split3(x):
    hi = x.astype(BF16)
    r1 = x - hi.astype(F32)
    mid = r1.astype(BF16)
    lo = (r1 - mid.astype(F32)).astype(BF16)
    return hi, mid, lo


def _const_spec(shape):
    zeros = (0,) * len(shape)
    return pl.BlockSpec(shape, lambda *_: zeros, pipeline_mode=pl.Buffered(1))


def _proj_kernel(x_ref, g_ref, wqkv_ref, wuv_ref, wg_ref, bf_ref, gn_ref, ws_ref, bs_ref,
                 gsum_ref, q_ref, k_ref, v_ref, qa_ref, ka_ref, sg_ref, carry_ref, *, att_w, gm_w):
    tm = x_ref.shape[1]

    @pl.when(pl.program_id(1) == 0)
    def _():
        carry_ref[...] = jnp.zeros_like(carry_ref)

    x = x_ref[0]
    ms = jnp.mean(x * x, axis=-1, keepdims=True)
    xn = ((x * lax.rsqrt(ms + EPS)) * g_ref[...]).astype(BF16)

    qkv = jnp.dot(xn, wqkv_ref[...], preferred_element_type=F32)
    q_ref[0] = (qkv[:, :att_w] * LOG2E).astype(BF16)
    k_ref[0] = qkv[:, att_w:2 * att_w].astype(BF16)
    v = qkv[:, 2 * att_w:]
    head_lane = lax.broadcasted_iota(jnp.int32, (tm, LANES), 1)
    for h in range(att_w // HEAD_DIM):
        blk = v[:, (h // 2) * LANES:(h // 2 + 1) * LANES]
        if h % 2:
            blk = pltpu.roll(blk, HEAD_DIM, axis=1)
        v_ref[0, :, h * LANES:(h + 1) * LANES] = jnp.where(head_lane < HEAD_DIM, blk, 1.0).astype(BF16)

    z = jnp.dot(xn, wg_ref[...], preferred_element_type=F32) + bf_ref[...]
    log_f = -(jnp.maximum(-z, 0.0) + jnp.log1p(jnp.exp(-jnp.abs(z))))

    row = lax.broadcasted_iota(jnp.int32, (CHUNK, CHUNK), 0)
    col = lax.broadcasted_iota(jnp.int32, (CHUNK, CHUNK), 1)
    causal = row >= col
    tri = jnp.where(causal, 1.0, 0.0).astype(BF16)
    offset = carry_ref[0:1, :]
    c_blocks = []
    for r in range(tm // CHUNK):
        blk = log_f[r * CHUNK:(r + 1) * CHUNK]
        s = offset
        for piece in _split3(blk):
            s = s + jnp.dot(tri, piece, preferred_element_type=F32)
        c_blocks.append(s)
        offset = s[CHUNK - 1:CHUNK, :]
    carry_ref[0:1, :] = offset
    c = jnp.concatenate(c_blocks, axis=0)

    c_hi, c_mid, c_lo = [piece.astype(F32) for piece in _split3(c * LOG2E)]
    slot = lax.broadcasted_iota(jnp.int32, (tm, LANES), 1) % AUG_W
    qa = jnp.where(slot == 0, c_hi, jnp.where(slot == 1, c_mid, jnp.where(slot == 2, c_lo,
         jnp.where(slot < 6, 1.0, 0.0))))
    ka = jnp.where(slot < 3, 1.0, jnp.where(slot == 3, -c_hi, jnp.where(slot == 4, -c_mid,
         jnp.where(slot == 5, -c_lo, 0.0))))
    qa_ref[0] = qa.astype(BF16)
    ka_ref[0] = ka.astype(BF16)

    uv = jax.nn.gelu(jnp.dot(xn, wuv_ref[...], preferred_element_type=F32))
    u = uv[:, :gm_w]
    vg = uv[:, gm_w:]
    sq = (vg * vg).astype(BF16)
    gsum = gsum_ref[...]
    msg = jnp.concatenate(
        [jnp.dot(sq[:, s * MXU_DIM:(s + 1) * MXU_DIM], gsum, preferred_element_type=F32)
         for s in range(gm_w // MXU_DIM)], axis=1)
    vgn = ((vg * lax.rsqrt(msg + EPS)) * gn_ref[...]).astype(BF16)

    groups_per_slab = MXU_DIM // GROUP_DIM
    lane_grp = lax.broadcasted_iota(jnp.int32, (CHUNK, MXU_DIM), 1) // GROUP_DIM
    w_masked = [jnp.where(causal, ws_ref[g], 0.0).astype(BF16) for g in range(ws_ref.shape[0])]
    mixed_rows = []
    for r in range(tm // CHUNK):
        slabs = []
        for s in range(gm_w // MXU_DIM):
            vs = vgn[r * CHUNK:(r + 1) * CHUNK, s * MXU_DIM:(s + 1) * MXU_DIM]
            m = None
            for j in range(groups_per_slab):
                mj = jnp.dot(w_masked[s * groups_per_slab + j], vs, preferred_element_type=F32)
                m = mj if m is None else jnp.where(lane_grp == j, mj, m)
            slabs.append(m)
        mixed_rows.append(jnp.concatenate(slabs, axis=1) + bs_ref[...])
    mixed = jnp.concatenate(mixed_rows, axis=0)
    sg_ref[0] = (u * mixed).astype(BF16)


def _proj_call(x, g, wqkv, wuv, wg, bfr, gn, ws, bs, gsum):
    b, s, d = x.shape
    tm = TOKEN_TILE
    att_w = wqkv.shape[1] // 3
    gm_w = wuv.shape[1] // 2
    tile = lambda w: pl.BlockSpec((1, tm, w), lambda i, j: (i, j, 0))
    out_shape = ([jax.ShapeDtypeStruct((b, s, att_w), BF16)] * 2
                 + [jax.ShapeDtypeStruct((b, s, 2 * att_w), BF16)]
                 + [jax.ShapeDtypeStruct((b, s, LANES), BF16)] * 2
                 + [jax.ShapeDtypeStruct((b, s, gm_w), BF16)])
    return pl.pallas_call(
        functools.partial(_proj_kernel, att_w=att_w, gm_w=gm_w),
        grid=(b, s // tm),
        in_specs=[tile(d), _const_spec(g.shape), _const_spec(wqkv.shape), _const_spec(wuv.shape),
                  _const_spec(wg.shape), _const_spec(bfr.shape), _const_spec(gn.shape),
                  _const_spec(ws.shape), _const_spec(bs.shape), _const_spec(gsum.shape)],
        out_specs=[tile(att_w)] * 2 + [tile(2 * att_w)] + [tile(LANES)] * 2 + [tile(gm_w)],
        out_shape=out_shape,
        scratch_shapes=[pltpu.VMEM((SUBLANES, LANES), F32)],
        compiler_params=pltpu.CompilerParams(
            dimension_semantics=("arbitrary", "arbitrary"), vmem_limit_bytes=VMEM_LIMIT_BYTES),
        name="proj",
    )(x, g, wqkv, wuv, wg, bfr, gn, ws, bs, gsum)


def _attn_kernel(q_ref, qa_ref, k_ref, ka_ref, v_ref, o_ref, m_ref, acc_ref, qf_ref, s_ref):
    t = q_ref.shape[1]
    pair = pl.program_id(1)
    qi = pl.program_id(2)
    heads = LANES // HEAD_DIM

    lane = lax.broadcasted_iota(jnp.int32, (t, LANES), 1)
    q = q_ref[0].astype(F32)
    qa = qa_ref[0].astype(F32)
    for i in range(heads):
        qm = jnp.where(lane // HEAD_DIM == i, q, 0.0)
        qam = jnp.where(lane // AUG_W == pair * heads + i, qa, 0.0)
        qf_ref[i] = jnp.concatenate([qm, qam], axis=1).astype(BF16)

    m_ref[...] = jnp.full_like(m_ref, -jnp.inf)
    acc_ref[...] = jnp.zeros_like(acc_ref)
    row = lax.broadcasted_iota(jnp.int32, (t, t), 0)
    col = lax.broadcasted_iota(jnp.int32, (t, t), 1)

    def scores(tile, par):
        start = pl.multiple_of(tile * t, t)
        kf = jnp.concatenate([k_ref[0, pl.ds(start, t), :], ka_ref[0, pl.ds(start, t), :]], axis=1)
        for i in range(heads):
            s_ref[par, i] = lax.dot_general(qf_ref[i], kf, (((1,), (1,)), ((), ())),
                                            preferred_element_type=F32)

    def softmax_pv(tile, par, diagonal=False):
        start = pl.multiple_of(tile * t, t)
        for i in range(heads):
            vf = v_ref[0, pl.ds(start, t), i * LANES:(i + 1) * LANES]
            s = s_ref[par, i]
            if diagonal:
                s = jnp.where(row >= col, s, -jnp.inf)
            m_prev = m_ref[i]
            m_next = jnp.maximum(m_prev, jnp.max(s, axis=1, keepdims=True))
            alpha = jnp.exp2(m_prev - m_next)
            p = jnp.exp2(s - jnp.concatenate([m_next] * (t // LANES), axis=1)).astype(BF16)
            pv = jnp.dot(p, vf, preferred_element_type=F32)
            acc_ref[i] = acc_ref[i] * alpha + pv
            m_ref[i] = m_next

    scores(0, 0)
    n_pairs = qi // 2

    def body(i, carry):
        scores(2 * i + 1, 1)
        softmax_pv(2 * i, 0)
        scores(2 * i + 2, 0)
        softmax_pv(2 * i + 1, 1)
        return carry

    lax.fori_loop(0, n_pairs, body, 0)

    @pl.when(qi % 2 == 1)
    def _():
        scores(qi, 1)
        softmax_pv(qi - 1, 0)
        softmax_pv(qi, 1, diagonal=True)

    @pl.when(qi % 2 == 0)
    def _():
        softmax_pv(qi, 0, diagonal=True)


    assert heads == 2
    o0, o1 = [acc_ref[i] / pltpu.roll(acc_ref[i], HEAD_DIM, axis=1) for i in range(heads)]
    out = jnp.where(lane < HEAD_DIM, o0, pltpu.roll(o1, HEAD_DIM, axis=1))
    o_ref[0] = out.astype(o_ref.dtype)


def _attn_call(q, qa, k, ka, v):
    b, s, w = q.shape
    t = ATT_TILE
    heads = LANES // HEAD_DIM
    qspec = pl.BlockSpec((1, t, LANES), lambda i, p, j: (i, j, p))
    aspec = pl.BlockSpec((1, t, LANES), lambda i, p, j: (i, j, 0))
    kspec = pl.BlockSpec((1, s, LANES), lambda i, p, j: (i, 0, p))
    kaspec = pl.BlockSpec((1, s, LANES), lambda i, p, j: (i, 0, 0))
    vspec = pl.BlockSpec((1, s, 2 * LANES), lambda i, p, j: (i, 0, p))
    return pl.pallas_call(
        _attn_kernel,
        grid=(b, w // LANES, s // t),
        in_specs=[qspec, aspec, kspec, kaspec, vspec],
        out_specs=qspec,
        out_shape=jax.ShapeDtypeStruct((b, s, w), BF16),
        scratch_shapes=[pltpu.VMEM((heads, t, LANES), F32),
                        pltpu.VMEM((heads, t, LANES), F32),
                        pltpu.VMEM((heads, t, 2 * LANES), BF16),
                        pltpu.VMEM((2, heads, t, t), F32)],
        compiler_params=pltpu.CompilerParams(
            dimension_semantics=("arbitrary", "arbitrary", "arbitrary"),
            vmem_limit_bytes=VMEM_LIMIT_BYTES),
        name="attn",
    )(q, qa, k, ka, v)


def _ffn_kernel(x_ref, att_ref, sg_ref, wo_ref, g2_ref, wup_ref, cw_ref, cb_ref, wdn_ref, g3_ref,
                o_ref, hn_ref, hbuf_ref, act_ref, acc_ref, carry_ref, *, final_norm):
    tm = x_ref.shape[1]
    n_chunks = wup_ref.shape[0]
    halo = SUBLANES

    @pl.when(pl.program_id(1) == 0)
    def _():
        carry_ref[...] = jnp.zeros_like(carry_ref)

    mix = jnp.concatenate([att_ref[0], sg_ref[0]], axis=1)
    h1 = x_ref[0] + jnp.dot(mix, wo_ref[...], preferred_element_type=F32)
    ms = jnp.mean(h1 * h1, axis=-1, keepdims=True)
    hn_ref[...] = ((h1 * lax.rsqrt(ms + EPS)) * g2_ref[...]).astype(BF16)

    acc_ref[...] = h1

    def up(c):
        hbuf = hbuf_ref.at[c % 2]
        hbuf[0:halo, :] = carry_ref[c]
        hbuf[halo:, :] = jnp.dot(hn_ref[...], wup_ref[c], preferred_element_type=F32)
        carry_ref[c] = hbuf[tm:tm + halo, :]

    def gate(c):
        hbuf = hbuf_ref.at[c % 2]
        cw = cw_ref[c]
        y = cb_ref[c]
        for tap in range(CONV_WIDTH):
            shift = CONV_WIDTH - 1 - tap
            y = y + hbuf[halo - shift:halo - shift + tm, :] * cw[tap:tap + 1, :]
        a = y[:, :FF_CHUNK]
        g = y[:, FF_CHUNK:]
        act_ref[:, c * FF_CHUNK:(c + 1) * FF_CHUNK] = (jax.nn.silu(g) * a).astype(BF16)

    up(0)
    for c in range(n_chunks):
        if c + 1 < n_chunks:
            up(c + 1)
        gate(c)

    h2 = acc_ref[...] + jnp.dot(act_ref[...], wdn_ref[...], preferred_element_type=F32)
    if final_norm:
        ms2 = jnp.mean(h2 * h2, axis=-1, keepdims=True)
        h2 = (h2 * lax.rsqrt(ms2 + EPS)) * g3_ref[...]
    o_ref[0] = h2


def _ffn_call(x, att, sg, wo, g2, wup, cw, cb, wdn, g3, final_norm):
    b, s, d = x.shape
    tm = TOKEN_TILE
    n_chunks = wup.shape[0]
    d_ff = wdn.shape[0]
    tile = lambda w: pl.BlockSpec((1, tm, w), lambda i, j: (i, j, 0))
    return pl.pallas_call(
        functools.partial(_ffn_kernel, final_norm=final_norm),
        grid=(b, s // tm),
        in_specs=[tile(d), tile(att.shape[2]), tile(sg.shape[2]), _const_spec(wo.shape),
                  _const_spec(g2.shape), _const_spec(wup.shape), _const_spec(cw.shape),
                  _const_spec(cb.shape), _const_spec(wdn.shape), _const_spec(g3.shape)],
        out_specs=tile(d),
        out_shape=jax.ShapeDtypeStruct((b, s, d), F32),
        scratch_shapes=[pltpu.VMEM((tm, d), BF16),
                        pltpu.VMEM((2, tm + SUBLANES, 2 * FF_CHUNK), F32),
                        pltpu.VMEM((tm, d_ff), BF16),
                        pltpu.VMEM((tm, d), F32),
                        pltpu.VMEM((n_chunks, SUBLANES, 2 * FF_CHUNK), F32)],
        compiler_params=pltpu.CompilerParams(
            dimension_semantics=("arbitrary", "arbitrary"), vmem_limit_bytes=VMEM_LIMIT_BYTES),
        name="ffn",
    )(x, att, sg, wo, g2, wup, cw, cb, wdn, g3)


def kernel(x, norm_mix_g, w_in, b_forget, gmlp_norm_g, w_spatial, b_spatial, w_out, norm_ffn_g,
           w_up, conv_w, conv_b, w_down, norm_final_g):
    depth, d_model, _ = w_in.shape
    n_heads = b_forget.shape[1]
    att_w = n_heads * HEAD_DIM
    gm_w = gmlp_norm_g.shape[1]
    n_groups = w_spatial.shape[1]
    d_ff = w_down.shape[1]
    assert gm_w == n_groups * GROUP_DIM and w_spatial.shape[2] == CHUNK
    assert n_heads * AUG_W <= LANES and att_w % LANES == 0 and gm_w % MXU_DIM == 0
    assert d_ff % FF_CHUNK == 0 and x.shape[1] % TOKEN_TILE == 0 and x.shape[1] % ATT_TILE == 0
    n_chunks = d_ff // FF_CHUNK
    scale = HEAD_DIM ** -0.5

    gi = jnp.arange(MXU_DIM) // GROUP_DIM
    gsum = jnp.where(gi[:, None] == gi[None, :], 1.0 / GROUP_DIM, 0.0).astype(BF16)

    h = x
    for layer in range(depth):
        w = w_in[layer]
        wq = w[:, :att_w] * scale
        wqkv = jnp.concatenate([wq, w[:, att_w:3 * att_w]], axis=1).astype(BF16)
        wuv = w[:, 3 * att_w:3 * att_w + 2 * gm_w].astype(BF16)
        wgate = w[:, 3 * att_w + 2 * gm_w:]
        wg = jnp.zeros((d_model, LANES), F32).at[:, :n_heads * AUG_W].set(
            jnp.repeat(wgate, AUG_W, axis=1)).astype(BF16)
        bfr = jnp.zeros((1, LANES), F32).at[0, :n_heads * AUG_W].set(
            jnp.repeat(b_forget[layer], AUG_W))
        bs = jnp.repeat(b_spatial[layer].T, GROUP_DIM, axis=1)

        q, k, v, qa, ka, sg = _proj_call(
            h, norm_mix_g[layer][None, :], wqkv, wuv, wg, bfr, gmlp_norm_g[layer][None, :],
            w_spatial[layer], bs, gsum)
        att = _attn_call(q, qa, k, ka, v)

        def chunked(m):
            a = m[..., :d_ff].reshape(m.shape[:-1] + (n_chunks, FF_CHUNK))
            g = m[..., d_ff:].reshape(m.shape[:-1] + (n_chunks, FF_CHUNK))
            return jnp.moveaxis(jnp.concatenate([a, g], axis=-1), -2, 0)
        wup = chunked(w_up[layer]).astype(BF16)
        cw = chunked(conv_w[layer])
        cb = chunked(conv_b[layer][None, :])
        h = _ffn_call(h, att, sg, w_out[layer].astype(BF16), norm_ffn_g[layer][None, :], wup, cw, cb,
                      w_down[layer].astype(BF16), norm_final_g[None, :], layer == depth - 1)
    return h
```

```python
import functools

import jax
import jax.numpy as jnp
from jax import lax
from jax.experimental import pallas as pl
from jax.experimental.pallas import tpu as pltpu

EPS = 1e-6
LOG2E = 1.4426950408889634
HEAD_DIM = 64
GROUP_DIM = 64
CHUNK = 128
CONV_WIDTH = 3

LANES = 128
SUBLANES = 8
MXU_DIM = 256
VMEM_LIMIT_BYTES = 56 * 1024 * 1024

AUG_W = 8
TOKEN_TILE = 512
ATT_TILE = 512
FF_CHUNK = 256

F32 = jnp.float32
BF16 = jnp.bfloat16


def _split3(x):
    hi = x.astype(BF16)
    r1 = x - hi.astype(F32)
    mid = r1.astype(BF16)
    lo = (r1 - mid.astype(F32)).astype(BF16)
    return hi, mid, lo


def _const_spec(shape):
    zeros = (0,) * len(shape)
    return pl.BlockSpec(shape, lambda *_: zeros, pipeline_mode=pl.Buffered(1))


def _proj_kernel(x_ref, g_ref, wqk_ref, wvt_ref, wuv_ref, wg_ref, bf_ref, gn_ref, ws_ref, bs_ref,
                 gsum_ref, q_ref, k_ref, vt_ref, qa_ref, ka_ref, sg_ref, carry_ref, *, att_w, gm_w):
    tm = x_ref.shape[1]

    @pl.when(pl.program_id(1) == 0)
    def _():
        carry_ref[...] = jnp.zeros_like(carry_ref)

    x = x_ref[0]
    ms = jnp.mean(x * x, axis=-1, keepdims=True)
    xn = ((x * lax.rsqrt(ms + EPS)) * g_ref[...]).astype(BF16)

    qk = jnp.dot(xn, wqk_ref[...], preferred_element_type=F32)
    q_ref[0] = (qk[:, :att_w] * LOG2E).astype(BF16)
    k_ref[0] = qk[:, att_w:].astype(BF16)
    vt = lax.dot_general(wvt_ref[...], xn, (((1,), (1,)), ((), ())), preferred_element_type=F32)
    for h in range(att_w // HEAD_DIM):
        vt_ref[0, h, :HEAD_DIM, :] = vt[h * HEAD_DIM:(h + 1) * HEAD_DIM].astype(BF16)
        vt_ref[0, h, HEAD_DIM:, :] = jnp.ones((LANES - HEAD_DIM, tm), BF16)

    z = jnp.dot(xn, wg_ref[...], preferred_element_type=F32) + bf_ref[...]
    log_f = -(jnp.maximum(-z, 0.0) + jnp.log1p(jnp.exp(-jnp.abs(z))))

    row = lax.broadcasted_iota(jnp.int32, (CHUNK, CHUNK), 0)
    col = lax.broadcasted_iota(jnp.int32, (CHUNK, CHUNK), 1)
    causal = row >= col
    tri = jnp.where(causal, 1.0, 0.0).astype(BF16)
    offset = carry_ref[0:1, :]
    c_blocks = []
    for r in range(tm // CHUNK):
        blk = log_f[r * CHUNK:(r + 1) * CHUNK]
        s = offset
        for piece in _split3(blk):
            s = s + jnp.dot(tri, piece, preferred_element_type=F32)
        c_blocks.append(s)
        offset = s[CHUNK - 1:CHUNK, :]
    carry_ref[0:1, :] = offset
    c = jnp.concatenate(c_blocks, axis=0)

    c_hi, c_mid, c_lo = [piece.astype(F32) for piece in _split3(c * LOG2E)]
    slot = lax.broadcasted_iota(jnp.int32, (tm, LANES), 1) % AUG_W
    qa = jnp.where(slot == 0, c_hi, jnp.where(slot == 1, c_mid, jnp.where(slot == 2, c_lo,
         jnp.where(slot < 6, 1.0, 0.0))))
    ka = jnp.where(slot < 3, 1.0, jnp.where(slot == 3, -c_hi, jnp.where(slot == 4, -c_mid,
         jnp.where(slot == 5, -c_lo, 0.0))))
    qa_ref[0] = qa.astype(BF16)
    ka_ref[0] = ka.astype(BF16)

    uv = jax.nn.gelu(jnp.dot(xn, wuv_ref[...], preferred_element_type=F32))
    u = uv[:, :gm_w]
    vg = uv[:, gm_w:]
    sq = (vg * vg).astype(BF16)
    gsum = gsum_ref[...]
    msg = jnp.concatenate(
        [jnp.dot(sq[:, s * MXU_DIM:(s + 1) * MXU_DIM], gsum, preferred_element_type=F32)
         for s in range(gm_w // MXU_DIM)], axis=1)
    vgn = ((vg * lax.rsqrt(msg + EPS)) * gn_ref[...]).astype(BF16)

    groups_per_slab = MXU_DIM // GROUP_DIM
    lane_grp = lax.broadcasted_iota(jnp.int32, (CHUNK, MXU_DIM), 1) // GROUP_DIM
    w_masked = [jnp.where(causal, ws_ref[g], 0.0).astype(BF16) for g in range(ws_ref.shape[0])]
    mixed_rows = []
    for r in range(tm // CHUNK):
        slabs = []
        for s in range(gm_w // MXU_DIM):
            vs = vgn[r * CHUNK:(r + 1) * CHUNK, s * MXU_DIM:(s + 1) * MXU_DIM]
            m = None
            for j in range(groups_per_slab):
                mj = jnp.dot(w_masked[s * groups_per_slab + j], vs, preferred_element_type=F32)
                m = mj if m is None else jnp.where(lane_grp == j, mj, m)
            slabs.append(m)
        mixed_rows.append(jnp.concatenate(slabs, axis=1) + bs_ref[...])
    mixed = jnp.concatenate(mixed_rows, axis=0)
    sg_ref[0] = (u * mixed).astype(BF16)


def _proj_call(x, g, wqk, wvt, wuv, wg, bfr, gn, ws, bs, gsum):
    b, s, d = x.shape
    tm = TOKEN_TILE
    att_w = wvt.shape[0]
    n_heads = att_w // HEAD_DIM
    gm_w = wuv.shape[1] // 2
    tile = lambda w: pl.BlockSpec((1, tm, w), lambda i, j: (i, j, 0))
    vt_spec = pl.BlockSpec((1, n_heads, LANES, tm), lambda i, j: (i, 0, 0, j))
    out_shape = ([jax.ShapeDtypeStruct((b, s, att_w), BF16)] * 2
                 + [jax.ShapeDtypeStruct((b, n_heads, LANES, s), BF16)]
                 + [jax.ShapeDtypeStruct((b, s, LANES), BF16)] * 2
                 + [jax.ShapeDtypeStruct((b, s, gm_w), BF16)])
    return pl.pallas_call(
        functools.partial(_proj_kernel, att_w=att_w, gm_w=gm_w),
        grid=(b, s // tm),
        in_specs=[tile(d), _const_spec(g.shape), _const_spec(wqk.shape), _const_spec(wvt.shape),
                  _const_spec(wuv.shape), _const_spec(wg.shape), _const_spec(bfr.shape),
                  _const_spec(gn.shape), _const_spec(ws.shape), _const_spec(bs.shape),
                  _const_spec(gsum.shape)],
        out_specs=[tile(att_w)] * 2 + [vt_spec] + [tile(LANES)] * 2 + [tile(gm_w)],
        out_shape=out_shape,
        scratch_shapes=[pltpu.VMEM((SUBLANES, LANES), F32)],
        compiler_params=pltpu.CompilerParams(
            dimension_semantics=("arbitrary", "arbitrary"), vmem_limit_bytes=VMEM_LIMIT_BYTES),
        name="proj",
    )(x, g, wqk, wvt, wuv, wg, bfr, gn, ws, bs, gsum)


def _attn_kernel(q_ref, qa_ref, k_ref, ka_ref, vt_ref, o_ref, m_ref, acc_ref, qf_ref, s_ref):
    t = q_ref.shape[1]
    pair = pl.program_id(1)
    qi = pl.program_id(2)
    heads = LANES // HEAD_DIM

    lane = lax.broadcasted_iota(jnp.int32, (t, LANES), 1)
    q = q_ref[0].astype(F32)
    qa = qa_ref[0].astype(F32)
    for i in range(heads):
        qm = jnp.where(lane // HEAD_DIM == i, q, 0.0)
        qam = jnp.where(lane // AUG_W == pair * heads + i, qa, 0.0)
        qf_ref[i] = jnp.concatenate([qm, qam], axis=1).astype(BF16)

    m_ref[...] = jnp.full_like(m_ref, -jnp.inf)
    acc_ref[...] = jnp.zeros_like(acc_ref)
    key = lax.broadcasted_iota(jnp.int32, (t, t), 0)
    qry = lax.broadcasted_iota(jnp.int32, (t, t), 1)

    def scores(tile, par):
        start = pl.multiple_of(tile * t, t)
        kf = jnp.concatenate([k_ref[0, pl.ds(start, t), :], ka_ref[0, pl.ds(start, t), :]], axis=1)
        for i in range(heads):
            s_ref[par, i] = lax.dot_general(kf, qf_ref[i], (((1,), (1,)), ((), ())),
                                            preferred_element_type=F32)

    def softmax_pv(tile, par, diagonal=False):
        start = pl.multiple_of(tile * t, t)
        for i in range(heads):
            vt = vt_ref[0, i, :, pl.ds(start, t)]
            s = s_ref[par, i]
            if diagonal:
                s = jnp.where(qry >= key, s, -jnp.inf)
            m_prev = m_ref[i]
            m_next = jnp.maximum(m_prev, jnp.max(s, axis=0, keepdims=True))
            alpha = jnp.exp2(m_prev - m_next)
            p = jnp.exp2(s - m_next).astype(BF16)
            pv = jnp.dot(vt, p, preferred_element_type=F32)
            acc_ref[i] = acc_ref[i] * alpha + pv
            m_ref[i] = m_next

    scores(0, 0)
    n_pairs = qi // 2

    def body(i, carry):
        scores(2 * i + 1, 1)
        softmax_pv(2 * i, 0)
        scores(2 * i + 2, 0)
        softmax_pv(2 * i + 1, 1)
        return carry

    lax.fori_loop(0, n_pairs, body, 0)

    @pl.when(qi % 2 == 1)
    def _():
        scores(qi, 1)
        softmax_pv(qi - 1, 0)
        softmax_pv(qi, 1, diagonal=True)

    @pl.when(qi % 2 == 0)
    def _():
        softmax_pv(qi, 0, diagonal=True)


    assert heads == 2
    a0, a1 = [acc_ref[i].T for i in range(heads)]
    o0 = a0 / pltpu.roll(a0, HEAD_DIM, axis=1)
    o1 = a1 / pltpu.roll(a1, HEAD_DIM, axis=1)
    out = jnp.where(lane < HEAD_DIM, o0, pltpu.roll(o1, HEAD_DIM, axis=1))
    o_ref[0] = out.astype(o_ref.dtype)


def _attn_call(q, qa, k, ka, v):
    b, s, w = q.shape
    t = ATT_TILE
    heads = LANES // HEAD_DIM
    qspec = pl.BlockSpec((1, t, LANES), lambda i, p, j: (i, j, p))
    aspec = pl.BlockSpec((1, t, LANES), lambda i, p, j: (i, j, 0))
    kspec = pl.BlockSpec((1, s, LANES), lambda i, p, j: (i, 0, p))
    kaspec = pl.BlockSpec((1, s, LANES), lambda i, p, j: (i, 0, 0))
    vspec = pl.BlockSpec((1, heads, LANES, s), lambda i, p, j: (i, p, 0, 0))
    return pl.pallas_call(
        _attn_kernel,
        grid=(b, w // LANES, s // t),
        in_specs=[qspec, aspec, kspec, kaspec, vspec],
        out_specs=qspec,
        out_shape=jax.ShapeDtypeStruct((b, s, w), BF16),
        scratch_shapes=[pltpu.VMEM((heads, 1, t), F32),
                        pltpu.VMEM((heads, LANES, t), F32),
                        pltpu.VMEM((heads, t, 2 * LANES), BF16),
                        pltpu.VMEM((2, heads, t, t), F32)],
        compiler_params=pltpu.CompilerParams(
            dimension_semantics=("arbitrary", "arbitrary", "arbitrary"),
            vmem_limit_bytes=VMEM_LIMIT_BYTES),
        name="attn",
    )(q, qa, k, ka, v)


def _ffn_kernel(x_ref, att_ref, sg_ref, wo_ref, g2_ref, wup_ref, cw_ref, cb_ref, wdn_ref, g3_ref,
                o_ref, hn_ref, hbuf_ref, act_ref, acc_ref, carry_ref, *, final_norm):
    tm = x_ref.shape[1]
    n_chunks = wup_ref.shape[0]
    halo = SUBLANES

    @pl.when(pl.program_id(1) == 0)
    def _():
        carry_ref[...] = jnp.zeros_like(carry_ref)

    mix = jnp.concatenate([att_ref[0], sg_ref[0]], axis=1)
    h1 = x_ref[0] + jnp.dot(mix, wo_ref[...], preferred_element_type=F32)
    ms = jnp.mean(h1 * h1, axis=-1, keepdims=True)
    hn_ref[...] = ((h1 * lax.rsqrt(ms + EPS)) * g2_ref[...]).astype(BF16)

    acc_ref[...] = h1

    def up(c):
        hbuf = hbuf_ref.at[c % 2]
        hbuf[0:halo, :] = carry_ref[c]
        hbuf[halo:, :] = jnp.dot(hn_ref[...], wup_ref[c], preferred_element_type=F32)
        carry_ref[c] = hbuf[tm:tm + halo, :]

    def gate(c):
        hbuf = hbuf_ref.at[c % 2]
        cw = cw_ref[c]
        y = cb_ref[c]
        for tap in range(CONV_WIDTH):
            shift = CONV_WIDTH - 1 - tap
            y = y + hbuf[halo - shift:halo - shift + tm, :] * cw[tap:tap + 1, :]
        a = y[:, :FF_CHUNK]
        g = y[:, FF_CHUNK:]
        act_ref[:, c * FF_CHUNK:(c + 1) * FF_CHUNK] = (jax.nn.silu(g) * a).astype(BF16)

    up(0)
    for c in range(n_chunks):
        if c + 1 < n_chunks:
            up(c + 1)
        gate(c)

    h2 = acc_ref[...] + jnp.dot(act_ref[...], wdn_ref[...], preferred_element_type=F32)
    if final_norm:
        ms2 = jnp.mean(h2 * h2, axis=-1, keepdims=True)
        h2 = (h2 * lax.rsqrt(ms2 + EPS)) * g3_ref[...]
    o_ref[0] = h2


def _ffn_call(x, att, sg, wo, g2, wup, cw, cb, wdn, g3, final_norm):
    b, s, d = x.shape
    tm = TOKEN_TILE
    n_chunks = wup.shape[0]
    d_ff = wdn.shape[0]
    tile = lambda w: pl.BlockSpec((1, tm, w), lambda i, j: (i, j, 0))
    return pl.pallas_call(
        functools.partial(_ffn_kernel, final_norm=final_norm),
        grid=(b, s // tm),
        in_specs=[tile(d), tile(att.shape[2]), tile(sg.shape[2]), _const_spec(wo.shape),
                  _const_spec(g2.shape), _const_spec(wup.shape), _const_spec(cw.shape),
                  _const_spec(cb.shape), _const_spec(wdn.shape), _const_spec(g3.shape)],
        out_specs=tile(d),
        out_shape=jax.ShapeDtypeStruct((b, s, d), F32),
        scratch_shapes=[pltpu.VMEM((tm, d), BF16),
                        pltpu.VMEM((2, tm + SUBLANES, 2 * FF_CHUNK), F32),
                        pltpu.VMEM((tm, d_ff), BF16),
                        pltpu.VMEM((tm, d), F32),
                        pltpu.VMEM((n_chunks, SUBLANES, 2 * FF_CHUNK), F32)],
        compiler_params=pltpu.CompilerParams(
            dimension_semantics=("arbitrary", "arbitrary"), vmem_limit_bytes=VMEM_LIMIT_BYTES),
        name="ffn",
    )(x, att, sg, wo, g2, wup, cw, cb, wdn, g3)


def kernel(x, norm_mix_g, w_in, b_forget, gmlp_norm_g, w_spatial, b_spatial, w_out, norm_ffn_g,
           w_up, conv_w, conv_b, w_down, norm_final_g):
    depth, d_model, _ = w_in.shape
    n_heads = b_forget.shape[1]
    att_w = n_heads * HEAD_DIM
    gm_w = gmlp_norm_g.shape[1]
    n_groups = w_spatial.shape[1]
    d_ff = w_down.shape[1]
    assert gm_w == n_groups * GROUP_DIM and w_spatial.shape[2] == CHUNK
    assert n_heads * AUG_W <= LANES and att_w % LANES == 0 and gm_w % MXU_DIM == 0
    assert d_ff % FF_CHUNK == 0 and x.shape[1] % TOKEN_TILE == 0 and x.shape[1] % ATT_TILE == 0
    n_chunks = d_ff // FF_CHUNK
    scale = HEAD_DIM ** -0.5

    gi = jnp.arange(MXU_DIM) // GROUP_DIM
    gsum = jnp.where(gi[:, None] == gi[None, :], 1.0 / GROUP_DIM, 0.0).astype(BF16)

    h = x
    for layer in range(depth):
        w = w_in[layer]
        wq = w[:, :att_w] * scale
        wqk = jnp.concatenate([wq, w[:, att_w:2 * att_w]], axis=1).astype(BF16)
        wvt = w[:, 2 * att_w:3 * att_w].T.astype(BF16)
        wuv = w[:, 3 * att_w:3 * att_w + 2 * gm_w].astype(BF16)
        wgate = w[:, 3 * att_w + 2 * gm_w:]
        wg = jnp.zeros((d_model, LANES), F32).at[:, :n_heads * AUG_W].set(
            jnp.repeat(wgate, AUG_W, axis=1)).astype(BF16)
        bfr = jnp.zeros((1, LANES), F32).at[0, :n_heads * AUG_W].set(
            jnp.repeat(b_forget[layer], AUG_W))
        bs = jnp.repeat(b_spatial[layer].T, GROUP_DIM, axis=1)

        q, k, v, qa, ka, sg = _proj_call(
            h, norm_mix_g[layer][None, :], wqk, wvt, wuv, wg, bfr, gmlp_norm_g[layer][None, :],
            w_spatial[layer], bs, gsum)
        att = _attn_call(q, qa, k, ka, v)

        def chunked(m):
            a = m[..., :d_ff].reshape(m.shape[:-1] + (n_chunks, FF_CHUNK))
            g = m[..., d_ff:].reshape(m.shape[:-1] + (n_chunks, FF_CHUNK))
            return jnp.moveaxis(jnp.concatenate([a, g], axis=-1), -2, 0)
        wup = chunked(w_up[layer]).astype(BF16)
        cw = chunked(conv_w[layer])
        cb = chunked(conv_b[layer][None, :])
        h = _ffn_call(h, att, sg, w_out[layer].astype(BF16), norm_ffn_g[layer][None, :], wup, cw, cb,
                      w_down[layer].astype(BF16), norm_final_g[None, :], layer == depth - 1)
    return h
```

```python
import functools

import jax
import jax.numpy as jnp
from jax import lax
from jax.experimental import pallas as pl
from jax.experimental.pallas import tpu as pltpu

EPS = 1e-6
LOG2E = 1.4426950408889634
HEAD_DIM = 64
GROUP_DIM = 64
CHUNK = 128
CONV_WIDTH = 3

LANES = 128
SUBLANES = 8
MXU_DIM = 256
VMEM_LIMIT_BYTES = 56 * 1024 * 1024

AUG_W = 8
TOKEN_TILE = 512
ATT_TILE = 512
FF_CHUNK = 256

F32 = jnp.float32
BF16 = jnp.bfloat16


def _split3(x):
    hi = x.astype(BF16)
    r1 = x - hi.astype(F32)
    mid = r1.astype(BF16)
    lo = (r1 - mid.astype(F32)).astype(BF16)
    return hi, mid, lo


def _const_spec(shape):
    zeros = (0,) * len(shape)
    return pl.BlockSpec(shape, lambda *_: zeros, pipeline_mode=pl.Buffered(1))


def _proj_kernel(x_ref, g_ref, wqk_ref, wvt_ref, wuv_ref, wg_ref, bf_ref, gn_ref, ws_ref, bs_ref,
                 gsum_ref, q_ref, k_ref, vt_ref, qa_ref, ka_ref, sg_ref, carry_ref, *, att_w, gm_w):
    tm = x_ref.shape[1]

    @pl.when(pl.program_id(1) == 0)
    def _():
        carry_ref[...] = jnp.zeros_like(carry_ref)

    x = x_ref[0]
    ms = jnp.mean(x * x, axis=-1, keepdims=True)
    xn = ((x * lax.rsqrt(ms + EPS)) * g_ref[...]).astype(BF16)

    qk = jnp.dot(xn, wqk_ref[...], preferred_element_type=F32)
    q_ref[0] = (qk[:, :att_w] * LOG2E).astype(BF16)
    k_ref[0] = qk[:, att_w:].astype(BF16)
    vt = lax.dot_general(wvt_ref[...], xn, (((1,), (1,)), ((), ())), preferred_element_type=F32)
    for h in range(att_w // HEAD_DIM):
        vt_ref[0, h, :HEAD_DIM, :] = vt[h * HEAD_DIM:(h + 1) * HEAD_DIM].astype(BF16)
        vt_ref[0, h, HEAD_DIM:, :] = jnp.ones((LANES - HEAD_DIM, tm), BF16)

    z = jnp.dot(xn, wg_ref[...], preferred_element_type=F32) + bf_ref[...]
    log_f = -(jnp.maximum(-z, 0.0) + jnp.log1p(jnp.exp(-jnp.abs(z))))

    row = lax.broadcasted_iota(jnp.int32, (CHUNK, CHUNK), 0)
    col = lax.broadcasted_iota(jnp.int32, (CHUNK, CHUNK), 1)
    causal = row >= col
    tri = jnp.where(causal, 1.0, 0.0).astype(BF16)
    offset = carry_ref[0:1, :]
    c_blocks = []
    for r in range(tm // CHUNK):
        blk = log_f[r * CHUNK:(r + 1) * CHUNK]
        s = offset
        for piece in _split3(blk):
            s = s + jnp.dot(tri, piece, preferred_element_type=F32)
        c_blocks.append(s)
        offset = s[CHUNK - 1:CHUNK, :]
    carry_ref[0:1, :] = offset
    c = jnp.concatenate(c_blocks, axis=0)

    c_hi, c_mid, c_lo = [piece.astype(F32) for piece in _split3(c * LOG2E)]
    slot = lax.broadcasted_iota(jnp.int32, (tm, LANES), 1) % AUG_W
    qa = jnp.where(slot == 0, c_hi, jnp.where(slot == 1, c_mid, jnp.where(slot == 2, c_lo,
         jnp.where(slot < 6, 1.0, 0.0))))
    ka = jnp.where(slot < 3, 1.0, jnp.where(slot == 3, -c_hi, jnp.where(slot == 4, -c_mid,
         jnp.where(slot == 5, -c_lo, 0.0))))
    qa_ref[0] = qa.astype(BF16)
    ka_ref[0] = ka.astype(BF16)

    uv = jax.nn.gelu(jnp.dot(xn, wuv_ref[...], preferred_element_type=F32))
    u = uv[:, :gm_w]
    vg = uv[:, gm_w:]
    sq = (vg * vg).astype(BF16)
    gsum = gsum_ref[...]
    msg = jnp.concatenate(
        [jnp.dot(sq[:, s * MXU_DIM:(s + 1) * MXU_DIM], gsum, preferred_element_type=F32)
         for s in range(gm_w // MXU_DIM)], axis=1)
    vgn = ((vg * lax.rsqrt(msg + EPS)) * gn_ref[...]).astype(BF16)

    groups_per_slab = MXU_DIM // GROUP_DIM
    lane_grp = lax.broadcasted_iota(jnp.int32, (CHUNK, MXU_DIM), 1) // GROUP_DIM
    w_masked = [jnp.where(causal, ws_ref[g], 0.0).astype(BF16) for g in range(ws_ref.shape[0])]
    mixed_rows = []
    for r in range(tm // CHUNK):
        slabs = []
        for s in range(gm_w // MXU_DIM):
            vs = vgn[r * CHUNK:(r + 1) * CHUNK, s * MXU_DIM:(s + 1) * MXU_DIM]
            m = None
            for j in range(groups_per_slab):
                mj = jnp.dot(w_masked[s * groups_per_slab + j], vs, preferred_element_type=F32)
                m = mj if m is None else jnp.where(lane_grp == j, mj, m)
            slabs.append(m)
        mixed_rows.append(jnp.concatenate(slabs, axis=1) + bs_ref[...])
    mixed = jnp.concatenate(mixed_rows, axis=0)
    sg_ref[0] = (u * mixed).astype(BF16)


def _proj_call(x, g, wqk, wvt, wuv, wg, bfr, gn, ws, bs, gsum):
    b, s, d = x.shape
    tm = TOKEN_TILE
    att_w = wvt.shape[0]
    n_heads = att_w // HEAD_DIM
    gm_w = wuv.shape[1] // 2
    tile = lambda w: pl.BlockSpec((1, tm, w), lambda i, j: (i, j, 0))
    vt_spec = pl.BlockSpec((1, n_heads, LANES, tm), lambda i, j: (i, 0, 0, j))
    out_shape = ([jax.ShapeDtypeStruct((b, s, att_w), BF16)] * 2
                 + [jax.ShapeDtypeStruct((b, n_heads, LANES, s), BF16)]
                 + [jax.ShapeDtypeStruct((b, s, LANES), BF16)] * 2
                 + [jax.ShapeDtypeStruct((b, s, gm_w), BF16)])
    return pl.pallas_call(
        functools.partial(_proj_kernel, att_w=att_w, gm_w=gm_w),
        grid=(b, s // tm),
        in_specs=[tile(d), _const_spec(g.shape), _const_spec(wqk.shape), _const_spec(wvt.shape),
                  _const_spec(wuv.shape), _const_spec(wg.shape), _const_spec(bfr.shape),
                  _const_spec(gn.shape), _const_spec(ws.shape), _const_spec(bs.shape),
                  _const_spec(gsum.shape)],
        out_specs=[tile(att_w)] * 2 + [vt_spec] + [tile(LANES)] * 2 + [tile(gm_w)],
        out_shape=out_shape,
        scratch_shapes=[pltpu.VMEM((SUBLANES, LANES), F32)],
        compiler_params=pltpu.CompilerParams(
            dimension_semantics=("arbitrary", "arbitrary"), vmem_limit_bytes=VMEM_LIMIT_BYTES),
        name="proj",
    )(x, g, wqk, wvt, wuv, wg, bfr, gn, ws, bs, gsum)


def _attn_kernel(q_ref, qa_ref, k_ref, ka_ref, vt_ref, o_ref, m_ref, acc_ref, qf_ref, s_ref):
    t = q_ref.shape[1]
    pair = pl.program_id(1)
    qi = pl.program_id(2)
    heads = LANES // HEAD_DIM

    lane = lax.broadcasted_iota(jnp.int32, (t, LANES), 1)
    q = q_ref[0].astype(F32)
    qa = qa_ref[0].astype(F32)
    for i in range(heads):
        qm = jnp.where(lane // HEAD_DIM == i, q, 0.0)
        qam = jnp.where(lane // AUG_W == pair * heads + i, qa, 0.0)
        qf_ref[i] = jnp.concatenate([qm.T, qam.T], axis=0).astype(BF16)

    m_ref[...] = jnp.full_like(m_ref, -jnp.inf)
    acc_ref[...] = jnp.zeros_like(acc_ref)
    key = lax.broadcasted_iota(jnp.int32, (t, t), 0)
    qry = lax.broadcasted_iota(jnp.int32, (t, t), 1)

    def scores(tile, par):
        start = pl.multiple_of(tile * t, t)
        kf = jnp.concatenate([k_ref[0, pl.ds(start, t), :], ka_ref[0, pl.ds(start, t), :]], axis=1)
        for i in range(heads):
            s_ref[par, i] = jnp.dot(kf, qf_ref[i], preferred_element_type=F32)

    def softmax_pv(tile, par, diagonal=False):
        start = pl.multiple_of(tile * t, t)
        for i in range(heads):
            vt = vt_ref[0, i, :, pl.ds(start, t)]
            s = s_ref[par, i]
            if diagonal:
                s = jnp.where(qry >= key, s, -jnp.inf)
            m_prev = m_ref[i]
            m_next = jnp.maximum(m_prev, jnp.max(s, axis=0, keepdims=True))
            alpha = jnp.exp2(m_prev - m_next)
            p = jnp.exp2(s - m_next).astype(BF16)
            pv = jnp.dot(vt, p, preferred_element_type=F32)
            acc_ref[i] = acc_ref[i] * alpha + pv
            m_ref[i] = m_next

    scores(0, 0)
    n_pairs = qi // 2

    def body(i, carry):
        scores(2 * i + 1, 1)
        softmax_pv(2 * i, 0)
        scores(2 * i + 2, 0)
        softmax_pv(2 * i + 1, 1)
        return carry

    lax.fori_loop(0, n_pairs, body, 0)

    @pl.when(qi % 2 == 1)
    def _():
        scores(qi, 1)
        softmax_pv(qi - 1, 0)
        softmax_pv(qi, 1, diagonal=True)

    @pl.when(qi % 2 == 0)
    def _():
        softmax_pv(qi, 0, diagonal=True)


    assert heads == 2
    a0, a1 = [acc_ref[i].T for i in range(heads)]
    o0 = a0 / pltpu.roll(a0, HEAD_DIM, axis=1)
    o1 = a1 / pltpu.roll(a1, HEAD_DIM, axis=1)
    out = jnp.where(lane < HEAD_DIM, o0, pltpu.roll(o1, HEAD_DIM, axis=1))
    o_ref[0] = out.astype(o_ref.dtype)


def _attn_call(q, qa, k, ka, v):
    b, s, w = q.shape
    t = ATT_TILE
    heads = LANES // HEAD_DIM
    qspec = pl.BlockSpec((1, t, LANES), lambda i, p, j: (i, j, p))
    aspec = pl.BlockSpec((1, t, LANES), lambda i, p, j: (i, j, 0))
    kspec = pl.BlockSpec((1, s, LANES), lambda i, p, j: (i, 0, p))
    kaspec = pl.BlockSpec((1, s, LANES), lambda i, p, j: (i, 0, 0))
    vspec = pl.BlockSpec((1, heads, LANES, s), lambda i, p, j: (i, p, 0, 0))
    return pl.pallas_call(
        _attn_kernel,
        grid=(b, w // LANES, s // t),
        in_specs=[qspec, aspec, kspec, kaspec, vspec],
        out_specs=qspec,
        out_shape=jax.ShapeDtypeStruct((b, s, w), BF16),
        scratch_shapes=[pltpu.VMEM((heads, 1, t), F32),
                        pltpu.VMEM((heads, LANES, t), F32),
                        pltpu.VMEM((heads, 2 * LANES, t), BF16),
                        pltpu.VMEM((2, heads, t, t), F32)],
        compiler_params=pltpu.CompilerParams(
            dimension_semantics=("arbitrary", "arbitrary", "arbitrary"),
            vmem_limit_bytes=VMEM_LIMIT_BYTES),
        name="attn",
    )(q, qa, k, ka, v)


def _ffn_kernel(x_ref, att_ref, sg_ref, wo_ref, g2_ref, wup_ref, cw_ref, cb_ref, wdn_ref, g3_ref,
                o_ref, hn_ref, hbuf_ref, act_ref, acc_ref, carry_ref, *, final_norm):
    tm = x_ref.shape[1]
    n_chunks = wup_ref.shape[0]
    halo = SUBLANES

    @pl.when(pl.program_id(1) == 0)
    def _():
        carry_ref[...] = jnp.zeros_like(carry_ref)

    mix = jnp.concatenate([att_ref[0], sg_ref[0]], axis=1)
    h1 = x_ref[0] + jnp.dot(mix, wo_ref[...], preferred_element_type=F32)
    ms = jnp.mean(h1 * h1, axis=-1, keepdims=True)
    hn_ref[...] = ((h1 * lax.rsqrt(ms + EPS)) * g2_ref[...]).astype(BF16)

    acc_ref[...] = h1

    def up(c):
        hbuf = hbuf_ref.at[c % 2]
        hbuf[0:halo, :] = carry_ref[c]
        hbuf[halo:, :] = jnp.dot(hn_ref[...], wup_ref[c], preferred_element_type=F32)
        carry_ref[c] = hbuf[tm:tm + halo, :]

    def gate(c):
        hbuf = hbuf_ref.at[c % 2]
        cw = cw_ref[c]
        y = cb_ref[c]
        for tap in range(CONV_WIDTH):
            shift = CONV_WIDTH - 1 - tap
            y = y + hbuf[halo - shift:halo - shift + tm, :] * cw[tap:tap + 1, :]
        a = y[:, :FF_CHUNK]
        g = y[:, FF_CHUNK:]
        act_ref[:, c * FF_CHUNK:(c + 1) * FF_CHUNK] = (jax.nn.silu(g) * a).astype(BF16)

    up(0)
    for c in range(n_chunks):
        if c + 1 < n_chunks:
            up(c + 1)
        gate(c)

    h2 = acc_ref[...] + jnp.dot(act_ref[...], wdn_ref[...], preferred_element_type=F32)
    if final_norm:
        ms2 = jnp.mean(h2 * h2, axis=-1, keepdims=True)
        h2 = (h2 * lax.rsqrt(ms2 + EPS)) * g3_ref[...]
    o_ref[0] = h2


def _ffn_call(x, att, sg, wo, g2, wup, cw, cb, wdn, g3, final_norm):
    b, s, d = x.shape
    tm = TOKEN_TILE
    n_chunks = wup.shape[0]
    d_ff = wdn.shape[0]
    tile = lambda w: pl.BlockSpec((1, tm, w), lambda i, j: (i, j, 0))
    return pl.pallas_call(
        functools.partial(_ffn_kernel, final_norm=final_norm),
        grid=(b, s // tm),
        in_specs=[tile(d), tile(att.shape[2]), tile(sg.shape[2]), _const_spec(wo.shape),
                  _const_spec(g2.shape), _const_spec(wup.shape), _const_spec(cw.shape),
                  _const_spec(cb.shape), _const_spec(wdn.shape), _const_spec(g3.shape)],
        out_specs=tile(d),
        out_shape=jax.ShapeDtypeStruct((b, s, d), F32),
        scratch_shapes=[pltpu.VMEM((tm, d), BF16),
                        pltpu.VMEM((2, tm + SUBLANES, 2 * FF_CHUNK), F32),
                        pltpu.VMEM((tm, d_ff), BF16),
                        pltpu.VMEM((tm, d), F32),
                        pltpu.VMEM((n_chunks, SUBLANES, 2 * FF_CHUNK), F32)],
        compiler_params=pltpu.CompilerParams(
            dimension_semantics=("arbitrary", "arbitrary"), vmem_limit_bytes=VMEM_LIMIT_BYTES),
        name="ffn",
    )(x, att, sg, wo, g2, wup, cw, cb, wdn, g3)


def kernel(x, norm_mix_g, w_in, b_forget, gmlp_norm_g, w_spatial, b_spatial, w_out, norm_ffn_g,
           w_up, conv_w, conv_b, w_down, norm_final_g):
    depth, d_model, _ = w_in.shape
    n_heads = b_forget.shape[1]
    att_w = n_heads * HEAD_DIM
    gm_w = gmlp_norm_g.shape[1]
    n_groups = w_spatial.shape[1]
    d_ff = w_down.shape[1]
    assert gm_w == n_groups * GROUP_DIM and w_spatial.shape[2] == CHUNK
    assert n_heads * AUG_W <= LANES and att_w % LANES == 0 and gm_w % MXU_DIM == 0
    assert d_ff % FF_CHUNK == 0 and x.shape[1] % TOKEN_TILE == 0 and x.shape[1] % ATT_TILE == 0
    n_chunks = d_ff // FF_CHUNK
    scale = HEAD_DIM ** -0.5

    gi = jnp.arange(MXU_DIM) // GROUP_DIM
    gsum = jnp.where(gi[:, None] == gi[None, :], 1.0 / GROUP_DIM, 0.0).astype(BF16)

    h = x
    for layer in range(depth):
        w = w_in[layer]
        wq = w[:, :att_w] * scale
        wqk = jnp.concatenate([wq, w[:, att_w:2 * att_w]], axis=1).astype(BF16)
        wvt = w[:, 2 * att_w:3 * att_w].T.astype(BF16)
        wuv = w[:, 3 * att_w:3 * att_w + 2 * gm_w].astype(BF16)
        wgate = w[:, 3 * att_w + 2 * gm_w:]
        wg = jnp.zeros((d_model, LANES), F32).at[:, :n_heads * AUG_W].set(
            jnp.repeat(wgate, AUG_W, axis=1)).astype(BF16)
        bfr = jnp.zeros((1, LANES), F32).at[0, :n_heads * AUG_W].set(
            jnp.repeat(b_forget[layer], AUG_W))
        bs = jnp.repeat(b_spatial[layer].T, GROUP_DIM, axis=1)

        q, k, v, qa, ka, sg = _proj_call(
            h, norm_mix_g[layer][None, :], wqk, wvt, wuv, wg, bfr, gmlp_norm_g[layer][None, :],
            w_spatial[layer], bs, gsum)
        att = _attn_call(q, qa, k, ka, v)

        def chunked(m):
            a = m[..., :d_ff].reshape(m.shape[:-1] + (n_chunks, FF_CHUNK))
            g = m[..., d_ff:].reshape(m.shape[:-1] + (n_chunks, FF_CHUNK))
            return jnp.moveaxis(jnp.concatenate([a, g], axis=-1), -2, 0)
        wup = chunked(w_up[layer]).astype(BF16)
        cw = chunked(conv_w[layer])
        cb = chunked(conv_b[layer][None, :])
        h = _ffn_call(h, att, sg, w_out[layer].astype(BF16), norm_ffn_g[layer][None, :], wup, cw, cb,
                      w_down[layer].astype(BF16), norm_final_g[None, :], layer == depth - 1)
    return h
```

```python
import functools

import jax
import jax.numpy as jnp
from jax import lax
from jax.experimental import pallas as pl
from jax.experimental.pallas import tpu as pltpu

EPS = 1e-6
LOG2E = 1.4426950408889634
HEAD_DIM = 64
GROUP_DIM = 64
CHUNK = 128
CONV_WIDTH = 3

LANES = 128
SUBLANES = 8
MXU_DIM = 256
VMEM_LIMIT_BYTES = 56 * 1024 * 1024

AUG_W = 8
TOKEN_TILE = 512
ATT_TILE = 512
FF_CHUNK = 256

F32 = jnp.float32
BF16 = jnp.bfloat16


def _split3(x):
    hi = x.astype(BF16)
    r1 = x - hi.astype(F32)
    mid = r1.astype(BF16)
    lo = (r1 - mid.astype(F32)).astype(BF16)
    return hi, mid, lo


def _const_spec(shape):
    zeros = (0,) * len(shape)
    return pl.BlockSpec(shape, lambda *_: zeros, pipeline_mode=pl.Buffered(1))


def _proj_kernel(x_ref, g_ref, wk_ref, wqvt_ref, wuv_ref, wg_ref, bf_ref, gn_ref, ws_ref, bs_ref,
                 gsum_ref, qt_ref, k_ref, vt_ref, ka_ref, sg_ref, carry_ref, *, att_w, gm_w):
    tm = x_ref.shape[1]

    @pl.when(pl.program_id(1) == 0)
    def _():
        carry_ref[...] = jnp.zeros_like(carry_ref)

    x = x_ref[0]
    ms = jnp.mean(x * x, axis=-1, keepdims=True)
    xn = ((x * lax.rsqrt(ms + EPS)) * g_ref[...]).astype(BF16)

    n_heads = att_w // HEAD_DIM
    k_ref[0] = jnp.dot(xn, wk_ref[...], preferred_element_type=F32).astype(BF16)
    qvt = lax.dot_general(wqvt_ref[...], xn, (((1,), (1,)), ((), ())), preferred_element_type=F32)
    qt = qvt[:att_w] * LOG2E
    vt = qvt[att_w:]
    for h in range(n_heads):
        own = (h % (LANES // HEAD_DIM)) * HEAD_DIM
        other = LANES - HEAD_DIM - own
        qt_ref[0, h, own:own + HEAD_DIM, :] = qt[h * HEAD_DIM:(h + 1) * HEAD_DIM].astype(BF16)
        qt_ref[0, h, other:other + HEAD_DIM, :] = jnp.zeros((HEAD_DIM, tm), BF16)
        vt_ref[0, h, :HEAD_DIM, :] = vt[h * HEAD_DIM:(h + 1) * HEAD_DIM].astype(BF16)
        vt_ref[0, h, HEAD_DIM:, :] = jnp.ones((LANES - HEAD_DIM, tm), BF16)

    z = jnp.dot(xn, wg_ref[...], preferred_element_type=F32) + bf_ref[...]
    log_f = -(jnp.maximum(-z, 0.0) + jnp.log1p(jnp.exp(-jnp.abs(z))))

    row = lax.broadcasted_iota(jnp.int32, (CHUNK, CHUNK), 0)
    col = lax.broadcasted_iota(jnp.int32, (CHUNK, CHUNK), 1)
    causal = row >= col
    tri = jnp.where(causal, 1.0, 0.0).astype(BF16)
    offset = carry_ref[0:1, :]
    c_blocks = []
    for r in range(tm // CHUNK):
        blk = log_f[r * CHUNK:(r + 1) * CHUNK]
        s = offset
        for piece in _split3(blk):
            s = s + jnp.dot(tri, piece, preferred_element_type=F32)
        c_blocks.append(s)
        offset = s[CHUNK - 1:CHUNK, :]
    carry_ref[0:1, :] = offset
    c = jnp.concatenate(c_blocks, axis=0)

    c_hi, c_mid, c_lo = [piece.astype(F32) for piece in _split3(c * LOG2E)]
    slot = lax.broadcasted_iota(jnp.int32, (tm, LANES), 1) % AUG_W
    qa = jnp.where(slot == 0, c_hi, jnp.where(slot == 1, c_mid, jnp.where(slot == 2, c_lo,
         jnp.where(slot < 6, 1.0, 0.0))))
    ka = jnp.where(slot < 3, 1.0, jnp.where(slot == 3, -c_hi, jnp.where(slot == 4, -c_mid,
         jnp.where(slot == 5, -c_lo, 0.0))))
    ka_ref[0] = ka.astype(BF16)
    qat = qa.T
    aug_row = lax.broadcasted_iota(jnp.int32, (LANES, tm), 0)
    for h in range(n_heads):
        qt_ref[0, h, LANES:, :] = jnp.where(aug_row // AUG_W == h, qat, 0.0).astype(BF16)

    uv = jax.nn.gelu(jnp.dot(xn, wuv_ref[...], preferred_element_type=F32))
    u = uv[:, :gm_w]
    vg = uv[:, gm_w:]
    sq = (vg * vg).astype(BF16)
    gsum = gsum_ref[...]
    msg = jnp.concatenate(
        [jnp.dot(sq[:, s * MXU_DIM:(s + 1) * MXU_DIM], gsum, preferred_element_type=F32)
         for s in range(gm_w // MXU_DIM)], axis=1)
    vgn = ((vg * lax.rsqrt(msg + EPS)) * gn_ref[...]).astype(BF16)

    groups_per_slab = MXU_DIM // GROUP_DIM
    lane_grp = lax.broadcasted_iota(jnp.int32, (CHUNK, MXU_DIM), 1) // GROUP_DIM
    w_masked = [jnp.where(causal, ws_ref[g], 0.0).astype(BF16) for g in range(ws_ref.shape[0])]
    mixed_rows = []
    for r in range(tm // CHUNK):
        slabs = []
        for s in range(gm_w // MXU_DIM):
            vs = vgn[r * CHUNK:(r + 1) * CHUNK, s * MXU_DIM:(s + 1) * MXU_DIM]
            m = None
            for j in range(groups_per_slab):
                mj = jnp.dot(w_masked[s * groups_per_slab + j], vs, preferred_element_type=F32)
                m = mj if m is None else jnp.where(lane_grp == j, mj, m)
            slabs.append(m)
        mixed_rows.append(jnp.concatenate(slabs, axis=1) + bs_ref[...])
    mixed = jnp.concatenate(mixed_rows, axis=0)
    sg_ref[0] = (u * mixed).astype(BF16)


def _proj_call(x, g, wk, wqvt, wuv, wg, bfr, gn, ws, bs, gsum):
    b, s, d = x.shape
    tm = TOKEN_TILE
    att_w = wk.shape[1]
    n_heads = att_w // HEAD_DIM
    gm_w = wuv.shape[1] // 2
    tile = lambda w: pl.BlockSpec((1, tm, w), lambda i, j: (i, j, 0))
    head_t = lambda rows: pl.BlockSpec((1, n_heads, rows, tm), lambda i, j: (i, 0, 0, j))
    out_shape = [jax.ShapeDtypeStruct((b, n_heads, 2 * LANES, s), BF16),
                 jax.ShapeDtypeStruct((b, s, att_w), BF16),
                 jax.ShapeDtypeStruct((b, n_heads, LANES, s), BF16),
                 jax.ShapeDtypeStruct((b, s, LANES), BF16),
                 jax.ShapeDtypeStruct((b, s, gm_w), BF16)]
    return pl.pallas_call(
        functools.partial(_proj_kernel, att_w=att_w, gm_w=gm_w),
        grid=(b, s // tm),
        in_specs=[tile(d), _const_spec(g.shape), _const_spec(wk.shape), _const_spec(wqvt.shape),
                  _const_spec(wuv.shape), _const_spec(wg.shape), _const_spec(bfr.shape),
                  _const_spec(gn.shape), _const_spec(ws.shape), _const_spec(bs.shape),
                  _const_spec(gsum.shape)],
        out_specs=[head_t(2 * LANES), tile(att_w), head_t(LANES), tile(LANES), tile(gm_w)],
        out_shape=out_shape,
        scratch_shapes=[pltpu.VMEM((SUBLANES, LANES), F32)],
        compiler_params=pltpu.CompilerParams(
            dimension_semantics=("arbitrary", "arbitrary"), vmem_limit_bytes=VMEM_LIMIT_BYTES),
        name="proj",
    )(x, g, wk, wqvt, wuv, wg, bfr, gn, ws, bs, gsum)


def _attn_kernel(qt_ref, k_ref, ka_ref, vt_ref, o_ref, m_ref, acc_ref, s_ref):
    t = qt_ref.shape[3]
    qi = pl.program_id(2)
    heads = qt_ref.shape[1]

    m_ref[...] = jnp.full_like(m_ref, -jnp.inf)
    acc_ref[...] = jnp.zeros_like(acc_ref)
    key = lax.broadcasted_iota(jnp.int32, (t, t), 0)
    qry = lax.broadcasted_iota(jnp.int32, (t, t), 1)

    def scores(tile, par):
        start = pl.multiple_of(tile * t, t)
        kf = jnp.concatenate([k_ref[0, pl.ds(start, t), :], ka_ref[0, pl.ds(start, t), :]], axis=1)
        for i in range(heads):
            s_ref[par, i] = jnp.dot(kf, qt_ref[0, i], preferred_element_type=F32)

    def softmax_pv(tile, par, diagonal=False):
        start = pl.multiple_of(tile * t, t)
        for i in range(heads):
            vt = vt_ref[0, i, :, pl.ds(start, t)]
            s = s_ref[par, i]
            if diagonal:
                s = jnp.where(qry >= key, s, -jnp.inf)
            m_prev = m_ref[i]
            m_next = jnp.maximum(m_prev, jnp.max(s, axis=0, keepdims=True))
            alpha = jnp.exp2(m_prev - m_next)
            p = jnp.exp2(s - m_next).astype(BF16)
            pv = jnp.dot(vt, p, preferred_element_type=F32)
            acc_ref[i] = acc_ref[i] * alpha + pv
            m_ref[i] = m_next

    scores(0, 0)
    n_pairs = qi // 2

    def body(i, carry):
        scores(2 * i + 1, 1)
        softmax_pv(2 * i, 0)
        scores(2 * i + 2, 0)
        softmax_pv(2 * i + 1, 1)
        return carry

    lax.fori_loop(0, n_pairs, body, 0)

    @pl.when(qi % 2 == 1)
    def _():
        scores(qi, 1)
        softmax_pv(qi - 1, 0)
        softmax_pv(qi, 1, diagonal=True)

    @pl.when(qi % 2 == 0)
    def _():
        softmax_pv(qi, 0, diagonal=True)


    outs = [acc_ref[i, :HEAD_DIM, :] / acc_ref[i, HEAD_DIM:, :] for i in range(heads)]
    o_ref[0] = jnp.concatenate(outs, axis=0).T.astype(o_ref.dtype)


def _attn_call(qt, k, ka, vt):
    b, s, w = k.shape
    t = ATT_TILE
    heads = LANES // HEAD_DIM
    qspec = pl.BlockSpec((1, heads, 2 * LANES, t), lambda i, p, j: (i, p, 0, j))
    kspec = pl.BlockSpec((1, s, LANES), lambda i, p, j: (i, 0, p))
    kaspec = pl.BlockSpec((1, s, LANES), lambda i, p, j: (i, 0, 0))
    vspec = pl.BlockSpec((1, heads, LANES, s), lambda i, p, j: (i, p, 0, 0))
    return pl.pallas_call(
        _attn_kernel,
        grid=(b, w // LANES, s // t),
        in_specs=[qspec, kspec, kaspec, vspec],
        out_specs=pl.BlockSpec((1, t, LANES), lambda i, p, j: (i, j, p)),
        out_shape=jax.ShapeDtypeStruct((b, s, w), BF16),
        scratch_shapes=[pltpu.VMEM((heads, 1, t), F32),
                        pltpu.VMEM((heads, LANES, t), F32),
                        pltpu.VMEM((2, heads, t, t), F32)],
        compiler_params=pltpu.CompilerParams(
            dimension_semantics=("arbitrary", "arbitrary", "arbitrary"),
            vmem_limit_bytes=VMEM_LIMIT_BYTES),
        name="attn",
    )(qt, k, ka, vt)


def _ffn_kernel(x_ref, att_ref, sg_ref, wo_ref, g2_ref, wup_ref, cw_ref, cb_ref, wdn_ref, g3_ref,
                o_ref, hn_ref, hbuf_ref, act_ref, acc_ref, carry_ref, *, final_norm):
    tm = x_ref.shape[1]
    n_chunks = wup_ref.shape[0]
    halo = SUBLANES

    @pl.when(pl.program_id(1) == 0)
    def _():
        carry_ref[...] = jnp.zeros_like(carry_ref)

    mix = jnp.concatenate([att_ref[0], sg_ref[0]], axis=1)
    h1 = x_ref[0] + jnp.dot(mix, wo_ref[...], preferred_element_type=F32)
    ms = jnp.mean(h1 * h1, axis=-1, keepdims=True)
    hn_ref[...] = ((h1 * lax.rsqrt(ms + EPS)) * g2_ref[...]).astype(BF16)

    acc_ref[...] = h1

    def up(c):
        hbuf = hbuf_ref.at[c % 2]
        hbuf[0:halo, :] = carry_ref[c]
        hbuf[halo:, :] = jnp.dot(hn_ref[...], wup_ref[c], preferred_element_type=F32)
        carry_ref[c] = hbuf[tm:tm + halo, :]

    def gate(c):
        hbuf = hbuf_ref.at[c % 2]
        cw = cw_ref[c]
        y = cb_ref[c]
        for tap in range(CONV_WIDTH):
            shift = CONV_WIDTH - 1 - tap
            y = y + hbuf[halo - shift:halo - shift + tm, :] * cw[tap:tap + 1, :]
        a = y[:, :FF_CHUNK]
        g = y[:, FF_CHUNK:]
        act_ref[:, c * FF_CHUNK:(c + 1) * FF_CHUNK] = (jax.nn.silu(g) * a).astype(BF16)

    up(0)
    for c in range(n_chunks):
        if c + 1 < n_chunks:
            up(c + 1)
        gate(c)

    h2 = acc_ref[...] + jnp.dot(act_ref[...], wdn_ref[...], preferred_element_type=F32)
    if final_norm:
        ms2 = jnp.mean(h2 * h2, axis=-1, keepdims=True)
        h2 = (h2 * lax.rsqrt(ms2 + EPS)) * g3_ref[...]
    o_ref[0] = h2


def _ffn_call(x, att, sg, wo, g2, wup, cw, cb, wdn, g3, final_norm):
    b, s, d = x.shape
    tm = TOKEN_TILE
    n_chunks = wup.shape[0]
    d_ff = wdn.shape[0]
    tile = lambda w: pl.BlockSpec((1, tm, w), lambda i, j: (i, j, 0))
    return pl.pallas_call(
        functools.partial(_ffn_kernel, final_norm=final_norm),
        grid=(b, s // tm),
        in_specs=[tile(d), tile(att.shape[2]), tile(sg.shape[2]), _const_spec(wo.shape),
                  _const_spec(g2.shape), _const_spec(wup.shape), _const_spec(cw.shape),
                  _const_spec(cb.shape), _const_spec(wdn.shape), _const_spec(g3.shape)],
        out_specs=tile(d),
        out_shape=jax.ShapeDtypeStruct((b, s, d), F32),
        scratch_shapes=[pltpu.VMEM((tm, d), BF16),
                        pltpu.VMEM((2, tm + SUBLANES, 2 * FF_CHUNK), F32),
                        pltpu.VMEM((tm, d_ff), BF16),
                        pltpu.VMEM((tm, d), F32),
                        pltpu.VMEM((n_chunks, SUBLANES, 2 * FF_CHUNK), F32)],
        compiler_params=pltpu.CompilerParams(
            dimension_semantics=("arbitrary", "arbitrary"), vmem_limit_bytes=VMEM_LIMIT_BYTES),
        name="ffn",
    )(x, att, sg, wo, g2, wup, cw, cb, wdn, g3)


def kernel(x, norm_mix_g, w_in, b_forget, gmlp_norm_g, w_spatial, b_spatial, w_out, norm_ffn_g,
           w_up, conv_w, conv_b, w_down, norm_final_g):
    depth, d_model, _ = w_in.shape
    n_heads = b_forget.shape[1]
    att_w = n_heads * HEAD_DIM
    gm_w = gmlp_norm_g.shape[1]
    n_groups = w_spatial.shape[1]
    d_ff = w_down.shape[1]
    assert gm_w == n_groups * GROUP_DIM and w_spatial.shape[2] == CHUNK
    assert n_heads * AUG_W <= LANES and att_w % LANES == 0 and gm_w % MXU_DIM == 0
    assert d_ff % FF_CHUNK == 0 and x.shape[1] % TOKEN_TILE == 0 and x.shape[1] % ATT_TILE == 0
    n_chunks = d_ff // FF_CHUNK
    scale = HEAD_DIM ** -0.5

    gi = jnp.arange(MXU_DIM) // GROUP_DIM
    gsum = jnp.where(gi[:, None] == gi[None, :], 1.0 / GROUP_DIM, 0.0).astype(BF16)

    h = x
    for layer in range(depth):
        w = w_in[layer]
        wq = w[:, :att_w] * scale
        wk = w[:, att_w:2 * att_w].astype(BF16)
        wqvt = jnp.concatenate([wq, w[:, 2 * att_w:3 * att_w]], axis=1).T.astype(BF16)
        wuv = w[:, 3 * att_w:3 * att_w + 2 * gm_w].astype(BF16)
        wgate = w[:, 3 * att_w + 2 * gm_w:]
        wg = jnp.zeros((d_model, LANES), F32).at[:, :n_heads * AUG_W].set(
            jnp.repeat(wgate, AUG_W, axis=1)).astype(BF16)
        bfr = jnp.zeros((1, LANES), F32).at[0, :n_heads * AUG_W].set(
            jnp.repeat(b_forget[layer], AUG_W))
        bs = jnp.repeat(b_spatial[layer].T, GROUP_DIM, axis=1)

        qt, k, vt, ka, sg = _proj_call(
            h, norm_mix_g[layer][None, :], wk, wqvt, wuv, wg, bfr, gmlp_norm_g[layer][None, :],
            w_spatial[layer], bs, gsum)
        att = _attn_call(qt, k, ka, vt)

        def chunked(m):
            a = m[..., :d_ff].reshape(m.shape[:-1] + (n_chunks, FF_CHUNK))
            g = m[..., d_ff:].reshape(m.shape[:-1] + (n_chunks, FF_CHUNK))
            return jnp.moveaxis(jnp.concatenate([a, g], axis=-1), -2, 0)
        wup = chunked(w_up[layer]).astype(BF16)
        cw = chunked(conv_w[layer])
        cb = chunked(conv_b[layer][None, :])
        h = _ffn_call(h, att, sg, w_out[layer].astype(BF16), norm_ffn_g[layer][None, :], wup, cw, cb,
                      w_down[layer].astype(BF16), norm_final_g[None, :], layer == depth - 1)
    return h
```

```python
import functools

import jax
import jax.numpy as jnp
from jax import lax
from jax.experimental import pallas as pl
from jax.experimental.pallas import tpu as pltpu

EPS = 1e-6
LOG2E = 1.4426950408889634
HEAD_DIM = 64
GROUP_DIM = 64
CHUNK = 128
CONV_WIDTH = 3

LANES = 128
SUBLANES = 8
MXU_DIM = 256
VMEM_LIMIT_BYTES = 56 * 1024 * 1024

AUG_W = 8
TOKEN_TILE = 512
ATT_TILE = 512
FF_CHUNK = 256
DOWN_GROUP = 4

F32 = jnp.float32
BF16 = jnp.bfloat16


def _split3(x):
    hi = x.astype(BF16)
    r1 = x - hi.astype(F32)
    mid = r1.astype(BF16)
    lo = (r1 - mid.astype(F32)).astype(BF16)
    return hi, mid, lo


def _const_spec(shape):
    zeros = (0,) * len(shape)
    return pl.BlockSpec(shape, lambda *_: zeros, pipeline_mode=pl.Buffered(1))


def _proj_kernel(x_ref, g_ref, wk_ref, wqvt_ref, wuv_ref, wg_ref, bf_ref, gn_ref, ws_ref, bs_ref,
                 gsum_ref, qt_ref, k_ref, vt_ref, sg_ref, carry_ref, *, att_w, gm_w):
    tm = x_ref.shape[1]

    @pl.when(pl.program_id(1) == 0)
    def _():
        carry_ref[...] = jnp.zeros_like(carry_ref)

    x = x_ref[0]
    ms = jnp.mean(x * x, axis=-1, keepdims=True)
    xn = ((x * lax.rsqrt(ms + EPS)) * g_ref[...]).astype(BF16)

    n_heads = att_w // HEAD_DIM
    k = jnp.dot(xn, wk_ref[...], preferred_element_type=F32)
    qvt = lax.dot_general(wqvt_ref[...], xn, (((1,), (1,)), ((), ())), preferred_element_type=F32)
    qt = qvt[:att_w] * LOG2E
    vt = qvt[att_w:]
    for h in range(n_heads):
        qt_ref[0, h, :HEAD_DIM, :] = qt[h * HEAD_DIM:(h + 1) * HEAD_DIM].astype(BF16)
        vt_ref[0, h, :HEAD_DIM, :] = vt[h * HEAD_DIM:(h + 1) * HEAD_DIM].astype(BF16)
        vt_ref[0, h, HEAD_DIM:, :] = jnp.ones((LANES - HEAD_DIM, tm), BF16)

    z = jnp.dot(xn, wg_ref[...], preferred_element_type=F32) + bf_ref[...]
    log_f = -(jnp.maximum(-z, 0.0) + jnp.log1p(jnp.exp(-jnp.abs(z))))

    row = lax.broadcasted_iota(jnp.int32, (CHUNK, CHUNK), 0)
    col = lax.broadcasted_iota(jnp.int32, (CHUNK, CHUNK), 1)
    causal = row >= col
    tri = jnp.where(causal, 1.0, 0.0).astype(BF16)
    offset = carry_ref[0:1, :]
    c_blocks = []
    for r in range(tm // CHUNK):
        blk = log_f[r * CHUNK:(r + 1) * CHUNK]
        s = offset
        for piece in _split3(blk):
            s = s + jnp.dot(tri, piece, preferred_element_type=F32)
        c_blocks.append(s)
        offset = s[CHUNK - 1:CHUNK, :]
    carry_ref[0:1, :] = offset
    c = jnp.concatenate(c_blocks, axis=0)

    c_hi, c_mid, c_lo = [piece.astype(F32) for piece in _split3(c * LOG2E)]
    slot = lax.broadcasted_iota(jnp.int32, (tm, LANES), 1) % AUG_W
    qa = jnp.where(slot == 0, c_hi, jnp.where(slot == 1, c_mid, jnp.where(slot == 2, c_lo,
         jnp.where(slot < 6, 1.0, 0.0))))
    ka = jnp.where(slot < 3, 1.0, jnp.where(slot == 3, -c_hi, jnp.where(slot == 4, -c_mid,
         jnp.where(slot == 5, -c_lo, 0.0))))
    qat = qa.T
    lane = lax.broadcasted_iota(jnp.int32, (tm, LANES), 1)
    pad_rows = 2 * SUBLANES - AUG_W
    for h in range(n_heads):
        decay_rows = jnp.concatenate(
            [qat[h * AUG_W:(h + 1) * AUG_W], jnp.zeros((pad_rows, tm), F32)], axis=0)
        qt_ref[0, h, HEAD_DIM:HEAD_DIM + 2 * SUBLANES, :] = decay_rows.astype(BF16)
        qt_ref[0, h, HEAD_DIM + 2 * SUBLANES:, :] = jnp.zeros(
            (LANES - HEAD_DIM - 2 * SUBLANES, tm), BF16)
        kh = k[:, (h // 2) * LANES:(h // 2 + 1) * LANES]
        if h % 2:
            kh = pltpu.roll(kh, HEAD_DIM, axis=1)
        kah = pltpu.roll(ka, HEAD_DIM - h * AUG_W, axis=1)
        k_ref[0, :, h * LANES:(h + 1) * LANES] = jnp.where(
            lane < HEAD_DIM, kh, jnp.where(lane < HEAD_DIM + AUG_W, kah, 0.0)).astype(BF16)

    uv = jax.nn.gelu(jnp.dot(xn, wuv_ref[...], preferred_element_type=F32))
    u = uv[:, :gm_w]
    vg = uv[:, gm_w:]
    sq = (vg * vg).astype(BF16)
    gsum = gsum_ref[...]
    msg = jnp.concatenate(
        [jnp.dot(sq[:, s * MXU_DIM:(s + 1) * MXU_DIM], gsum, preferred_element_type=F32)
         for s in range(gm_w // MXU_DIM)], axis=1)
    vgn = ((vg * lax.rsqrt(msg + EPS)) * gn_ref[...]).astype(BF16)

    groups_per_slab = MXU_DIM // GROUP_DIM
    lane_grp = lax.broadcasted_iota(jnp.int32, (CHUNK, MXU_DIM), 1) // GROUP_DIM
    w_masked = [jnp.where(causal, ws_ref[g], 0.0).astype(BF16) for g in range(ws_ref.shape[0])]
    mixed_rows = []
    for r in range(tm // CHUNK):
        slabs = []
        for s in range(gm_w // MXU_DIM):
            vs = vgn[r * CHUNK:(r + 1) * CHUNK, s * MXU_DIM:(s + 1) * MXU_DIM]
            m = None
            for j in range(groups_per_slab):
                mj = jnp.dot(w_masked[s * groups_per_slab + j], vs, preferred_element_type=F32)
                m = mj if m is None else jnp.where(lane_grp == j, mj, m)
            slabs.append(m)
        mixed_rows.append(jnp.concatenate(slabs, axis=1) + bs_ref[...])
    mixed = jnp.concatenate(mixed_rows, axis=0)
    sg_ref[0] = (u * mixed).astype(BF16)


def _proj_call(x, g, wk, wqvt, wuv, wg, bfr, gn, ws, bs, gsum):
    b, s, d = x.shape
    tm = TOKEN_TILE
    att_w = wk.shape[1]
    n_heads = att_w // HEAD_DIM
    gm_w = wuv.shape[1] // 2
    tile = lambda w: pl.BlockSpec((1, tm, w), lambda i, j: (i, j, 0))
    head_t = lambda rows: pl.BlockSpec((1, n_heads, rows, tm), lambda i, j: (i, 0, 0, j))
    out_shape = [jax.ShapeDtypeStruct((b, n_heads, LANES, s), BF16),
                 jax.ShapeDtypeStruct((b, s, n_heads * LANES), BF16),
                 jax.ShapeDtypeStruct((b, n_heads, LANES, s), BF16),
                 jax.ShapeDtypeStruct((b, s, gm_w), BF16)]
    return pl.pallas_call(
        functools.partial(_proj_kernel, att_w=att_w, gm_w=gm_w),
        grid=(b, s // tm),
        in_specs=[tile(d), _const_spec(g.shape), _const_spec(wk.shape), _const_spec(wqvt.shape),
                  _const_spec(wuv.shape), _const_spec(wg.shape), _const_spec(bfr.shape),
                  _const_spec(gn.shape), _const_spec(ws.shape), _const_spec(bs.shape),
                  _const_spec(gsum.shape)],
        out_specs=[head_t(LANES), tile(n_heads * LANES), head_t(LANES), tile(gm_w)],
        out_shape=out_shape,
        scratch_shapes=[pltpu.VMEM((SUBLANES, LANES), F32)],
        compiler_params=pltpu.CompilerParams(
            dimension_semantics=("arbitrary", "arbitrary"), vmem_limit_bytes=VMEM_LIMIT_BYTES),
        name="proj",
    )(x, g, wk, wqvt, wuv, wg, bfr, gn, ws, bs, gsum)


def _attn_kernel(qt_ref, k_ref, vt_ref, o_ref, m_ref, acc_ref, s_ref):
    t = qt_ref.shape[3]
    qi = pl.program_id(2)
    heads = qt_ref.shape[1]

    m_ref[...] = jnp.full_like(m_ref, -jnp.inf)
    acc_ref[...] = jnp.zeros_like(acc_ref)
    key = lax.broadcasted_iota(jnp.int32, (t, t), 0)
    qry = lax.broadcasted_iota(jnp.int32, (t, t), 1)

    def scores(tile, par):
        start = pl.multiple_of(tile * t, t)
        for i in range(heads):
            kf = k_ref[0, pl.ds(start, t), i * LANES:(i + 1) * LANES]
            s_ref[par, i] = jnp.dot(kf, qt_ref[0, i], preferred_element_type=F32)

    def softmax_pv(tile, par, diagonal=False):
        start = pl.multiple_of(tile * t, t)
        for i in range(heads):
            vt = vt_ref[0, i, :, pl.ds(start, t)]
            s = s_ref[par, i]
            if diagonal:
                s = jnp.where(qry >= key, s, -jnp.inf)
            m_prev = m_ref[i]
            m_next = jnp.maximum(m_prev, jnp.max(s, axis=0, keepdims=True))
            alpha = jnp.exp2(m_prev - m_next)
            p = jnp.exp2(s - m_next).astype(BF16)
            pv = jnp.dot(vt, p, preferred_element_type=F32)
            acc_ref[i] = acc_ref[i] * alpha + pv
            m_ref[i] = m_next

    scores(0, 0)
    n_pairs = qi // 2

    def body(i, carry):
        scores(2 * i + 1, 1)
        softmax_pv(2 * i, 0)
        scores(2 * i + 2, 0)
        softmax_pv(2 * i + 1, 1)
        return carry

    lax.fori_loop(0, n_pairs, body, 0)

    @pl.when(qi % 2 == 1)
    def _():
        scores(qi, 1)
        softmax_pv(qi - 1, 0)
        softmax_pv(qi, 1, diagonal=True)

    @pl.when(qi % 2 == 0)
    def _():
        softmax_pv(qi, 0, diagonal=True)


    outs = [acc_ref[i, :HEAD_DIM, :] / acc_ref[i, HEAD_DIM:, :] for i in range(heads)]
    o_ref[0] = jnp.concatenate(outs, axis=0).T.astype(o_ref.dtype)


def _attn_call(qt, k, vt):
    b, n_heads, _, s = qt.shape
    t = ATT_TILE
    heads = LANES // HEAD_DIM
    qspec = pl.BlockSpec((1, heads, LANES, t), lambda i, p, j: (i, p, 0, j))
    kspec = pl.BlockSpec((1, s, heads * LANES), lambda i, p, j: (i, 0, p))
    vspec = pl.BlockSpec((1, heads, LANES, s), lambda i, p, j: (i, p, 0, 0))
    return pl.pallas_call(
        _attn_kernel,
        grid=(b, n_heads // heads, s // t),
        in_specs=[qspec, kspec, vspec],
        out_specs=pl.BlockSpec((1, t, LANES), lambda i, p, j: (i, j, p)),
        out_shape=jax.ShapeDtypeStruct((b, s, n_heads * HEAD_DIM), BF16),
        scratch_shapes=[pltpu.VMEM((heads, 1, t), F32),
                        pltpu.VMEM((heads, LANES, t), F32),
                        pltpu.VMEM((2, heads, t, t), F32)],
        compiler_params=pltpu.CompilerParams(
            dimension_semantics=("arbitrary", "arbitrary", "arbitrary"),
            vmem_limit_bytes=VMEM_LIMIT_BYTES),
        name="attn",
    )(qt, k, vt)


def _ffn_kernel(x_ref, att_ref, sg_ref, wo_ref, g2_ref, wup_ref, cw_ref, cb_ref, wdn_ref, g3_ref,
                o_ref, hn_ref, hbuf_ref, act_ref, acc_ref, carry_ref, *, final_norm):
    tm = x_ref.shape[1]
    n_chunks = wup_ref.shape[0]
    halo = SUBLANES

    @pl.when(pl.program_id(1) == 0)
    def _():
        carry_ref[...] = jnp.zeros_like(carry_ref)

    mix = jnp.concatenate([att_ref[0], sg_ref[0]], axis=1)
    h1 = x_ref[0] + jnp.dot(mix, wo_ref[...], preferred_element_type=F32)
    ms = jnp.mean(h1 * h1, axis=-1, keepdims=True)
    hn_ref[...] = ((h1 * lax.rsqrt(ms + EPS)) * g2_ref[...]).astype(BF16)

    acc_ref[...] = h1

    def up(c):
        hbuf = hbuf_ref.at[c % 2]
        hbuf[0:halo, :] = carry_ref[c]
        hbuf[halo:, :] = jnp.dot(hn_ref[...], wup_ref[c], preferred_element_type=F32)
        carry_ref[c] = hbuf[tm:tm + halo, :]

    def gate(c):
        hbuf = hbuf_ref.at[c % 2]
        cw = cw_ref[c]
        y = cb_ref[c]
        for tap in range(CONV_WIDTH):
            shift = CONV_WIDTH - 1 - tap
            y = y + hbuf[halo - shift:halo - shift + tm, :] * cw[tap:tap + 1, :]
        a = y[:, :FF_CHUNK]
        g = y[:, FF_CHUNK:]
        act_ref[:, c * FF_CHUNK:(c + 1) * FF_CHUNK] = (jax.nn.silu(g) * a).astype(BF16)

    def down(c0, c1):
        lo, hi = c0 * FF_CHUNK, c1 * FF_CHUNK
        return jnp.dot(act_ref[:, lo:hi], wdn_ref[lo:hi, :], preferred_element_type=F32)

    group_ends = [c for c in range(DOWN_GROUP, n_chunks, DOWN_GROUP)]
    done = 0
    up(0)
    for c in range(n_chunks):
        if c + 1 < n_chunks:
            up(c + 1)
        if c in group_ends:
            acc_ref[...] += down(done, c)
            done = c
        gate(c)

    h2 = acc_ref[...] + down(done, n_chunks)
    if final_norm:
        ms2 = jnp.mean(h2 * h2, axis=-1, keepdims=True)
        h2 = (h2 * lax.rsqrt(ms2 + EPS)) * g3_ref[...]
    o_ref[0] = h2


def _ffn_call(x, att, sg, wo, g2, wup, cw, cb, wdn, g3, final_norm):
    b, s, d = x.shape
    tm = TOKEN_TILE
    n_chunks = wup.shape[0]
    d_ff = wdn.shape[0]
    tile = lambda w: pl.BlockSpec((1, tm, w), lambda i, j: (i, j, 0))
    return pl.pallas_call(
        functools.partial(_ffn_kernel, final_norm=final_norm),
        grid=(b, s // tm),
        in_specs=[tile(d), tile(att.shape[2]), tile(sg.shape[2]), _const_spec(wo.shape),
                  _const_spec(g2.shape), _const_spec(wup.shape), _const_spec(cw.shape),
                  _const_spec(cb.shape), _const_spec(wdn.shape), _const_spec(g3.shape)],
        out_specs=tile(d),
        out_shape=jax.ShapeDtypeStruct((b, s, d), F32),
        scratch_shapes=[pltpu.VMEM((tm, d), BF16),
                        pltpu.VMEM((2, tm + SUBLANES, 2 * FF_CHUNK), F32),
                        pltpu.VMEM((tm, d_ff), BF16),
                        pltpu.VMEM((tm, d), F32),
                        pltpu.VMEM((n_chunks, SUBLANES, 2 * FF_CHUNK), F32)],
        compiler_params=pltpu.CompilerParams(
            dimension_semantics=("arbitrary", "arbitrary"), vmem_limit_bytes=VMEM_LIMIT_BYTES),
        name="ffn",
    )(x, att, sg, wo, g2, wup, cw, cb, wdn, g3)


def kernel(x, norm_mix_g, w_in, b_forget, gmlp_norm_g, w_spatial, b_spatial, w_out, norm_ffn_g,
           w_up, conv_w, conv_b, w_down, norm_final_g):
    depth, d_model, _ = w_in.shape
    n_heads = b_forget.shape[1]
    att_w = n_heads * HEAD_DIM
    gm_w = gmlp_norm_g.shape[1]
    n_groups = w_spatial.shape[1]
    d_ff = w_down.shape[1]
    assert gm_w == n_groups * GROUP_DIM and w_spatial.shape[2] == CHUNK
    assert n_heads * AUG_W <= LANES and att_w % LANES == 0 and gm_w % MXU_DIM == 0
    assert d_ff % FF_CHUNK == 0 and x.shape[1] % TOKEN_TILE == 0 and x.shape[1] % ATT_TILE == 0
    n_chunks = d_ff // FF_CHUNK
    scale = HEAD_DIM ** -0.5

    gi = jnp.arange(MXU_DIM) // GROUP_DIM
    gsum = jnp.where(gi[:, None] == gi[None, :], 1.0 / GROUP_DIM, 0.0).astype(BF16)

    h = x
    for layer in range(depth):
        w = w_in[layer]
        wq = w[:, :att_w] * scale
        wk = w[:, att_w:2 * att_w].astype(BF16)
        wqvt = jnp.concatenate([wq, w[:, 2 * att_w:3 * att_w]], axis=1).T.astype(BF16)
        wuv = w[:, 3 * att_w:3 * att_w + 2 * gm_w].astype(BF16)
        wgate = w[:, 3 * att_w + 2 * gm_w:]
        wg = jnp.zeros((d_model, LANES), F32).at[:, :n_heads * AUG_W].set(
            jnp.repeat(wgate, AUG_W, axis=1)).astype(BF16)
        bfr = jnp.zeros((1, LANES), F32).at[0, :n_heads * AUG_W].set(
            jnp.repeat(b_forget[layer], AUG_W))
        bs = jnp.repeat(b_spatial[layer].T, GROUP_DIM, axis=1)

        qt, k, vt, sg = _proj_call(
            h, norm_mix_g[layer][None, :], wk, wqvt, wuv, wg, bfr, gmlp_norm_g[layer][None, :],
            w_spatial[layer], bs, gsum)
        att = _attn_call(qt, k, vt)

        def chunked(m):
            a = m[..., :d_ff].reshape(m.shape[:-1] + (n_chunks, FF_CHUNK))
            g = m[..., d_ff:].reshape(m.shape[:-1] + (n_chunks, FF_CHUNK))
            return jnp.moveaxis(jnp.concatenate([a, g], axis=-1), -2, 0)
        wup = chunked(w_up[layer]).astype(BF16)
        cw = chunked(conv_w[layer])
        cb = chunked(conv_b[layer][None, :])
        h = _ffn_call(h, att, sg, w_out[layer].astype(BF16), norm_ffn_g[layer][None, :], wup, cw, cb,
                      w_down[layer].astype(BF16), norm_final_g[None, :], layer == depth - 1)
    return h
```

```python
import functools

import jax
import jax.numpy as jnp
from jax import lax
from jax.experimental import pallas as pl
from jax.experimental.pallas import tpu as pltpu

EPS = 1e-6
LOG2E = 1.4426950408889634
HEAD_DIM = 64
GROUP_DIM = 64
CHUNK = 128
CONV_WIDTH = 3

LANES = 128
SUBLANES = 8
MXU_DIM = 256
VMEM_LIMIT_BYTES = 56 * 1024 * 1024

AUG_W = 8
PROJ_TILE = 1024
TOKEN_TILE = 512
ATT_TILE = 512
FF_CHUNK = 256
DOWN_GROUP = 4

F32 = jnp.float32
BF16 = jnp.bfloat16


def _split3(x):
    hi = x.astype(BF16)
    r1 = x - hi.astype(F32)
    mid = r1.astype(BF16)
    lo = (r1 - mid.astype(F32)).astype(BF16)
    return hi, mid, lo


def _const_spec(shape):
    zeros = (0,) * len(shape)
    return pl.BlockSpec(shape, lambda *_: zeros, pipeline_mode=pl.Buffered(1))


def _proj_kernel(x_ref, g_ref, wk_ref, wqvt_ref, wuv_ref, wg_ref, bf_ref, gn_ref, ws_ref, bs_ref,
                 gsum_ref, qt_ref, k_ref, vt_ref, sg_ref, carry_ref, *, att_w, gm_w):
    tm = x_ref.shape[1]

    @pl.when(pl.program_id(1) == 0)
    def _():
        carry_ref[...] = jnp.zeros_like(carry_ref)

    x = x_ref[0]
    ms = jnp.mean(x * x, axis=-1, keepdims=True)
    xn = ((x * lax.rsqrt(ms + EPS)) * g_ref[...]).astype(BF16)

    n_heads = att_w // HEAD_DIM
    k = jnp.dot(xn, wk_ref[...], preferred_element_type=F32)
    qvt = lax.dot_general(wqvt_ref[...], xn, (((1,), (1,)), ((), ())), preferred_element_type=F32)
    qt = qvt[:att_w] * LOG2E
    vt = qvt[att_w:]
    for h in range(n_heads):
        qt_ref[0, h, :HEAD_DIM, :] = qt[h * HEAD_DIM:(h + 1) * HEAD_DIM].astype(BF16)
        vt_ref[0, h, :HEAD_DIM, :] = vt[h * HEAD_DIM:(h + 1) * HEAD_DIM].astype(BF16)
        vt_ref[0, h, HEAD_DIM:, :] = jnp.ones((LANES - HEAD_DIM, tm), BF16)

    z = jnp.dot(xn, wg_ref[...], preferred_element_type=F32) + bf_ref[...]
    log_f = -(jnp.maximum(-z, 0.0) + jnp.log1p(jnp.exp(-jnp.abs(z))))

    row = lax.broadcasted_iota(jnp.int32, (CHUNK, CHUNK), 0)
    col = lax.broadcasted_iota(jnp.int32, (CHUNK, CHUNK), 1)
    causal = row >= col
    tri = jnp.where(causal, 1.0, 0.0).astype(BF16)
    offset = carry_ref[0:1, :]
    c_blocks = []
    for r in range(tm // CHUNK):
        blk = log_f[r * CHUNK:(r + 1) * CHUNK]
        s = offset
        for piece in _split3(blk):
            s = s + jnp.dot(tri, piece, preferred_element_type=F32)
        c_blocks.append(s)
        offset = s[CHUNK - 1:CHUNK, :]
    carry_ref[0:1, :] = offset
    c = jnp.concatenate(c_blocks, axis=0)

    c_hi, c_mid, c_lo = [piece.astype(F32) for piece in _split3(c * LOG2E)]
    slot = lax.broadcasted_iota(jnp.int32, (tm, LANES), 1) % AUG_W
    qa = jnp.where(slot == 0, c_hi, jnp.where(slot == 1, c_mid, jnp.where(slot == 2, c_lo,
         jnp.where(slot < 6, 1.0, 0.0))))
    ka = jnp.where(slot < 3, 1.0, jnp.where(slot == 3, -c_hi, jnp.where(slot == 4, -c_mid,
         jnp.where(slot == 5, -c_lo, 0.0))))
    qat = qa.T
    lane = lax.broadcasted_iota(jnp.int32, (tm, LANES), 1)
    pad_rows = 2 * SUBLANES - AUG_W
    for h in range(n_heads):
        decay_rows = jnp.concatenate(
            [qat[h * AUG_W:(h + 1) * AUG_W], jnp.zeros((pad_rows, tm), F32)], axis=0)
        qt_ref[0, h, HEAD_DIM:HEAD_DIM + 2 * SUBLANES, :] = decay_rows.astype(BF16)
        qt_ref[0, h, HEAD_DIM + 2 * SUBLANES:, :] = jnp.zeros(
            (LANES - HEAD_DIM - 2 * SUBLANES, tm), BF16)
        kh = k[:, (h // 2) * LANES:(h // 2 + 1) * LANES]
        if h % 2:
            kh = pltpu.roll(kh, HEAD_DIM, axis=1)
        kah = pltpu.roll(ka, HEAD_DIM - h * AUG_W, axis=1)
        k_ref[0, :, h * LANES:(h + 1) * LANES] = jnp.where(
            lane < HEAD_DIM, kh, jnp.where(lane < HEAD_DIM + AUG_W, kah, 0.0)).astype(BF16)

    uv = jax.nn.gelu(jnp.dot(xn, wuv_ref[...], preferred_element_type=F32))
    u = uv[:, :gm_w]
    vg = uv[:, gm_w:]
    sq = (vg * vg).astype(BF16)
    gsum = gsum_ref[...]
    msg = jnp.concatenate(
        [jnp.dot(sq[:, s * MXU_DIM:(s + 1) * MXU_DIM], gsum, preferred_element_type=F32)
         for s in range(gm_w // MXU_DIM)], axis=1)
    vgn = ((vg * lax.rsqrt(msg + EPS)) * gn_ref[...]).astype(BF16)

    groups_per_slab = MXU_DIM // GROUP_DIM
    lane_grp = lax.broadcasted_iota(jnp.int32, (CHUNK, MXU_DIM), 1) // GROUP_DIM
    w_masked = [jnp.where(causal, ws_ref[g], 0.0).astype(BF16) for g in range(ws_ref.shape[0])]
    mixed_rows = []
    for r in range(tm // CHUNK):
        slabs = []
        for s in range(gm_w // MXU_DIM):
            vs = vgn[r * CHUNK:(r + 1) * CHUNK, s * MXU_DIM:(s + 1) * MXU_DIM]
            m = None
            for j in range(groups_per_slab):
                mj = jnp.dot(w_masked[s * groups_per_slab + j], vs, preferred_element_type=F32)
                m = mj if m is None else jnp.where(lane_grp == j, mj, m)
            slabs.append(m)
        mixed_rows.append(jnp.concatenate(slabs, axis=1) + bs_ref[...])
    mixed = jnp.concatenate(mixed_rows, axis=0)
    sg_ref[0] = (u * mixed).astype(BF16)


def _proj_call(x, g, wk, wqvt, wuv, wg, bfr, gn, ws, bs, gsum):
    b, s, d = x.shape
    tm = PROJ_TILE
    att_w = wk.shape[1]
    n_heads = att_w // HEAD_DIM
    gm_w = wuv.shape[1] // 2
    tile = lambda w: pl.BlockSpec((1, tm, w), lambda i, j: (i, j, 0))
    head_t = lambda rows: pl.BlockSpec((1, n_heads, rows, tm), lambda i, j: (i, 0, 0, j))
    out_shape = [jax.ShapeDtypeStruct((b, n_heads, LANES, s), BF16),
                 jax.ShapeDtypeStruct((b, s, n_heads * LANES), BF16),
                 jax.ShapeDtypeStruct((b, n_heads, LANES, s), BF16),
                 jax.ShapeDtypeStruct((b, s, gm_w), BF16)]
    return pl.pallas_call(
        functools.partial(_proj_kernel, att_w=att_w, gm_w=gm_w),
        grid=(b, s // tm),
        in_specs=[tile(d), _const_spec(g.shape), _const_spec(wk.shape), _const_spec(wqvt.shape),
                  _const_spec(wuv.shape), _const_spec(wg.shape), _const_spec(bfr.shape),
                  _const_spec(gn.shape), _const_spec(ws.shape), _const_spec(bs.shape),
                  _const_spec(gsum.shape)],
        out_specs=[head_t(LANES), tile(n_heads * LANES), head_t(LANES), tile(gm_w)],
        out_shape=out_shape,
        scratch_shapes=[pltpu.VMEM((SUBLANES, LANES), F32)],
        compiler_params=pltpu.CompilerParams(
            dimension_semantics=("arbitrary", "arbitrary"), vmem_limit_bytes=VMEM_LIMIT_BYTES),
        name="proj",
    )(x, g, wk, wqvt, wuv, wg, bfr, gn, ws, bs, gsum)


def _attn_kernel(qt_ref, k_ref, vt_ref, o_ref, m_ref, acc_ref, s_ref):
    t = s_ref.shape[2]
    heads = qt_ref.shape[1]
    key = lax.broadcasted_iota(jnp.int32, (t, t), 0)
    qry = lax.broadcasted_iota(jnp.int32, (t, t), 1)
    for odd in range(2):
        _attn_query_tile(qt_ref, k_ref, vt_ref, o_ref, m_ref, acc_ref, s_ref, key, qry,
                         2 * pl.program_id(2) + odd, odd, t, heads)


def _attn_query_tile(qt_ref, k_ref, vt_ref, o_ref, m_ref, acc_ref, s_ref, key, qry, qi, odd, t,
                     heads):
    cols = slice(odd * t, (odd + 1) * t)
    m_ref[...] = jnp.full_like(m_ref, -jnp.inf)
    acc_ref[...] = jnp.zeros_like(acc_ref)

    def scores(tile, par):
        start = pl.multiple_of(tile * t, t)
        for i in range(heads):
            kf = k_ref[0, pl.ds(start, t), i * LANES:(i + 1) * LANES]
            s_ref[par, i] = jnp.dot(kf, qt_ref[0, i, :, cols],
                                    preferred_element_type=F32)

    def softmax_pv(tile, par, diagonal=False):
        start = pl.multiple_of(tile * t, t)
        for i in range(heads):
            vt = vt_ref[0, i, :, pl.ds(start, t)]
            s = s_ref[par, i]
            if diagonal:
                s = jnp.where(qry >= key, s, -jnp.inf)
            m_prev = m_ref[i]
            m_next = jnp.maximum(m_prev, jnp.max(s, axis=0, keepdims=True))
            alpha = jnp.exp2(m_prev - m_next)
            p = jnp.exp2(s - m_next).astype(BF16)
            pv = jnp.dot(vt, p, preferred_element_type=F32)
            acc_ref[i] = acc_ref[i] * alpha + pv
            m_ref[i] = m_next

    scores(0, 0)
    n_pairs = qi // 2

    def body(i, carry):
        scores(2 * i + 1, 1)
        softmax_pv(2 * i, 0)
        scores(2 * i + 2, 0)
        softmax_pv(2 * i + 1, 1)
        return carry

    lax.fori_loop(0, n_pairs, body, 0)

    if odd:
        scores(qi, 1)
        softmax_pv(qi - 1, 0)
        softmax_pv(qi, 1, diagonal=True)
    else:
        softmax_pv(qi, 0, diagonal=True)

    outs = [acc_ref[i, :HEAD_DIM, :] / acc_ref[i, HEAD_DIM:, :] for i in range(heads)]
    o_ref[0, cols, :] = jnp.concatenate(outs, axis=0).T.astype(o_ref.dtype)


def _attn_call(qt, k, vt):
    b, n_heads, _, s = qt.shape
    t = ATT_TILE
    heads = LANES // HEAD_DIM
    qspec = pl.BlockSpec((1, heads, LANES, 2 * t), lambda i, p, j: (i, p, 0, j))
    kspec = pl.BlockSpec((1, s, heads * LANES), lambda i, p, j: (i, 0, p))
    vspec = pl.BlockSpec((1, heads, LANES, s), lambda i, p, j: (i, p, 0, 0))
    return pl.pallas_call(
        _attn_kernel,
        grid=(b, n_heads // heads, s // (2 * t)),
        in_specs=[qspec, kspec, vspec],
        out_specs=pl.BlockSpec((1, 2 * t, LANES), lambda i, p, j: (i, j, p)),
        out_shape=jax.ShapeDtypeStruct((b, s, n_heads * HEAD_DIM), BF16),
        scratch_shapes=[pltpu.VMEM((heads, 1, t), F32),
                        pltpu.VMEM((heads, LANES, t), F32),
                        pltpu.VMEM((2, heads, t, t), F32)],
        compiler_params=pltpu.CompilerParams(
            dimension_semantics=("arbitrary", "arbitrary", "arbitrary"),
            vmem_limit_bytes=VMEM_LIMIT_BYTES),
        name="attn",
    )(qt, k, vt)


def _ffn_kernel(x_ref, att_ref, sg_ref, wo_ref, g2_ref, wup_ref, cw_ref, cb_ref, wdn_ref, g3_ref,
                o_ref, hn_ref, hbuf_ref, act_ref, acc_ref, carry_ref, *, final_norm):
    tm = x_ref.shape[1]
    n_chunks = wup_ref.shape[0]
    halo = SUBLANES

    @pl.when(pl.program_id(1) == 0)
    def _():
        carry_ref[...] = jnp.zeros_like(carry_ref)

    mix = jnp.concatenate([att_ref[0], sg_ref[0]], axis=1)
    h1 = x_ref[0] + jnp.dot(mix, wo_ref[...], preferred_element_type=F32)
    ms = jnp.mean(h1 * h1, axis=-1, keepdims=True)
    hn_ref[...] = ((h1 * lax.rsqrt(ms + EPS)) * g2_ref[...]).astype(BF16)

    acc_ref[...] = h1

    def up(c):
        hbuf = hbuf_ref.at[c % 2]
        hbuf[0:halo, :] = carry_ref[c]
        hbuf[halo:, :] = jnp.dot(hn_ref[...], wup_ref[c], preferred_element_type=F32)
        carry_ref[c] = hbuf[tm:tm + halo, :]

    def gate(c):
        hbuf = hbuf_ref.at[c % 2]
        cw = cw_ref[c]
        y = cb_ref[c]
        for tap in range(CONV_WIDTH):
            shift = CONV_WIDTH - 1 - tap
            y = y + hbuf[halo - shift:halo - shift + tm, :] * cw[tap:tap + 1, :]
        a = y[:, :FF_CHUNK]
        g = y[:, FF_CHUNK:]
        act_ref[:, c * FF_CHUNK:(c + 1) * FF_CHUNK] = (jax.nn.silu(g) * a).astype(BF16)

    def down(c0, c1):
        lo, hi = c0 * FF_CHUNK, c1 * FF_CHUNK
        return jnp.dot(act_ref[:, lo:hi], wdn_ref[lo:hi, :], preferred_element_type=F32)

    group_ends = [c for c in range(DOWN_GROUP, n_chunks, DOWN_GROUP)]
    done = 0
    up(0)
    for c in range(n_chunks):
        if c + 1 < n_chunks:
            up(c + 1)
        if c in group_ends:
            acc_ref[...] += down(done, c)
            done = c
        gate(c)

    h2 = acc_ref[...] + down(done, n_chunks)
    if final_norm:
        ms2 = jnp.mean(h2 * h2, axis=-1, keepdims=True)
        h2 = (h2 * lax.rsqrt(ms2 + EPS)) * g3_ref[...]
    o_ref[0] = h2


def _ffn_call(x, att, sg, wo, g2, wup, cw, cb, wdn, g3, final_norm):
    b, s, d = x.shape
    tm = TOKEN_TILE
    n_chunks = wup.shape[0]
    d_ff = wdn.shape[0]
    tile = lambda w: pl.BlockSpec((1, tm, w), lambda i, j: (i, j, 0))
    return pl.pallas_call(
        functools.partial(_ffn_kernel, final_norm=final_norm),
        grid=(b, s // tm),
        in_specs=[tile(d), tile(att.shape[2]), tile(sg.shape[2]), _const_spec(wo.shape),
                  _const_spec(g2.shape), _const_spec(wup.shape), _const_spec(cw.shape),
                  _const_spec(cb.shape), _const_spec(wdn.shape), _const_spec(g3.shape)],
        out_specs=tile(d),
        out_shape=jax.ShapeDtypeStruct((b, s, d), F32),
        scratch_shapes=[pltpu.VMEM((tm, d), BF16),
                        pltpu.VMEM((2, tm + SUBLANES, 2 * FF_CHUNK), F32),
                        pltpu.VMEM((tm, d_ff), BF16),
                        pltpu.VMEM((tm, d), F32),
                        pltpu.VMEM((n_chunks, SUBLANES, 2 * FF_CHUNK), F32)],
        compiler_params=pltpu.CompilerParams(
            dimension_semantics=("arbitrary", "arbitrary"), vmem_limit_bytes=VMEM_LIMIT_BYTES),
        name="ffn",
    )(x, att, sg, wo, g2, wup, cw, cb, wdn, g3)


def kernel(x, norm_mix_g, w_in, b_forget, gmlp_norm_g, w_spatial, b_spatial, w_out, norm_ffn_g,
           w_up, conv_w, conv_b, w_down, norm_final_g):
    depth, d_model, _ = w_in.shape
    n_heads = b_forget.shape[1]
    att_w = n_heads * HEAD_DIM
    gm_w = gmlp_norm_g.shape[1]
    n_groups = w_spatial.shape[1]
    d_ff = w_down.shape[1]
    assert gm_w == n_groups * GROUP_DIM and w_spatial.shape[2] == CHUNK
    assert n_heads * AUG_W <= LANES and att_w % LANES == 0 and gm_w % MXU_DIM == 0
    assert d_ff % FF_CHUNK == 0 and x.shape[1] % TOKEN_TILE == 0 and x.shape[1] % PROJ_TILE == 0
    assert x.shape[1] % (2 * ATT_TILE) == 0
    n_chunks = d_ff // FF_CHUNK
    scale = HEAD_DIM ** -0.5

    gi = jnp.arange(MXU_DIM) // GROUP_DIM
    gsum = jnp.where(gi[:, None] == gi[None, :], 1.0 / GROUP_DIM, 0.0).astype(BF16)

    h = x
    for layer in range(depth):
        w = w_in[layer]
        wq = w[:, :att_w] * scale
        wk = w[:, att_w:2 * att_w].astype(BF16)
        wqvt = jnp.concatenate([wq, w[:, 2 * att_w:3 * att_w]], axis=1).T.astype(BF16)
        wuv = w[:, 3 * att_w:3 * att_w + 2 * gm_w].astype(BF16)
        wgate = w[:, 3 * att_w + 2 * gm_w:]
        wg = jnp.zeros((d_model, LANES), F32).at[:, :n_heads * AUG_W].set(
            jnp.repeat(wgate, AUG_W, axis=1)).astype(BF16)
        bfr = jnp.zeros((1, LANES), F32).at[0, :n_heads * AUG_W].set(
            jnp.repeat(b_forget[layer], AUG_W))
        bs = jnp.repeat(b_spatial[layer].T, GROUP_DIM, axis=1)

        qt, k, vt, sg = _proj_call(
            h, norm_mix_g[layer][None, :], wk, wqvt, wuv, wg, bfr, gmlp_norm_g[layer][None, :],
            w_spatial[layer], bs, gsum)
        att = _attn_call(qt, k, vt)

        def chunked(m):
            a = m[..., :d_ff].reshape(m.shape[:-1] + (n_chunks, FF_CHUNK))
            g = m[..., d_ff:].reshape(m.shape[:-1] + (n_chunks, FF_CHUNK))
            return jnp.moveaxis(jnp.concatenate([a, g], axis=-1), -2, 0)
        wup = chunked(w_up[layer]).astype(BF16)
        cw = chunked(conv_w[layer])
        cb = chunked(conv_b[layer][None, :])
        h = _ffn_call(h, att, sg, w_out[layer].astype(BF16), norm_ffn_g[layer][None, :], wup, cw, cb,
                      w_down[layer].astype(BF16), norm_final_g[None, :], layer == depth - 1)
    return h
```

```python
import functools

import jax
import jax.numpy as jnp
from jax import lax
from jax.experimental import pallas as pl
from jax.experimental.pallas import tpu as pltpu

EPS = 1e-6
LOG2E = 1.4426950408889634
HEAD_DIM = 64
GROUP_DIM = 64
CHUNK = 128
CONV_WIDTH = 3

LANES = 128
SUBLANES = 8
MXU_DIM = 256
VMEM_LIMIT_BYTES = 56 * 1024 * 1024

AUG_W = 8
PROJ_TILE = 1024
TOKEN_TILE = 512
ATT_TILE = 512
FF_CHUNK = 256
DOWN_GROUP = 4

F32 = jnp.float32
BF16 = jnp.bfloat16


def _split3(x):
    hi = x.astype(BF16)
    r1 = x - hi.astype(F32)
    mid = r1.astype(BF16)
    lo = (r1 - mid.astype(F32)).astype(BF16)
    return hi, mid, lo


def _const_spec(shape):
    zeros = (0,) * len(shape)
    return pl.BlockSpec(shape, lambda *_: zeros, pipeline_mode=pl.Buffered(1))


def _proj_kernel(x_ref, g_ref, wk_ref, wqvt_ref, wuv_ref, wg_ref, bf_ref, gn_ref, ws_ref, bs_ref,
                 gsum_ref, qt_ref, k_ref, vt_ref, sg_ref, carry_ref, *, att_w, gm_w):
    tm = x_ref.shape[1]

    @pl.when(pl.program_id(1) == 0)
    def _():
        carry_ref[...] = jnp.zeros_like(carry_ref)

    x = x_ref[0]
    ms = jnp.mean(x * x, axis=-1, keepdims=True)
    xn = ((x * lax.rsqrt(ms + EPS)) * g_ref[...]).astype(BF16)

    n_heads = att_w // HEAD_DIM
    k = jnp.dot(xn, wk_ref[...], preferred_element_type=F32)
    qvt = lax.dot_general(wqvt_ref[...], xn, (((1,), (1,)), ((), ())), preferred_element_type=F32)
    qt = qvt[:att_w] * LOG2E
    vt = qvt[att_w:]
    for h in range(n_heads):
        qt_ref[0, h, :HEAD_DIM, :] = qt[h * HEAD_DIM:(h + 1) * HEAD_DIM].astype(BF16)
        vt_ref[0, h, :HEAD_DIM, :] = vt[h * HEAD_DIM:(h + 1) * HEAD_DIM].astype(BF16)
        vt_ref[0, h, HEAD_DIM:, :] = jnp.ones((LANES - HEAD_DIM, tm), BF16)

    z = jnp.dot(xn, wg_ref[...], preferred_element_type=F32) + bf_ref[...]
    log_f = -(jnp.maximum(-z, 0.0) + jnp.log1p(jnp.exp(-jnp.abs(z))))

    row = lax.broadcasted_iota(jnp.int32, (CHUNK, CHUNK), 0)
    col = lax.broadcasted_iota(jnp.int32, (CHUNK, CHUNK), 1)
    causal = row >= col
    tri = jnp.where(causal, 1.0, 0.0).astype(BF16)
    offset = carry_ref[0:1, :]
    c_blocks = []
    for r in range(tm // CHUNK):
        blk = log_f[r * CHUNK:(r + 1) * CHUNK]
        s = offset
        for piece in _split3(blk):
            s = s + jnp.dot(tri, piece, preferred_element_type=F32)
        c_blocks.append(s)
        offset = s[CHUNK - 1:CHUNK, :]
    carry_ref[0:1, :] = offset
    c = jnp.concatenate(c_blocks, axis=0)

    c_hi, c_mid, c_lo = [piece.astype(F32) for piece in _split3(c * LOG2E)]
    slot = lax.broadcasted_iota(jnp.int32, (tm, LANES), 1) % AUG_W
    qa = jnp.where(slot == 0, c_hi, jnp.where(slot == 1, c_mid, jnp.where(slot == 2, c_lo,
         jnp.where(slot < 6, 1.0, 0.0))))
    ka = jnp.where(slot < 3, 1.0, jnp.where(slot == 3, -c_hi, jnp.where(slot == 4, -c_mid,
         jnp.where(slot == 5, -c_lo, 0.0))))
    qat = qa.T
    lane = lax.broadcasted_iota(jnp.int32, (tm, LANES), 1)
    pad_rows = 2 * SUBLANES - AUG_W
    for h in range(n_heads):
        decay_rows = jnp.concatenate(
            [qat[h * AUG_W:(h + 1) * AUG_W], jnp.zeros((pad_rows, tm), F32)], axis=0)
        qt_ref[0, h, HEAD_DIM:HEAD_DIM + 2 * SUBLANES, :] = decay_rows.astype(BF16)
        qt_ref[0, h, HEAD_DIM + 2 * SUBLANES:, :] = jnp.zeros(
            (LANES - HEAD_DIM - 2 * SUBLANES, tm), BF16)
        kh = k[:, (h // 2) * LANES:(h // 2 + 1) * LANES]
        if h % 2:
            kh = pltpu.roll(kh, HEAD_DIM, axis=1)
        kah = pltpu.roll(ka, HEAD_DIM - h * AUG_W, axis=1)
        k_ref[0, :, h * LANES:(h + 1) * LANES] = jnp.where(
            lane < HEAD_DIM, kh, jnp.where(lane < HEAD_DIM + AUG_W, kah, 0.0)).astype(BF16)

    uv = jax.nn.gelu(jnp.dot(xn, wuv_ref[...], preferred_element_type=F32))
    u = uv[:, :gm_w]
    vg = uv[:, gm_w:]
    sq = (vg * vg).astype(BF16)
    gsum = gsum_ref[...]
    msg = jnp.concatenate(
        [jnp.dot(sq[:, s * MXU_DIM:(s + 1) * MXU_DIM], gsum, preferred_element_type=F32)
         for s in range(gm_w // MXU_DIM)], axis=1)
    vgn = ((vg * lax.rsqrt(msg + EPS)) * gn_ref[...]).astype(BF16)

    groups_per_slab = MXU_DIM // GROUP_DIM
    lane_grp = lax.broadcasted_iota(jnp.int32, (CHUNK, MXU_DIM), 1) // GROUP_DIM
    w_masked = [jnp.where(causal, ws_ref[g], 0.0).astype(BF16) for g in range(ws_ref.shape[0])]
    mixed_rows = []
    for r in range(tm // CHUNK):
        slabs = []
        for s in range(gm_w // MXU_DIM):
            vs = vgn[r * CHUNK:(r + 1) * CHUNK, s * MXU_DIM:(s + 1) * MXU_DIM]
            m = None
            for j in range(groups_per_slab):
                mj = jnp.dot(w_masked[s * groups_per_slab + j], vs, preferred_element_type=F32)
                m = mj if m is None else jnp.where(lane_grp == j, mj, m)
            slabs.append(m)
        mixed_rows.append(jnp.concatenate(slabs, axis=1) + bs_ref[...])
    mixed = jnp.concatenate(mixed_rows, axis=0)
    sg_ref[0] = (u * mixed).astype(BF16)


def _proj_call(x, g, wk, wqvt, wuv, wg, bfr, gn, ws, bs, gsum):
    b, s, d = x.shape
    tm = PROJ_TILE
    att_w = wk.shape[1]
    n_heads = att_w // HEAD_DIM
    gm_w = wuv.shape[1] // 2
    tile = lambda w: pl.BlockSpec((1, tm, w), lambda i, j: (i, j, 0))
    head_t = lambda rows: pl.BlockSpec((1, n_heads, rows, tm), lambda i, j: (i, 0, 0, j))
    out_shape = [jax.ShapeDtypeStruct((b, n_heads, LANES, s), BF16),
                 jax.ShapeDtypeStruct((b, s, n_heads * LANES), BF16),
                 jax.ShapeDtypeStruct((b, n_heads, LANES, s), BF16),
                 jax.ShapeDtypeStruct((b, s, gm_w), BF16)]
    return pl.pallas_call(
        functools.partial(_proj_kernel, att_w=att_w, gm_w=gm_w),
        grid=(b, s // tm),
        in_specs=[tile(d), _const_spec(g.shape), _const_spec(wk.shape), _const_spec(wqvt.shape),
                  _const_spec(wuv.shape), _const_spec(wg.shape), _const_spec(bfr.shape),
                  _const_spec(gn.shape), _const_spec(ws.shape), _const_spec(bs.shape),
                  _const_spec(gsum.shape)],
        out_specs=[head_t(LANES), tile(n_heads * LANES), head_t(LANES), tile(gm_w)],
        out_shape=out_shape,
        scratch_shapes=[pltpu.VMEM((SUBLANES, LANES), F32)],
        compiler_params=pltpu.CompilerParams(
            dimension_semantics=("arbitrary", "arbitrary"), vmem_limit_bytes=VMEM_LIMIT_BYTES),
        name="proj",
    )(x, g, wk, wqvt, wuv, wg, bfr, gn, ws, bs, gsum)


def _attn_kernel(qt_ref, k_ref, vt_ref, o_ref, m_ref, acc_ref, s_ref):
    t = s_ref.shape[2]
    heads = qt_ref.shape[1]
    key = lax.broadcasted_iota(jnp.int32, (t, t), 0)
    qry = lax.broadcasted_iota(jnp.int32, (t, t), 1)
    for odd in range(2):
        _attn_query_tile(qt_ref, k_ref, vt_ref, o_ref, m_ref, acc_ref, s_ref, key, qry,
                         2 * pl.program_id(2) + odd, odd, t, heads)


def _attn_query_tile(qt_ref, k_ref, vt_ref, o_ref, m_ref, acc_ref, s_ref, key, qry, qi, odd, t,
                     heads):
    cols = slice(odd * t, (odd + 1) * t)
    m_ref[...] = jnp.full_like(m_ref, -jnp.inf)
    acc_ref[...] = jnp.zeros_like(acc_ref)

    def scores(tile, par):
        start = pl.multiple_of(tile * t, t)
        for i in range(heads):
            kf = k_ref[0, pl.ds(start, t), i * LANES:(i + 1) * LANES]
            s_ref[par, i] = jnp.dot(kf, qt_ref[0, i, :, cols],
                                    preferred_element_type=F32)

    def softmax_pv(tile, par, diagonal=False):
        start = pl.multiple_of(tile * t, t)
        for i in range(heads):
            vt = vt_ref[0, i, :, pl.ds(start, t)]
            s = s_ref[par, i]
            if diagonal:
                s = jnp.where(qry >= key, s, -jnp.inf)
            m_prev = m_ref[i]
            m_next = jnp.maximum(m_prev, jnp.max(s, axis=0, keepdims=True))
            alpha = jnp.exp2(m_prev - m_next)
            p = jnp.exp2(s - m_next).astype(BF16)
            pv = jnp.dot(vt, p, preferred_element_type=F32)
            acc_ref[i] = acc_ref[i] * alpha + pv
            m_ref[i] = m_next

    scores(0, 0)
    n_pairs = qi // 2

    def tile_pair(i):
        scores(2 * i + 1, 1)
        softmax_pv(2 * i, 0)
        scores(2 * i + 2, 0)
        softmax_pv(2 * i + 1, 1)

    def body(i, carry):
        tile_pair(2 * i)
        tile_pair(2 * i + 1)
        return carry

    lax.fori_loop(0, n_pairs // 2, body, 0)

    @pl.when(n_pairs % 2 == 1)
    def _():
        tile_pair(n_pairs - 1)

    if odd:
        scores(qi, 1)
        softmax_pv(qi - 1, 0)
        softmax_pv(qi, 1, diagonal=True)
    else:
        softmax_pv(qi, 0, diagonal=True)

    outs = [acc_ref[i, :HEAD_DIM, :] / acc_ref[i, HEAD_DIM:, :] for i in range(heads)]
    o_ref[0, cols, :] = jnp.concatenate(outs, axis=0).T.astype(o_ref.dtype)


def _attn_call(qt, k, vt):
    b, n_heads, _, s = qt.shape
    t = ATT_TILE
    heads = LANES // HEAD_DIM
    qspec = pl.BlockSpec((1, heads, LANES, 2 * t), lambda i, p, j: (i, p, 0, j))
    kspec = pl.BlockSpec((1, s, heads * LANES), lambda i, p, j: (i, 0, p))
    vspec = pl.BlockSpec((1, heads, LANES, s), lambda i, p, j: (i, p, 0, 0))
    return pl.pallas_call(
        _attn_kernel,
        grid=(b, n_heads // heads, s // (2 * t)),
        in_specs=[qspec, kspec, vspec],
        out_specs=pl.BlockSpec((1, 2 * t, LANES), lambda i, p, j: (i, j, p)),
        out_shape=jax.ShapeDtypeStruct((b, s, n_heads * HEAD_DIM), BF16),
        scratch_shapes=[pltpu.VMEM((heads, 1, t), F32),
                        pltpu.VMEM((heads, LANES, t), F32),
                        pltpu.VMEM((2, heads, t, t), F32)],
        compiler_params=pltpu.CompilerParams(
            dimension_semantics=("arbitrary", "arbitrary", "arbitrary"),
            vmem_limit_bytes=VMEM_LIMIT_BYTES),
        name="attn",
    )(qt, k, vt)


def _ffn_kernel(x_ref, att_ref, sg_ref, wo_ref, g2_ref, wup_ref, cw_ref, cb_ref, wdn_ref, g3_ref,
                o_ref, hn_ref, hbuf_ref, act_ref, acc_ref, carry_ref, *, final_norm):
    tm = x_ref.shape[1]
    n_chunks = wup_ref.shape[0]
    halo = SUBLANES

    @pl.when(pl.program_id(1) == 0)
    def _():
        carry_ref[...] = jnp.zeros_like(carry_ref)

    mix = jnp.concatenate([att_ref[0], sg_ref[0]], axis=1)
    h1 = x_ref[0] + jnp.dot(mix, wo_ref[...], preferred_element_type=F32)
    ms = jnp.mean(h1 * h1, axis=-1, keepdims=True)
    hn_ref[...] = ((h1 * lax.rsqrt(ms + EPS)) * g2_ref[...]).astype(BF16)

    acc_ref[...] = h1

    def up(c):
        hbuf = hbuf_ref.at[c % 2]
        hbuf[0:halo, :] = carry_ref[c]
        hbuf[halo:, :] = jnp.dot(hn_ref[...], wup_ref[c], preferred_element_type=F32)
        carry_ref[c] = hbuf[tm:tm + halo, :]

    def gate(c):
        hbuf = hbuf_ref.at[c % 2]
        cw = cw_ref[c]
        y = cb_ref[c]
        for tap in range(CONV_WIDTH):
            shift = CONV_WIDTH - 1 - tap
            y = y + hbuf[halo - shift:halo - shift + tm, :] * cw[tap:tap + 1, :]
        a = y[:, :FF_CHUNK]
        g = y[:, FF_CHUNK:]
        act_ref[:, c * FF_CHUNK:(c + 1) * FF_CHUNK] = (jax.nn.silu(g) * a).astype(BF16)

    def down(c0, c1):
        lo, hi = c0 * FF_CHUNK, c1 * FF_CHUNK
        return jnp.dot(act_ref[:, lo:hi], wdn_ref[lo:hi, :], preferred_element_type=F32)

    group_ends = [c for c in range(DOWN_GROUP, n_chunks, DOWN_GROUP)]
    done = 0
    up(0)
    for c in range(n_chunks):
        if c + 1 < n_chunks:
            up(c + 1)
        if c in group_ends:
            acc_ref[...] += down(done, c)
            done = c
        gate(c)

    h2 = acc_ref[...] + down(done, n_chunks)
    if final_norm:
        ms2 = jnp.mean(h2 * h2, axis=-1, keepdims=True)
        h2 = (h2 * lax.rsqrt(ms2 + EPS)) * g3_ref[...]
    o_ref[0] = h2


def _ffn_call(x, att, sg, wo, g2, wup, cw, cb, wdn, g3, final_norm):
    b, s, d = x.shape
    tm = TOKEN_TILE
    n_chunks = wup.shape[0]
    d_ff = wdn.shape[0]
    tile = lambda w: pl.BlockSpec((1, tm, w), lambda i, j: (i, j, 0))
    return pl.pallas_call(
        functools.partial(_ffn_kernel, final_norm=final_norm),
        grid=(b, s // tm),
        in_specs=[tile(d), tile(att.shape[2]), tile(sg.shape[2]), _const_spec(wo.shape),
                  _const_spec(g2.shape), _const_spec(wup.shape), _const_spec(cw.shape),
                  _const_spec(cb.shape), _const_spec(wdn.shape), _const_spec(g3.shape)],
        out_specs=tile(d),
        out_shape=jax.ShapeDtypeStruct((b, s, d), F32),
        scratch_shapes=[pltpu.VMEM((tm, d), BF16),
                        pltpu.VMEM((2, tm + SUBLANES, 2 * FF_CHUNK), F32),
                        pltpu.VMEM((tm, d_ff), BF16),
                        pltpu.VMEM((tm, d), F32),
                        pltpu.VMEM((n_chunks, SUBLANES, 2 * FF_CHUNK), F32)],
        compiler_params=pltpu.CompilerParams(
            dimension_semantics=("arbitrary", "arbitrary"), vmem_limit_bytes=VMEM_LIMIT_BYTES),
        name="ffn",
    )(x, att, sg, wo, g2, wup, cw, cb, wdn, g3)


def kernel(x, norm_mix_g, w_in, b_forget, gmlp_norm_g, w_spatial, b_spatial, w_out, norm_ffn_g,
           w_up, conv_w, conv_b, w_down, norm_final_g):
    depth, d_model, _ = w_in.shape
    n_heads = b_forget.shape[1]
    att_w = n_heads * HEAD_DIM
    gm_w = gmlp_norm_g.shape[1]
    n_groups = w_spatial.shape[1]
    d_ff = w_down.shape[1]
    assert gm_w == n_groups * GROUP_DIM and w_spatial.shape[2] == CHUNK
    assert n_heads * AUG_W <= LANES and att_w % LANES == 0 and gm_w % MXU_DIM == 0
    assert d_ff % FF_CHUNK == 0 and x.shape[1] % TOKEN_TILE == 0 and x.shape[1] % PROJ_TILE == 0
    assert x.shape[1] % (2 * ATT_TILE) == 0
    n_chunks = d_ff // FF_CHUNK
    scale = HEAD_DIM ** -0.5

    gi = jnp.arange(MXU_DIM) // GROUP_DIM
    gsum = jnp.where(gi[:, None] == gi[None, :], 1.0 / GROUP_DIM, 0.0).astype(BF16)

    h = x
    for layer in range(depth):
        w = w_in[layer]
        wq = w[:, :att_w] * scale
        wk = w[:, att_w:2 * att_w].astype(BF16)
        wqvt = jnp.concatenate([wq, w[:, 2 * att_w:3 * att_w]], axis=1).T.astype(BF16)
        wuv = w[:, 3 * att_w:3 * att_w + 2 * gm_w].astype(BF16)
        wgate = w[:, 3 * att_w + 2 * gm_w:]
        wg = jnp.zeros((d_model, LANES), F32).at[:, :n_heads * AUG_W].set(
            jnp.repeat(wgate, AUG_W, axis=1)).astype(BF16)
        bfr = jnp.zeros((1, LANES), F32).at[0, :n_heads * AUG_W].set(
            jnp.repeat(b_forget[layer], AUG_W))
        bs = jnp.repeat(b_spatial[layer].T, GROUP_DIM, axis=1)

        qt, k, vt, sg = _proj_call(
            h, norm_mix_g[layer][None, :], wk, wqvt, wuv, wg, bfr, gmlp_norm_g[layer][None, :],
            w_spatial[layer], bs, gsum)
        att = _attn_call(qt, k, vt)

        def chunked(m):
            a = m[..., :d_ff].reshape(m.shape[:-1] + (n_chunks, FF_CHUNK))
            g = m[..., d_ff:].reshape(m.shape[:-1] + (n_chunks, FF_CHUNK))
            return jnp.moveaxis(jnp.concatenate([a, g], axis=-1), -2, 0)
        wup = chunked(w_up[layer]).astype(BF16)
        cw = chunked(conv_w[layer])
        cb = chunked(conv_b[layer][None, :])
        h = _ffn_call(h, att, sg, w_out[layer].astype(BF16), norm_ffn_g[layer][None, :], wup, cw, cb,
                      w_down[layer].astype(BF16), norm_final_g[None, :], layer == depth - 1)
    return h
```

```python
import functools

import jax
import jax.numpy as jnp
from jax import lax
from jax.experimental import pallas as pl
from jax.experimental.pallas import tpu as pltpu

EPS = 1e-6
LOG2E = 1.4426950408889634
HEAD_DIM = 64
GROUP_DIM = 64
CHUNK = 128
CONV_WIDTH = 3

LANES = 128
SUBLANES = 8
MXU_DIM = 256
VMEM_LIMIT_BYTES = 56 * 1024 * 1024

AUG_W = 8
PROJ_TILE = 1024
TOKEN_TILE = 512
ATT_TILE = 512
FF_CHUNK = 256

F32 = jnp.float32
BF16 = jnp.bfloat16


def _split3(x):
    hi = x.astype(BF16)
    r1 = x - hi.astype(F32)
    mid = r1.astype(BF16)
    lo = (r1 - mid.astype(F32)).astype(BF16)
    return hi, mid, lo


def _const_spec(shape):
    zeros = (0,) * len(shape)
    return pl.BlockSpec(shape, lambda *_: zeros, pipeline_mode=pl.Buffered(1))


def _proj_kernel(x_ref, g_ref, wk_ref, wqvt_ref, wuv_ref, wg_ref, bf_ref, gn_ref, ws_ref, bs_ref,
                 gsum_ref, qt_ref, k_ref, vt_ref, sg_ref, carry_ref, *, att_w, gm_w):
    tm = x_ref.shape[1]

    @pl.when(pl.program_id(1) == 0)
    def _():
        carry_ref[...] = jnp.zeros_like(carry_ref)

    x = x_ref[0]
    ms = jnp.mean(x * x, axis=-1, keepdims=True)
    xn = ((x * lax.rsqrt(ms + EPS)) * g_ref[...]).astype(BF16)

    n_heads = att_w // HEAD_DIM
    k = jnp.dot(xn, wk_ref[...], preferred_element_type=F32)
    qvt = lax.dot_general(wqvt_ref[...], xn, (((1,), (1,)), ((), ())), preferred_element_type=F32)
    qt = qvt[:att_w] * LOG2E
    vt = qvt[att_w:]
    for h in range(n_heads):
        qt_ref[0, h, :HEAD_DIM, :] = qt[h * HEAD_DIM:(h + 1) * HEAD_DIM].astype(BF16)
        vt_ref[0, h, :HEAD_DIM, :] = vt[h * HEAD_DIM:(h + 1) * HEAD_DIM].astype(BF16)
        vt_ref[0, h, HEAD_DIM:, :] = jnp.ones((LANES - HEAD_DIM, tm), BF16)

    z = jnp.dot(xn, wg_ref[...], preferred_element_type=F32) + bf_ref[...]
    log_f = -(jnp.maximum(-z, 0.0) + jnp.log1p(jnp.exp(-jnp.abs(z))))

    row = lax.broadcasted_iota(jnp.int32, (CHUNK, CHUNK), 0)
    col = lax.broadcasted_iota(jnp.int32, (CHUNK, CHUNK), 1)
    causal = row >= col
    tri = jnp.where(causal, 1.0, 0.0).astype(BF16)
    offset = carry_ref[0:1, :]
    c_blocks = []
    for r in range(tm // CHUNK):
        blk = log_f[r * CHUNK:(r + 1) * CHUNK]
        s = offset
        for piece in _split3(blk):
            s = s + jnp.dot(tri, piece, preferred_element_type=F32)
        c_blocks.append(s)
        offset = s[CHUNK - 1:CHUNK, :]
    carry_ref[0:1, :] = offset
    c = jnp.concatenate(c_blocks, axis=0)

    c_hi, c_mid, c_lo = [piece.astype(F32) for piece in _split3(c * LOG2E)]
    slot = lax.broadcasted_iota(jnp.int32, (tm, LANES), 1) % AUG_W
    qa = jnp.where(slot == 0, c_hi, jnp.where(slot == 1, c_mid, jnp.where(slot == 2, c_lo,
         jnp.where(slot < 6, 1.0, 0.0))))
    ka = jnp.where(slot < 3, 1.0, jnp.where(slot == 3, -c_hi, jnp.where(slot == 4, -c_mid,
         jnp.where(slot == 5, -c_lo, 0.0))))
    qat = qa.T
    lane = lax.broadcasted_iota(jnp.int32, (tm, LANES), 1)
    pad_rows = 2 * SUBLANES - AUG_W
    for h in range(n_heads):
        decay_rows = jnp.concatenate(
            [qat[h * AUG_W:(h + 1) * AUG_W], jnp.zeros((pad_rows, tm), F32)], axis=0)
        qt_ref[0, h, HEAD_DIM:HEAD_DIM + 2 * SUBLANES, :] = decay_rows.astype(BF16)
        qt_ref[0, h, HEAD_DIM + 2 * SUBLANES:, :] = jnp.zeros(
            (LANES - HEAD_DIM - 2 * SUBLANES, tm), BF16)
        kh = k[:, (h // 2) * LANES:(h // 2 + 1) * LANES]
        if h % 2:
            kh = pltpu.roll(kh, HEAD_DIM, axis=1)
        kah = pltpu.roll(ka, HEAD_DIM - h * AUG_W, axis=1)
        k_ref[0, :, h * LANES:(h + 1) * LANES] = jnp.where(
            lane < HEAD_DIM, kh, jnp.where(lane < HEAD_DIM + AUG_W, kah, 0.0)).astype(BF16)

    uv = jax.nn.gelu(jnp.dot(xn, wuv_ref[...], preferred_element_type=F32))
    u = uv[:, :gm_w]
    vg = uv[:, gm_w:]
    sq = (vg * vg).astype(BF16)
    gsum = gsum_ref[...]
    msg = jnp.concatenate(
        [jnp.dot(sq[:, s * MXU_DIM:(s + 1) * MXU_DIM], gsum, preferred_element_type=F32)
         for s in range(gm_w // MXU_DIM)], axis=1)
    vgn = ((vg * lax.rsqrt(msg + EPS)) * gn_ref[...]).astype(BF16)

    groups_per_slab = MXU_DIM // GROUP_DIM
    lane_grp = lax.broadcasted_iota(jnp.int32, (CHUNK, MXU_DIM), 1) // GROUP_DIM
    w_masked = [jnp.where(causal, ws_ref[g], 0.0).astype(BF16) for g in range(ws_ref.shape[0])]
    mixed_rows = []
    for r in range(tm // CHUNK):
        slabs = []
        for s in range(gm_w // MXU_DIM):
            vs = vgn[r * CHUNK:(r + 1) * CHUNK, s * MXU_DIM:(s + 1) * MXU_DIM]
            m = None
            for j in range(groups_per_slab):
                mj = jnp.dot(w_masked[s * groups_per_slab + j], vs, preferred_element_type=F32)
                m = mj if m is None else jnp.where(lane_grp == j, mj, m)
            slabs.append(m)
        mixed_rows.append(jnp.concatenate(slabs, axis=1) + bs_ref[...])
    mixed = jnp.concatenate(mixed_rows, axis=0)
    sg_ref[0] = (u * mixed).astype(BF16)


def _proj_call(x, g, wk, wqvt, wuv, wg, bfr, gn, ws, bs, gsum):
    b, s, d = x.shape
    tm = PROJ_TILE
    att_w = wk.shape[1]
    n_heads = att_w // HEAD_DIM
    gm_w = wuv.shape[1] // 2
    tile = lambda w: pl.BlockSpec((1, tm, w), lambda i, j: (i, j, 0))
    head_t = lambda rows: pl.BlockSpec((1, n_heads, rows, tm), lambda i, j: (i, 0, 0, j))
    out_shape = [jax.ShapeDtypeStruct((b, n_heads, LANES, s), BF16),
                 jax.ShapeDtypeStruct((b, s, n_heads * LANES), BF16),
                 jax.ShapeDtypeStruct((b, n_heads, LANES, s), BF16),
                 jax.ShapeDtypeStruct((b, s, gm_w), BF16)]
    return pl.pallas_call(
        functools.partial(_proj_kernel, att_w=att_w, gm_w=gm_w),
        grid=(b, s // tm),
        in_specs=[tile(d), _const_spec(g.shape), _const_spec(wk.shape), _const_spec(wqvt.shape),
                  _const_spec(wuv.shape), _const_spec(wg.shape), _const_spec(bfr.shape),
                  _const_spec(gn.shape), _const_spec(ws.shape), _const_spec(bs.shape),
                  _const_spec(gsum.shape)],
        out_specs=[head_t(LANES), tile(n_heads * LANES), head_t(LANES), tile(gm_w)],
        out_shape=out_shape,
        scratch_shapes=[pltpu.VMEM((SUBLANES, LANES), F32)],
        compiler_params=pltpu.CompilerParams(
            dimension_semantics=("arbitrary", "arbitrary"), vmem_limit_bytes=VMEM_LIMIT_BYTES),
        name="proj",
    )(x, g, wk, wqvt, wuv, wg, bfr, gn, ws, bs, gsum)


def _attn_kernel(qt_ref, k_ref, vt_ref, o_ref, m_ref, acc_ref, s_ref):
    t = s_ref.shape[2]
    heads = qt_ref.shape[1]
    key = lax.broadcasted_iota(jnp.int32, (t, t), 0)
    qry = lax.broadcasted_iota(jnp.int32, (t, t), 1)
    for odd in range(2):
        _attn_query_tile(qt_ref, k_ref, vt_ref, o_ref, m_ref, acc_ref, s_ref, key, qry,
                         2 * pl.program_id(2) + odd, odd, t, heads)


def _attn_query_tile(qt_ref, k_ref, vt_ref, o_ref, m_ref, acc_ref, s_ref, key, qry, qi, odd, t,
                     heads):
    cols = slice(odd * t, (odd + 1) * t)
    m_ref[...] = jnp.full_like(m_ref, -jnp.inf)
    acc_ref[...] = jnp.zeros_like(acc_ref)

    def scores(tile, par):
        start = pl.multiple_of(tile * t, t)
        for i in range(heads):
            kf = k_ref[0, pl.ds(start, t), i * LANES:(i + 1) * LANES]
            s_ref[par, i] = jnp.dot(kf, qt_ref[0, i, :, cols],
                                    preferred_element_type=F32)

    def softmax_pv(tile, par, diagonal=False):
        start = pl.multiple_of(tile * t, t)
        for i in range(heads):
            vt = vt_ref[0, i, :, pl.ds(start, t)]
            s = s_ref[par, i]
            if diagonal:
                s = jnp.where(qry >= key, s, -jnp.inf)
            m_prev = m_ref[i]
            m_next = jnp.maximum(m_prev, jnp.max(s, axis=0, keepdims=True))
            alpha = jnp.exp2(m_prev - m_next)
            p = jnp.exp2(s - m_next).astype(BF16)
            pv = jnp.dot(vt, p, preferred_element_type=F32)
            acc_ref[i] = acc_ref[i] * alpha + pv
            m_ref[i] = m_next

    scores(0, 0)
    n_pairs = qi // 2

    def tile_pair(i):
        scores(2 * i + 1, 1)
        softmax_pv(2 * i, 0)
        scores(2 * i + 2, 0)
        softmax_pv(2 * i + 1, 1)

    def body(i, carry):
        tile_pair(2 * i)
        tile_pair(2 * i + 1)
        return carry

    lax.fori_loop(0, n_pairs // 2, body, 0)

    @pl.when(n_pairs % 2 == 1)
    def _():
        tile_pair(n_pairs - 1)

    if odd:
        scores(qi, 1)
        softmax_pv(qi - 1, 0)
        softmax_pv(qi, 1, diagonal=True)
    else:
        softmax_pv(qi, 0, diagonal=True)

    outs = [acc_ref[i, :HEAD_DIM, :] / acc_ref[i, HEAD_DIM:, :] for i in range(heads)]
    o_ref[0, cols, :] = jnp.concatenate(outs, axis=0).T.astype(o_ref.dtype)


def _attn_call(qt, k, vt):
    b, n_heads, _, s = qt.shape
    t = ATT_TILE
    heads = LANES // HEAD_DIM
    qspec = pl.BlockSpec((1, heads, LANES, 2 * t), lambda i, p, j: (i, p, 0, j))
    kspec = pl.BlockSpec((1, s, heads * LANES), lambda i, p, j: (i, 0, p))
    vspec = pl.BlockSpec((1, heads, LANES, s), lambda i, p, j: (i, p, 0, 0))
    return pl.pallas_call(
        _attn_kernel,
        grid=(b, n_heads // heads, s // (2 * t)),
        in_specs=[qspec, kspec, vspec],
        out_specs=pl.BlockSpec((1, 2 * t, LANES), lambda i, p, j: (i, j, p)),
        out_shape=jax.ShapeDtypeStruct((b, s, n_heads * HEAD_DIM), BF16),
        scratch_shapes=[pltpu.VMEM((heads, 1, t), F32),
                        pltpu.VMEM((heads, LANES, t), F32),
                        pltpu.VMEM((2, heads, t, t), F32)],
        compiler_params=pltpu.CompilerParams(
            dimension_semantics=("arbitrary", "arbitrary", "arbitrary"),
            vmem_limit_bytes=VMEM_LIMIT_BYTES),
        name="attn",
    )(qt, k, vt)


def _store_lane_tiles(ref, value):
    for c in range(ref.shape[0]):
        ref[c] = value[:, c * LANES:(c + 1) * LANES]


def _load_row_groups(ref, starts_strides):
    return jnp.concatenate(
        [jnp.concatenate([ref[c, pl.ds(start, SUBLANES, stride=stride), :]
                          for c in range(ref.shape[0])], axis=1)
         for start, stride in starts_strides], axis=0)


def _ffn_kernel(x_ref, att_ref, sg_ref, wo_ref, g2_ref, wup_ref, cw_ref, cb_ref, wdn_ref, g3_ref,
                o_ref, h1_ref, hnat_ref, hn_ref, hbuf_ref, act_ref, res_ref, carry_ref, *,
                final_norm):
    tm = x_ref.shape[1]
    n_chunks = wup_ref.shape[0]
    groups = tm // SUBLANES
    halo = (CONV_WIDTH - 1) * SUBLANES

    @pl.when(pl.program_id(1) == 0)
    def _():
        carry_ref[...] = jnp.zeros_like(carry_ref)

    mix = jnp.concatenate([att_ref[0], sg_ref[0]], axis=1)
    h1 = x_ref[0] + jnp.dot(mix, wo_ref[...], preferred_element_type=F32)
    h1_ref[...] = h1
    ms = jnp.mean(h1 * h1, axis=-1, keepdims=True)
    _store_lane_tiles(hnat_ref, (h1 * lax.rsqrt(ms + EPS)) * g2_ref[...])

    hn_ref[...] = _load_row_groups(hnat_ref, [(k, groups) for k in range(groups)]).astype(BF16)
    first_sublane = lax.broadcasted_iota(jnp.int32, (SUBLANES, 2 * FF_CHUNK), 0) == 0

    def up(c):
        hbuf = hbuf_ref.at[c % 2]
        hbuf[halo:, :] = jnp.dot(hn_ref[...], wup_ref[c], preferred_element_type=F32)
        for j in range(CONV_WIDTH - 1):
            prev_tile = carry_ref[c, j * SUBLANES:(j + 1) * SUBLANES, :]
            this_tile = hbuf[tm + j * SUBLANES:tm + (j + 1) * SUBLANES, :]
            hbuf[j * SUBLANES:(j + 1) * SUBLANES, :] = jnp.where(
                first_sublane, pltpu.roll(prev_tile, 1, axis=0), pltpu.roll(this_tile, 1, axis=0))
        carry_ref[c] = hbuf[tm:tm + halo, :]

    def gate(c):
        hbuf = hbuf_ref.at[c % 2]
        cw = cw_ref[c]
        y = cb_ref[c]
        for tap in range(CONV_WIDTH):
            lo = tap * SUBLANES
            y = y + hbuf[lo:lo + tm, :] * cw[tap:tap + 1, :]
        a = y[:, :FF_CHUNK]
        g = y[:, FF_CHUNK:]
        act_ref[:, c * FF_CHUNK:(c + 1) * FF_CHUNK] = (jax.nn.silu(g) * a).astype(BF16)

    up(0)
    for c in range(n_chunks):
        if c + 1 < n_chunks:
            up(c + 1)
        gate(c)

    _store_lane_tiles(res_ref, jnp.dot(act_ref[...], wdn_ref[...], preferred_element_type=F32))
    ffn = _load_row_groups(
        res_ref, [(SUBLANES * ((SUBLANES * k) % groups) + (SUBLANES * k) // groups, SUBLANES)
                  for k in range(groups)])
    h2 = h1_ref[...] + ffn
    if final_norm:
        ms2 = jnp.mean(h2 * h2, axis=-1, keepdims=True)
        h2 = (h2 * lax.rsqrt(ms2 + EPS)) * g3_ref[...]
    o_ref[0] = h2


def _ffn_call(x, att, sg, wo, g2, wup, cw, cb, wdn, g3, final_norm):
    b, s, d = x.shape
    tm = TOKEN_TILE
    n_chunks = wup.shape[0]
    d_ff = wdn.shape[0]
    halo = (CONV_WIDTH - 1) * SUBLANES
    tile = lambda w: pl.BlockSpec((1, tm, w), lambda i, j: (i, j, 0))
    return pl.pallas_call(
        functools.partial(_ffn_kernel, final_norm=final_norm),
        grid=(b, s // tm),
        in_specs=[tile(d), tile(att.shape[2]), tile(sg.shape[2]), _const_spec(wo.shape),
                  _const_spec(g2.shape), _const_spec(wup.shape), _const_spec(cw.shape),
                  _const_spec(cb.shape), _const_spec(wdn.shape), _const_spec(g3.shape)],
        out_specs=tile(d),
        out_shape=jax.ShapeDtypeStruct((b, s, d), F32),
        scratch_shapes=[pltpu.VMEM((tm, d), F32),
                        pltpu.VMEM((d // LANES, tm, LANES), F32),
                        pltpu.VMEM((tm, d), BF16),
                        pltpu.VMEM((2, tm + halo, 2 * FF_CHUNK), F32),
                        pltpu.VMEM((tm, d_ff), BF16),
                        pltpu.VMEM((d // LANES, tm, LANES), F32),
                        pltpu.VMEM((n_chunks, halo, 2 * FF_CHUNK), F32)],
        compiler_params=pltpu.CompilerParams(
            dimension_semantics=("arbitrary", "arbitrary"), vmem_limit_bytes=VMEM_LIMIT_BYTES),
        name="ffn",
    )(x, att, sg, wo, g2, wup, cw, cb, wdn, g3)


def kernel(x, norm_mix_g, w_in, b_forget, gmlp_norm_g, w_spatial, b_spatial, w_out, norm_ffn_g,
           w_up, conv_w, conv_b, w_down, norm_final_g):
    depth, d_model, _ = w_in.shape
    n_heads = b_forget.shape[1]
    att_w = n_heads * HEAD_DIM
    gm_w = gmlp_norm_g.shape[1]
    n_groups = w_spatial.shape[1]
    d_ff = w_down.shape[1]
    assert gm_w == n_groups * GROUP_DIM and w_spatial.shape[2] == CHUNK
    assert n_heads * AUG_W <= LANES and att_w % LANES == 0 and gm_w % MXU_DIM == 0
    assert d_ff % FF_CHUNK == 0 and x.shape[1] % TOKEN_TILE == 0 and x.shape[1] % PROJ_TILE == 0
    assert x.shape[1] % (2 * ATT_TILE) == 0
    n_chunks = d_ff // FF_CHUNK
    scale = HEAD_DIM ** -0.5

    gi = jnp.arange(MXU_DIM) // GROUP_DIM
    gsum = jnp.where(gi[:, None] == gi[None, :], 1.0 / GROUP_DIM, 0.0).astype(BF16)

    h = x
    for layer in range(depth):
        w = w_in[layer]
        wq = w[:, :att_w] * scale
        wk = w[:, att_w:2 * att_w].astype(BF16)
        wqvt = jnp.concatenate([wq, w[:, 2 * att_w:3 * att_w]], axis=1).T.astype(BF16)
        wuv = w[:, 3 * att_w:3 * att_w + 2 * gm_w].astype(BF16)
        wgate = w[:, 3 * att_w + 2 * gm_w:]
        wg = jnp.zeros((d_model, LANES), F32).at[:, :n_heads * AUG_W].set(
            jnp.repeat(wgate, AUG_W, axis=1)).astype(BF16)
        bfr = jnp.zeros((1, LANES), F32).at[0, :n_heads * AUG_W].set(
            jnp.repeat(b_forget[layer], AUG_W))
        bs = jnp.repeat(b_spatial[layer].T, GROUP_DIM, axis=1)

        qt, k, vt, sg = _proj_call(
            h, norm_mix_g[layer][None, :], wk, wqvt, wuv, wg, bfr, gmlp_norm_g[layer][None, :],
            w_spatial[layer], bs, gsum)
        att = _attn_call(qt, k, vt)

        def chunked(m):
            a = m[..., :d_ff].reshape(m.shape[:-1] + (n_chunks, FF_CHUNK))
            g = m[..., d_ff:].reshape(m.shape[:-1] + (n_chunks, FF_CHUNK))
            return jnp.moveaxis(jnp.concatenate([a, g], axis=-1), -2, 0)
        wup = chunked(w_up[layer]).astype(BF16)
        cw = chunked(conv_w[layer])
        cb = chunked(conv_b[layer][None, :])
        h = _ffn_call(h, att, sg, w_out[layer].astype(BF16), norm_ffn_g[layer][None, :], wup, cw, cb,
                      w_down[layer].astype(BF16), norm_final_g[None, :], layer == depth - 1)
    return h
```

```python
import functools

import jax
import jax.numpy as jnp
from jax import lax
from jax.experimental import pallas as pl
from jax.experimental.pallas import tpu as pltpu

EPS = 1e-6
LOG2E = 1.4426950408889634
HEAD_DIM = 64
GROUP_DIM = 64
CHUNK = 128
CONV_WIDTH = 3

LANES = 128
SUBLANES = 8
MXU_DIM = 256
VMEM_LIMIT_BYTES = 56 * 1024 * 1024

AUG_W = 8
PROJ_TILE = 1024
TOKEN_TILE = 512
ATT_TILE = 512
Q_TILES_PER_STEP = 4
FF_CHUNK = 256

F32 = jnp.float32
BF16 = jnp.bfloat16


def _split3(x):
    hi = x.astype(BF16)
    r1 = x - hi.astype(F32)
    mid = r1.astype(BF16)
    lo = (r1 - mid.astype(F32)).astype(BF16)
    return hi, mid, lo


def _const_spec(shape):
    zeros = (0,) * len(shape)
    return pl.BlockSpec(shape, lambda *_: zeros, pipeline_mode=pl.Buffered(1))


def _proj_kernel(x_ref, g_ref, wk_ref, wqvt_ref, wuv_ref, wg_ref, bf_ref, gn_ref, ws_ref, bs_ref,
                 gsum_ref, qt_ref, k_ref, vt_ref, sg_ref, carry_ref, *, att_w, gm_w):
    tm = x_ref.shape[1]

    @pl.when(pl.program_id(1) == 0)
    def _():
        carry_ref[...] = jnp.zeros_like(carry_ref)

    x = x_ref[0]
    ms = jnp.mean(x * x, axis=-1, keepdims=True)
    xn = ((x * lax.rsqrt(ms + EPS)) * g_ref[...]).astype(BF16)

    n_heads = att_w // HEAD_DIM
    k = jnp.dot(xn, wk_ref[...], preferred_element_type=F32)
    qvt = lax.dot_general(wqvt_ref[...], xn, (((1,), (1,)), ((), ())), preferred_element_type=F32)
    qt = qvt[:att_w] * LOG2E
    vt = qvt[att_w:]
    for h in range(n_heads):
        qt_ref[0, h, :HEAD_DIM, :] = qt[h * HEAD_DIM:(h + 1) * HEAD_DIM].astype(BF16)
        vt_ref[0, h, :HEAD_DIM, :] = vt[h * HEAD_DIM:(h + 1) * HEAD_DIM].astype(BF16)
        vt_ref[0, h, HEAD_DIM:, :] = jnp.ones((LANES - HEAD_DIM, tm), BF16)

    z = jnp.dot(xn, wg_ref[...], preferred_element_type=F32) + bf_ref[...]
    log_f = -(jnp.maximum(-z, 0.0) + jnp.log1p(jnp.exp(-jnp.abs(z))))

    row = lax.broadcasted_iota(jnp.int32, (CHUNK, CHUNK), 0)
    col = lax.broadcasted_iota(jnp.int32, (CHUNK, CHUNK), 1)
    causal = row >= col
    tri = jnp.where(causal, 1.0, 0.0).astype(BF16)
    offset = carry_ref[0:1, :]
    c_blocks = []
    for r in range(tm // CHUNK):
        blk = log_f[r * CHUNK:(r + 1) * CHUNK]
        s = offset
        for piece in _split3(blk):
            s = s + jnp.dot(tri, piece, preferred_element_type=F32)
        c_blocks.append(s)
        offset = s[CHUNK - 1:CHUNK, :]
    carry_ref[0:1, :] = offset
    c = jnp.concatenate(c_blocks, axis=0)

    c_hi, c_mid, c_lo = [piece.astype(F32) for piece in _split3(c * LOG2E)]
    slot = lax.broadcasted_iota(jnp.int32, (tm, LANES), 1) % AUG_W
    qa = jnp.where(slot == 0, c_hi, jnp.where(slot == 1, c_mid, jnp.where(slot == 2, c_lo,
         jnp.where(slot < 6, 1.0, 0.0))))
    ka = jnp.where(slot < 3, 1.0, jnp.where(slot == 3, -c_hi, jnp.where(slot == 4, -c_mid,
         jnp.where(slot == 5, -c_lo, 0.0))))
    qat = qa.T
    lane = lax.broadcasted_iota(jnp.int32, (tm, LANES), 1)
    pad_rows = 2 * SUBLANES - AUG_W
    for h in range(n_heads):
        decay_rows = jnp.concatenate(
            [qat[h * AUG_W:(h + 1) * AUG_W], jnp.zeros((pad_rows, tm), F32)], axis=0)
        qt_ref[0, h, HEAD_DIM:HEAD_DIM + 2 * SUBLANES, :] = decay_rows.astype(BF16)
        qt_ref[0, h, HEAD_DIM + 2 * SUBLANES:, :] = jnp.zeros(
            (LANES - HEAD_DIM - 2 * SUBLANES, tm), BF16)
        kh = k[:, (h // 2) * LANES:(h // 2 + 1) * LANES]
        if h % 2:
            kh = pltpu.roll(kh, HEAD_DIM, axis=1)
        kah = pltpu.roll(ka, HEAD_DIM - h * AUG_W, axis=1)
        k_ref[0, :, h * LANES:(h + 1) * LANES] = jnp.where(
            lane < HEAD_DIM, kh, jnp.where(lane < HEAD_DIM + AUG_W, kah, 0.0)).astype(BF16)

    uv = jax.nn.gelu(jnp.dot(xn, wuv_ref[...], preferred_element_type=F32))
    u = uv[:, :gm_w]
    vg = uv[:, gm_w:]
    sq = (vg * vg).astype(BF16)
    gsum = gsum_ref[...]
    msg = jnp.concatenate(
        [jnp.dot(sq[:, s * MXU_DIM:(s + 1) * MXU_DIM], gsum, preferred_element_type=F32)
         for s in range(gm_w // MXU_DIM)], axis=1)
    vgn = ((vg * lax.rsqrt(msg + EPS)) * gn_ref[...]).astype(BF16)

    groups_per_slab = MXU_DIM // GROUP_DIM
    lane_grp = lax.broadcasted_iota(jnp.int32, (CHUNK, MXU_DIM), 1) // GROUP_DIM
    w_masked = [jnp.where(causal, ws_ref[g], 0.0).astype(BF16) for g in range(ws_ref.shape[0])]
    mixed_rows = []
    for r in range(tm // CHUNK):
        slabs = []
        for s in range(gm_w // MXU_DIM):
            vs = vgn[r * CHUNK:(r + 1) * CHUNK, s * MXU_DIM:(s + 1) * MXU_DIM]
            m = None
            for j in range(groups_per_slab):
                mj = jnp.dot(w_masked[s * groups_per_slab + j], vs, preferred_element_type=F32)
                m = mj if m is None else jnp.where(lane_grp == j, mj, m)
            slabs.append(m)
        mixed_rows.append(jnp.concatenate(slabs, axis=1) + bs_ref[...])
    mixed = jnp.concatenate(mixed_rows, axis=0)
    sg_ref[0] = (u * mixed).astype(BF16)


def _proj_call(x, g, wk, wqvt, wuv, wg, bfr, gn, ws, bs, gsum):
    b, s, d = x.shape
    tm = PROJ_TILE
    att_w = wk.shape[1]
    n_heads = att_w // HEAD_DIM
    gm_w = wuv.shape[1] // 2
    tile = lambda w: pl.BlockSpec((1, tm, w), lambda i, j: (i, j, 0))
    head_t = lambda rows: pl.BlockSpec((1, n_heads, rows, tm), lambda i, j: (i, 0, 0, j))
    out_shape = [jax.ShapeDtypeStruct((b, n_heads, LANES, s), BF16),
                 jax.ShapeDtypeStruct((b, s, n_heads * LANES), BF16),
                 jax.ShapeDtypeStruct((b, n_heads, LANES, s), BF16),
                 jax.ShapeDtypeStruct((b, s, gm_w), BF16)]
    return pl.pallas_call(
        functools.partial(_proj_kernel, att_w=att_w, gm_w=gm_w),
        grid=(b, s // tm),
        in_specs=[tile(d), _const_spec(g.shape), _const_spec(wk.shape), _const_spec(wqvt.shape),
                  _const_spec(wuv.shape), _const_spec(wg.shape), _const_spec(bfr.shape),
                  _const_spec(gn.shape), _const_spec(ws.shape), _const_spec(bs.shape),
                  _const_spec(gsum.shape)],
        out_specs=[head_t(LANES), tile(n_heads * LANES), head_t(LANES), tile(gm_w)],
        out_shape=out_shape,
        scratch_shapes=[pltpu.VMEM((SUBLANES, LANES), F32)],
        compiler_params=pltpu.CompilerParams(
            dimension_semantics=("arbitrary", "arbitrary"), vmem_limit_bytes=VMEM_LIMIT_BYTES),
        name="proj",
    )(x, g, wk, wqvt, wuv, wg, bfr, gn, ws, bs, gsum)


def _attn_kernel(qt_ref, k_ref, vt_ref, o_ref, m_ref, acc_ref, s_ref):
    t = s_ref.shape[2]
    heads = qt_ref.shape[1]
    key = lax.broadcasted_iota(jnp.int32, (t, t), 0)
    qry = lax.broadcasted_iota(jnp.int32, (t, t), 1)
    for w in range(Q_TILES_PER_STEP):
        _attn_query_tile(qt_ref, k_ref, vt_ref, o_ref, m_ref, acc_ref, s_ref, key, qry,
                         pl.program_id(2), w, t, heads)


def _attn_query_tile(qt_ref, k_ref, vt_ref, o_ref, m_ref, acc_ref, s_ref, key, qry, step, w, t,
                     heads):
    qi = Q_TILES_PER_STEP * step + w
    cols = slice(w * t, (w + 1) * t)
    m_ref[...] = jnp.full_like(m_ref, -jnp.inf)
    acc_ref[...] = jnp.zeros_like(acc_ref)

    def scores(tile, par):
        start = pl.multiple_of(tile * t, t)
        for i in range(heads):
            kf = k_ref[0, pl.ds(start, t), i * LANES:(i + 1) * LANES]
            s_ref[par, i] = jnp.dot(kf, qt_ref[0, i, :, cols],
                                    preferred_element_type=F32)

    def softmax_pv(tile, par, diagonal=False):
        start = pl.multiple_of(tile * t, t)
        for i in range(heads):
            vt = vt_ref[0, i, :, pl.ds(start, t)]
            s = s_ref[par, i]
            if diagonal:
                s = jnp.where(qry >= key, s, -jnp.inf)
            m_prev = m_ref[i]
            m_next = jnp.maximum(m_prev, jnp.max(s, axis=0, keepdims=True))
            alpha = jnp.exp2(m_prev - m_next)
            p = jnp.exp2(s - m_next).astype(BF16)
            pv = jnp.dot(vt, p, preferred_element_type=F32)
            acc_ref[i] = acc_ref[i] * alpha + pv
            m_ref[i] = m_next

    scores(0, 0)

    def tile_pair(i):
        scores(2 * i + 1, 1)
        softmax_pv(2 * i, 0)
        scores(2 * i + 2, 0)
        softmax_pv(2 * i + 1, 1)

    assert Q_TILES_PER_STEP == 4
    def body(i, carry):
        tile_pair(2 * i)
        tile_pair(2 * i + 1)
        return carry

    lax.fori_loop(0, step, body, 0)
    if w // 2:
        tile_pair(2 * step)

    if w % 2:
        scores(qi, 1)
        softmax_pv(qi - 1, 0)
        softmax_pv(qi, 1, diagonal=True)
    else:
        softmax_pv(qi, 0, diagonal=True)

    outs = [acc_ref[i, :HEAD_DIM, :] / acc_ref[i, HEAD_DIM:, :] for i in range(heads)]
    o_ref[0, cols, :] = jnp.concatenate(outs, axis=0).T.astype(o_ref.dtype)


def _attn_call(qt, k, vt):
    b, n_heads, _, s = qt.shape
    t = ATT_TILE
    heads = LANES // HEAD_DIM
    tq = Q_TILES_PER_STEP * t
    qspec = pl.BlockSpec((1, heads, LANES, tq), lambda i, p, j: (i, p, 0, j))
    kspec = pl.BlockSpec((1, s, heads * LANES), lambda i, p, j: (i, 0, p))
    vspec = pl.BlockSpec((1, heads, LANES, s), lambda i, p, j: (i, p, 0, 0))
    return pl.pallas_call(
        _attn_kernel,
        grid=(b, n_heads // heads, s // tq),
        in_specs=[qspec, kspec, vspec],
        out_specs=pl.BlockSpec((1, tq, LANES), lambda i, p, j: (i, j, p)),
        out_shape=jax.ShapeDtypeStruct((b, s, n_heads * HEAD_DIM), BF16),
        scratch_shapes=[pltpu.VMEM((heads, 1, t), F32),
                        pltpu.VMEM((heads, LANES, t), F32),
                        pltpu.VMEM((2, heads, t, t), F32)],
        compiler_params=pltpu.CompilerParams(
            dimension_semantics=("arbitrary", "arbitrary", "arbitrary"),
            vmem_limit_bytes=VMEM_LIMIT_BYTES),
        name="attn",
    )(qt, k, vt)


def _store_lane_tiles(ref, value):
    for c in range(ref.shape[0]):
        ref[c] = value[:, c * LANES:(c + 1) * LANES]


def _load_row_groups(ref, starts_strides):
    return jnp.concatenate(
        [jnp.concatenate([ref[c, pl.ds(start, SUBLANES, stride=stride), :]
                          for c in range(ref.shape[0])], axis=1)
         for start, stride in starts_strides], axis=0)


def _ffn_kernel(x_ref, att_ref, sg_ref, wo_ref, g2_ref, wup_ref, cw_ref, cb_ref, wdn_ref, g3_ref,
                o_ref, h1_ref, hnat_ref, hn_ref, hbuf_ref, act_ref, res_ref, carry_ref, *,
                final_norm):
    tm = x_ref.shape[1]
    n_chunks = wup_ref.shape[0]
    groups = tm // SUBLANES
    halo = (CONV_WIDTH - 1) * SUBLANES

    @pl.when(pl.program_id(1) == 0)
    def _():
        carry_ref[...] = jnp.zeros_like(carry_ref)

    mix = jnp.concatenate([att_ref[0], sg_ref[0]], axis=1)
    h1 = x_ref[0] + jnp.dot(mix, wo_ref[...], preferred_element_type=F32)
    h1_ref[...] = h1
    ms = jnp.mean(h1 * h1, axis=-1, keepdims=True)
    _store_lane_tiles(hnat_ref, (h1 * lax.rsqrt(ms + EPS)) * g2_ref[...])

    hn_ref[...] = _load_row_groups(hnat_ref, [(k, groups) for k in range(groups)]).astype(BF16)
    first_sublane = lax.broadcasted_iota(jnp.int32, (SUBLANES, 2 * FF_CHUNK), 0) == 0

    def up(c):
        hbuf = hbuf_ref.at[c % 2]
        hbuf[halo:, :] = jnp.dot(hn_ref[...], wup_ref[c], preferred_element_type=F32)
        for j in range(CONV_WIDTH - 1):
            prev_tile = carry_ref[c, j * SUBLANES:(j + 1) * SUBLANES, :]
            this_tile = hbuf[tm + j * SUBLANES:tm + (j + 1) * SUBLANES, :]
            hbuf[j * SUBLANES:(j + 1) * SUBLANES, :] = jnp.where(
                first_sublane, pltpu.roll(prev_tile, 1, axis=0), pltpu.roll(this_tile, 1, axis=0))
        carry_ref[c] = hbuf[tm:tm + halo, :]

    def gate(c):
        hbuf = hbuf_ref.at[c % 2]
        cw = cw_ref[c]
        y = cb_ref[c]
        for tap in range(CONV_WIDTH):
            lo = tap * SUBLANES
            y = y + hbuf[lo:lo + tm, :] * cw[tap:tap + 1, :]
        a = y[:, :FF_CHUNK]
        g = y[:, FF_CHUNK:]
        act_ref[:, c * FF_CHUNK:(c + 1) * FF_CHUNK] = (jax.nn.silu(g) * a).astype(BF16)

    up(0)
    for c in range(n_chunks):
        if c + 1 < n_chunks:
            up(c + 1)
        gate(c)

    _store_lane_tiles(res_ref, jnp.dot(act_ref[...], wdn_ref[...], preferred_element_type=F32))
    ffn = _load_row_groups(
        res_ref, [(SUBLANES * ((SUBLANES * k) % groups) + (SUBLANES * k) // groups, SUBLANES)
                  for k in range(groups)])
    h2 = h1_ref[...] + ffn
    if final_norm:
        ms2 = jnp.mean(h2 * h2, axis=-1, keepdims=True)
        h2 = (h2 * lax.rsqrt(ms2 + EPS)) * g3_ref[...]
    o_ref[0] = h2


def _ffn_call(x, att, sg, wo, g2, wup, cw, cb, wdn, g3, final_norm):
    b, s, d = x.shape
    tm = TOKEN_TILE
    n_chunks = wup.shape[0]
    d_ff = wdn.shape[0]
    halo = (CONV_WIDTH - 1) * SUBLANES
    tile = lambda w: pl.BlockSpec((1, tm, w), lambda i, j: (i, j, 0))
    return pl.pallas_call(
        functools.partial(_ffn_kernel, final_norm=final_norm),
        grid=(b, s // tm),
        in_specs=[tile(d), tile(att.shape[2]), tile(sg.shape[2]), _const_spec(wo.shape),
                  _const_spec(g2.shape), _const_spec(wup.shape), _const_spec(cw.shape),
                  _const_spec(cb.shape), _const_spec(wdn.shape), _const_spec(g3.shape)],
        out_specs=tile(d),
        out_shape=jax.ShapeDtypeStruct((b, s, d), F32),
        scratch_shapes=[pltpu.VMEM((tm, d), F32),
                        pltpu.VMEM((d // LANES, tm, LANES), F32),
                        pltpu.VMEM((tm, d), BF16),
                        pltpu.VMEM((2, tm + halo, 2 * FF_CHUNK), F32),
                        pltpu.VMEM((tm, d_ff), BF16),
                        pltpu.VMEM((d // LANES, tm, LANES), F32),
                        pltpu.VMEM((n_chunks, halo, 2 * FF_CHUNK), F32)],
        compiler_params=pltpu.CompilerParams(
            dimension_semantics=("arbitrary", "arbitrary"), vmem_limit_bytes=VMEM_LIMIT_BYTES),
        name="ffn",
    )(x, att, sg, wo, g2, wup, cw, cb, wdn, g3)


def kernel(x, norm_mix_g, w_in, b_forget, gmlp_norm_g, w_spatial, b_spatial, w_out, norm_ffn_g,
           w_up, conv_w, conv_b, w_down, norm_final_g):
    depth, d_model, _ = w_in.shape
    n_heads = b_forget.shape[1]
    att_w = n_heads * HEAD_DIM
    gm_w = gmlp_norm_g.shape[1]
    n_groups = w_spatial.shape[1]
    d_ff = w_down.shape[1]
    assert gm_w == n_groups * GROUP_DIM and w_spatial.shape[2] == CHUNK
    assert n_heads * AUG_W <= LANES and att_w % LANES == 0 and gm_w % MXU_DIM == 0
    assert d_ff % FF_CHUNK == 0 and x.shape[1] % TOKEN_TILE == 0 and x.shape[1] % PROJ_TILE == 0
    assert x.shape[1] % (Q_TILES_PER_STEP * ATT_TILE) == 0
    n_chunks = d_ff // FF_CHUNK
    scale = HEAD_DIM ** -0.5

    gi = jnp.arange(MXU_DIM) // GROUP_DIM
    gsum = jnp.where(gi[:, None] == gi[None, :], 1.0 / GROUP_DIM, 0.0).astype(BF16)

    h = x
    for layer in range(depth):
        w = w_in[layer]
        wq = w[:, :att_w] * scale
        wk = w[:, att_w:2 * att_w].astype(BF16)
        wqvt = jnp.concatenate([wq, w[:, 2 * att_w:3 * att_w]], axis=1).T.astype(BF16)
        wuv = w[:, 3 * att_w:3 * att_w + 2 * gm_w].astype(BF16)
        wgate = w[:, 3 * att_w + 2 * gm_w:]
        wg = jnp.zeros((d_model, LANES), F32).at[:, :n_heads * AUG_W].set(
            jnp.repeat(wgate, AUG_W, axis=1)).astype(BF16)
        bfr = jnp.zeros((1, LANES), F32).at[0, :n_heads * AUG_W].set(
            jnp.repeat(b_forget[layer], AUG_W))
        bs = jnp.repeat(b_spatial[layer].T, GROUP_DIM, axis=1)

        qt, k, vt, sg = _proj_call(
            h, norm_mix_g[layer][None, :], wk, wqvt, wuv, wg, bfr, gmlp_norm_g[layer][None, :],
            w_spatial[layer], bs, gsum)
        att = _attn_call(qt, k, vt)

        def chunked(m):
            a = m[..., :d_ff].reshape(m.shape[:-1] + (n_chunks, FF_CHUNK))
            g = m[..., d_ff:].reshape(m.shape[:-1] + (n_chunks, FF_CHUNK))
            return jnp.moveaxis(jnp.concatenate([a, g], axis=-1), -2, 0)
        wup = chunked(w_up[layer]).astype(BF16)
        cw = chunked(conv_w[layer])
        cb = chunked(conv_b[layer][None, :])
        h = _ffn_call(h, att, sg, w_out[layer].astype(BF16), norm_ffn_g[layer][None, :], wup, cw, cb,
                      w_down[layer].astype(BF16), norm_final_g[None, :], layer == depth - 1)
    return h
```

```python
import functools

import jax
import jax.numpy as jnp
from jax import lax
from jax.experimental import pallas as pl
from jax.experimental.pallas import tpu as pltpu

EPS = 1e-6
LOG2E = 1.4426950408889634
HEAD_DIM = 64
GROUP_DIM = 64
CHUNK = 128
CONV_WIDTH = 3

LANES = 128
SUBLANES = 8
MXU_DIM = 256
VMEM_LIMIT_BYTES = 56 * 1024 * 1024

AUG_W = 8
PROJ_TILE = 1024
TOKEN_TILE = 512
ATT_TILE = 512
Q_TILES_PER_STEP = 4
FF_CHUNK = 256

F32 = jnp.float32
BF16 = jnp.bfloat16


def _split3(x):
    hi = x.astype(BF16)
    r1 = x - hi.astype(F32)
    mid = r1.astype(BF16)
    lo = (r1 - mid.astype(F32)).astype(BF16)
    return hi, mid, lo


def _const_spec(shape):
    zeros = (0,) * len(shape)
    return pl.BlockSpec(shape, lambda *_: zeros, pipeline_mode=pl.Buffered(1))


def _proj_kernel(x_ref, g_ref, wk_ref, wqvt_ref, wuv_ref, wg_ref, bf_ref, gn_ref, ws_ref, bs_ref,
                 gsum_ref, qt_ref, k_ref, vt_ref, sg_ref, carry_ref, *, att_w, gm_w):
    tm = x_ref.shape[1]

    @pl.when(pl.program_id(1) == 0)
    def _():
        carry_ref[...] = jnp.zeros_like(carry_ref)

    x = x_ref[0]
    ms = jnp.mean(x * x, axis=-1, keepdims=True)
    xn = ((x * lax.rsqrt(ms + EPS)) * g_ref[...]).astype(BF16)

    n_heads = att_w // HEAD_DIM
    k = jnp.dot(xn, wk_ref[...], preferred_element_type=F32)
    qvt = lax.dot_general(wqvt_ref[...], xn, (((1,), (1,)), ((), ())), preferred_element_type=F32)
    qt = qvt[:att_w] * LOG2E
    vt = qvt[att_w:]
    for h in range(n_heads):
        qt_ref[0, h, :HEAD_DIM, :] = qt[h * HEAD_DIM:(h + 1) * HEAD_DIM].astype(BF16)
        vt_ref[0, h, :HEAD_DIM, :] = vt[h * HEAD_DIM:(h + 1) * HEAD_DIM].astype(BF16)
        vt_ref[0, h, HEAD_DIM:, :] = jnp.ones((LANES - HEAD_DIM, tm), BF16)

    z = jnp.dot(xn, wg_ref[...], preferred_element_type=F32) + bf_ref[...]
    log_f = -(jnp.maximum(-z, 0.0) + jnp.log1p(jnp.exp(-jnp.abs(z))))

    row = lax.broadcasted_iota(jnp.int32, (CHUNK, CHUNK), 0)
    col = lax.broadcasted_iota(jnp.int32, (CHUNK, CHUNK), 1)
    causal = row >= col
    tri = jnp.where(causal, 1.0, 0.0).astype(BF16)
    offset = carry_ref[0:1, :]
    c_blocks = []
    for r in range(tm // CHUNK):
        blk = log_f[r * CHUNK:(r + 1) * CHUNK]
        s = offset
        for piece in _split3(blk):
            s = s + jnp.dot(tri, piece, preferred_element_type=F32)
        c_blocks.append(s)
        offset = s[CHUNK - 1:CHUNK, :]
    carry_ref[0:1, :] = offset
    c = jnp.concatenate(c_blocks, axis=0)

    c_hi, c_mid, c_lo = [piece.astype(F32) for piece in _split3(c * LOG2E)]
    slot = lax.broadcasted_iota(jnp.int32, (tm, LANES), 1) % AUG_W
    qa = jnp.where(slot == 0, c_hi, jnp.where(slot == 1, c_mid, jnp.where(slot == 2, c_lo,
         jnp.where(slot < 6, 1.0, 0.0))))
    ka = jnp.where(slot < 3, 1.0, jnp.where(slot == 3, -c_hi, jnp.where(slot == 4, -c_mid,
         jnp.where(slot == 5, -c_lo, 0.0))))
    qat = qa.T
    lane = lax.broadcasted_iota(jnp.int32, (tm, LANES), 1)
    pad_rows = 2 * SUBLANES - AUG_W
    for h in range(n_heads):
        decay_rows = jnp.concatenate(
            [qat[h * AUG_W:(h + 1) * AUG_W], jnp.zeros((pad_rows, tm), F32)], axis=0)
        qt_ref[0, h, HEAD_DIM:HEAD_DIM + 2 * SUBLANES, :] = decay_rows.astype(BF16)
        qt_ref[0, h, HEAD_DIM + 2 * SUBLANES:, :] = jnp.zeros(
            (LANES - HEAD_DIM - 2 * SUBLANES, tm), BF16)
        kh = k[:, (h // 2) * LANES:(h // 2 + 1) * LANES]
        if h % 2:
            kh = pltpu.roll(kh, HEAD_DIM, axis=1)
        kah = pltpu.roll(ka, HEAD_DIM - h * AUG_W, axis=1)
        k_ref[0, :, h * LANES:(h + 1) * LANES] = jnp.where(
            lane < HEAD_DIM, kh, jnp.where(lane < HEAD_DIM + AUG_W, kah, 0.0)).astype(BF16)

    uv = jax.nn.gelu(jnp.dot(xn, wuv_ref[...], preferred_element_type=F32))
    u = uv[:, :gm_w]
    vg = uv[:, gm_w:]
    sq = (vg * vg).astype(BF16)
    gsum = gsum_ref[...]
    msg = jnp.concatenate(
        [jnp.dot(sq[:, s * MXU_DIM:(s + 1) * MXU_DIM], gsum, preferred_element_type=F32)
         for s in range(gm_w // MXU_DIM)], axis=1)
    vgn = (vg * lax.rsqrt(msg + EPS)) * gn_ref[...]

    groups_per_slab = MXU_DIM // GROUP_DIM
    lane_grp = lax.broadcasted_iota(jnp.int32, (CHUNK, MXU_DIM), 1) // GROUP_DIM
    w_masked = [jnp.where(causal, ws_ref[g], 0.0).astype(BF16) for g in range(ws_ref.shape[0])]
    w_slab = [jnp.concatenate(w_masked[s * groups_per_slab:(s + 1) * groups_per_slab], axis=1)
              for s in range(gm_w // MXU_DIM)]
    mixed_rows = []
    for r in range(tm // CHUNK):
        slabs = []
        for s in range(gm_w // MXU_DIM):
            vs = vgn[r * CHUNK:(r + 1) * CHUNK, s * MXU_DIM:(s + 1) * MXU_DIM]
            stacked = jnp.concatenate(
                [jnp.where(lane_grp == j, vs, 0.0).astype(BF16) for j in range(groups_per_slab)],
                axis=0)
            slabs.append(jnp.dot(w_slab[s], stacked, preferred_element_type=F32))
        mixed_rows.append(jnp.concatenate(slabs, axis=1) + bs_ref[...])
    mixed = jnp.concatenate(mixed_rows, axis=0)
    sg_ref[0] = (u * mixed).astype(BF16)


def _proj_call(x, g, wk, wqvt, wuv, wg, bfr, gn, ws, bs, gsum):
    b, s, d = x.shape
    tm = PROJ_TILE
    att_w = wk.shape[1]
    n_heads = att_w // HEAD_DIM
    gm_w = wuv.shape[1] // 2
    tile = lambda w: pl.BlockSpec((1, tm, w), lambda i, j: (i, j, 0))
    head_t = lambda rows: pl.BlockSpec((1, n_heads, rows, tm), lambda i, j: (i, 0, 0, j))
    out_shape = [jax.ShapeDtypeStruct((b, n_heads, LANES, s), BF16),
                 jax.ShapeDtypeStruct((b, s, n_heads * LANES), BF16),
                 jax.ShapeDtypeStruct((b, n_heads, LANES, s), BF16),
                 jax.ShapeDtypeStruct((b, s, gm_w), BF16)]
    return pl.pallas_call(
        functools.partial(_proj_kernel, att_w=att_w, gm_w=gm_w),
        grid=(b, s // tm),
        in_specs=[tile(d), _const_spec(g.shape), _const_spec(wk.shape), _const_spec(wqvt.shape),
                  _const_spec(wuv.shape), _const_spec(wg.shape), _const_spec(bfr.shape),
                  _const_spec(gn.shape), _const_spec(ws.shape), _const_spec(bs.shape),
                  _const_spec(gsum.shape)],
        out_specs=[head_t(LANES), tile(n_heads * LANES), head_t(LANES), tile(gm_w)],
        out_shape=out_shape,
        scratch_shapes=[pltpu.VMEM((SUBLANES, LANES), F32)],
        compiler_params=pltpu.CompilerParams(
            dimension_semantics=("arbitrary", "arbitrary"), vmem_limit_bytes=VMEM_LIMIT_BYTES),
        name="proj",
    )(x, g, wk, wqvt, wuv, wg, bfr, gn, ws, bs, gsum)


def _attn_kernel(qt_ref, k_ref, vt_ref, o_ref, m_ref, acc_ref, s_ref):
    t = s_ref.shape[2]
    heads = qt_ref.shape[1]
    key = lax.broadcasted_iota(jnp.int32, (t, t), 0)
    qry = lax.broadcasted_iota(jnp.int32, (t, t), 1)
    for w in range(Q_TILES_PER_STEP):
        _attn_query_tile(qt_ref, k_ref, vt_ref, o_ref, m_ref, acc_ref, s_ref, key, qry,
                         pl.program_id(2), w, t, heads)


def _attn_query_tile(qt_ref, k_ref, vt_ref, o_ref, m_ref, acc_ref, s_ref, key, qry, step, w, t,
                     heads):
    qi = Q_TILES_PER_STEP * step + w
    cols = slice(w * t, (w + 1) * t)
    m_ref[...] = jnp.full_like(m_ref, -jnp.inf)
    acc_ref[...] = jnp.zeros_like(acc_ref)

    def scores(tile, par):
        start = pl.multiple_of(tile * t, t)
        for i in range(heads):
            kf = k_ref[0, pl.ds(start, t), i * LANES:(i + 1) * LANES]
            s_ref[par, i] = jnp.dot(kf, qt_ref[0, i, :, cols],
                                    preferred_element_type=F32)

    def softmax_pv(tile, par, diagonal=False):
        start = pl.multiple_of(tile * t, t)
        for i in range(heads):
            vt = vt_ref[0, i, :, pl.ds(start, t)]
            s = s_ref[par, i]
            if diagonal:
                s = jnp.where(qry >= key, s, -jnp.inf)
            m_prev = m_ref[i]
            m_next = jnp.maximum(m_prev, jnp.max(s, axis=0, keepdims=True))
            alpha = jnp.exp2(m_prev - m_next)
            p = jnp.exp2(s - m_next).astype(BF16)
            pv = jnp.dot(vt, p, preferred_element_type=F32)
            acc_ref[i] = acc_ref[i] * alpha + pv
            m_ref[i] = m_next

    scores(0, 0)

    def tile_pair(i):
        scores(2 * i + 1, 1)
        softmax_pv(2 * i, 0)
        scores(2 * i + 2, 0)
        softmax_pv(2 * i + 1, 1)

    assert Q_TILES_PER_STEP == 4
    def body(i, carry):
        tile_pair(2 * i)
        tile_pair(2 * i + 1)
        return carry

    lax.fori_loop(0, step, body, 0)
    if w // 2:
        tile_pair(2 * step)

    if w % 2:
        scores(qi, 1)
        softmax_pv(qi - 1, 0)
        softmax_pv(qi, 1, diagonal=True)
    else:
        softmax_pv(qi, 0, diagonal=True)

    outs = [acc_ref[i, :HEAD_DIM, :] / acc_ref[i, HEAD_DIM:, :] for i in range(heads)]
    o_ref[0, cols, :] = jnp.concatenate(outs, axis=0).T.astype(o_ref.dtype)


def _attn_call(qt, k, vt):
    b, n_heads, _, s = qt.shape
    t = ATT_TILE
    heads = LANES // HEAD_DIM
    tq = Q_TILES_PER_STEP * t
    qspec = pl.BlockSpec((1, heads, LANES, tq), lambda i, p, j: (i, p, 0, j))
    kspec = pl.BlockSpec((1, s, heads * LANES), lambda i, p, j: (i, 0, p))
    vspec = pl.BlockSpec((1, heads, LANES, s), lambda i, p, j: (i, p, 0, 0))
    return pl.pallas_call(
        _attn_kernel,
        grid=(b, n_heads // heads, s // tq),
        in_specs=[qspec, kspec, vspec],
        out_specs=pl.BlockSpec((1, tq, LANES), lambda i, p, j: (i, j, p)),
        out_shape=jax.ShapeDtypeStruct((b, s, n_heads * HEAD_DIM), BF16),
        scratch_shapes=[pltpu.VMEM((heads, 1, t), F32),
                        pltpu.VMEM((heads, LANES, t), F32),
                        pltpu.VMEM((2, heads, t, t), F32)],
        compiler_params=pltpu.CompilerParams(
            dimension_semantics=("arbitrary", "arbitrary", "arbitrary"),
            vmem_limit_bytes=VMEM_LIMIT_BYTES),
        name="attn",
    )(qt, k, vt)


def _store_lane_tiles(ref, value):
    for c in range(ref.shape[0]):
        ref[c] = value[:, c * LANES:(c + 1) * LANES]


def _load_row_groups(ref, starts_strides):
    return jnp.concatenate(
        [jnp.concatenate([ref[c, pl.ds(start, SUBLANES, stride=stride), :]
                          for c in range(ref.shape[0])], axis=1)
         for start, stride in starts_strides], axis=0)


def _ffn_kernel(x_ref, att_ref, sg_ref, wo_ref, g2_ref, wup_ref, cw_ref, cb_ref, wdn_ref, g3_ref,
                o_ref, h1_ref, hnat_ref, hn_ref, hbuf_ref, act_ref, res_ref, carry_ref, *,
                final_norm):
    tm = x_ref.shape[1]
    n_chunks = wup_ref.shape[0]
    groups = tm // SUBLANES
    halo = (CONV_WIDTH - 1) * SUBLANES

    @pl.when(pl.program_id(1) == 0)
    def _():
        carry_ref[...] = jnp.zeros_like(carry_ref)

    mix = jnp.concatenate([att_ref[0], sg_ref[0]], axis=1)
    h1 = x_ref[0] + jnp.dot(mix, wo_ref[...], preferred_element_type=F32)
    h1_ref[...] = h1
    ms = jnp.mean(h1 * h1, axis=-1, keepdims=True)
    _store_lane_tiles(hnat_ref, (h1 * lax.rsqrt(ms + EPS)) * g2_ref[...])

    hn_ref[...] = _load_row_groups(hnat_ref, [(k, groups) for k in range(groups)]).astype(BF16)
    first_sublane = lax.broadcasted_iota(jnp.int32, (SUBLANES, 2 * FF_CHUNK), 0) == 0

    def up(c):
        hbuf = hbuf_ref.at[c % 2]
        hbuf[halo:, :] = jnp.dot(hn_ref[...], wup_ref[c], preferred_element_type=F32)
        for j in range(CONV_WIDTH - 1):
            prev_tile = carry_ref[c, j * SUBLANES:(j + 1) * SUBLANES, :]
            this_tile = hbuf[tm + j * SUBLANES:tm + (j + 1) * SUBLANES, :]
            hbuf[j * SUBLANES:(j + 1) * SUBLANES, :] = jnp.where(
                first_sublane, pltpu.roll(prev_tile, 1, axis=0), pltpu.roll(this_tile, 1, axis=0))
        carry_ref[c] = hbuf[tm:tm + halo, :]

    def gate(c):
        hbuf = hbuf_ref.at[c % 2]
        cw = cw_ref[c]
        y = cb_ref[c]
        for tap in range(CONV_WIDTH):
            lo = tap * SUBLANES
            y = y + hbuf[lo:lo + tm, :] * cw[tap:tap + 1, :]
        a = y[:, :FF_CHUNK]
        g = y[:, FF_CHUNK:]
        act_ref[:, c * FF_CHUNK:(c + 1) * FF_CHUNK] = (jax.nn.silu(g) * a).astype(BF16)

    up(0)
    for c in range(n_chunks):
        if c + 1 < n_chunks:
            up(c + 1)
        gate(c)

    _store_lane_tiles(res_ref, jnp.dot(act_ref[...], wdn_ref[...], preferred_element_type=F32))
    ffn = _load_row_groups(
        res_ref, [(SUBLANES * ((SUBLANES * k) % groups) + (SUBLANES * k) // groups, SUBLANES)
                  for k in range(groups)])
    h2 = h1_ref[...] + ffn
    if final_norm:
        ms2 = jnp.mean(h2 * h2, axis=-1, keepdims=True)
        h2 = (h2 * lax.rsqrt(ms2 + EPS)) * g3_ref[...]
    o_ref[0] = h2


def _ffn_call(x, att, sg, wo, g2, wup, cw, cb, wdn, g3, final_norm):
    b, s, d = x.shape
    tm = TOKEN_TILE
    n_chunks = wup.shape[0]
    d_ff = wdn.shape[0]
    halo = (CONV_WIDTH - 1) * SUBLANES
    tile = lambda w: pl.BlockSpec((1, tm, w), lambda i, j: (i, j, 0))
    return pl.pallas_call(
        functools.partial(_ffn_kernel, final_norm=final_norm),
        grid=(b, s // tm),
        in_specs=[tile(d), tile(att.shape[2]), tile(sg.shape[2]), _const_spec(wo.shape),
                  _const_spec(g2.shape), _const_spec(wup.shape), _const_spec(cw.shape),
                  _const_spec(cb.shape), _const_spec(wdn.shape), _const_spec(g3.shape)],
        out_specs=tile(d),
        out_shape=jax.ShapeDtypeStruct((b, s, d), F32),
        scratch_shapes=[pltpu.VMEM((tm, d), F32),
                        pltpu.VMEM((d // LANES, tm, LANES), F32),
                        pltpu.VMEM((tm, d), BF16),
                        pltpu.VMEM((2, tm + halo, 2 * FF_CHUNK), F32),
                        pltpu.VMEM((tm, d_ff), BF16),
                        pltpu.VMEM((d // LANES, tm, LANES), F32),
                        pltpu.VMEM((n_chunks, halo, 2 * FF_CHUNK), F32)],
        compiler_params=pltpu.CompilerParams(
            dimension_semantics=("arbitrary", "arbitrary"), vmem_limit_bytes=VMEM_LIMIT_BYTES),
        name="ffn",
    )(x, att, sg, wo, g2, wup, cw, cb, wdn, g3)


def kernel(x, norm_mix_g, w_in, b_forget, gmlp_norm_g, w_spatial, b_spatial, w_out, norm_ffn_g,
           w_up, conv_w, conv_b, w_down, norm_final_g):
    depth, d_model, _ = w_in.shape
    n_heads = b_forget.shape[1]
    att_w = n_heads * HEAD_DIM
    gm_w = gmlp_norm_g.shape[1]
    n_groups = w_spatial.shape[1]
    d_ff = w_down.shape[1]
    assert gm_w == n_groups * GROUP_DIM and w_spatial.shape[2] == CHUNK
    assert n_heads * AUG_W <= LANES and att_w % LANES == 0 and gm_w % MXU_DIM == 0
    assert d_ff % FF_CHUNK == 0 and x.shape[1] % TOKEN_TILE == 0 and x.shape[1] % PROJ_TILE == 0
    assert x.shape[1] % (Q_TILES_PER_STEP * ATT_TILE) == 0
    n_chunks = d_ff // FF_CHUNK
    scale = HEAD_DIM ** -0.5

    gi = jnp.arange(MXU_DIM) // GROUP_DIM
    gsum = jnp.where(gi[:, None] == gi[None, :], 1.0 / GROUP_DIM, 0.0).astype(BF16)

    h = x
    for layer in range(depth):
        w = w_in[layer]
        wq = w[:, :att_w] * scale
        wk = w[:, att_w:2 * att_w].astype(BF16)
        wqvt = jnp.concatenate([wq, w[:, 2 * att_w:3 * att_w]], axis=1).T.astype(BF16)
        wuv = w[:, 3 * att_w:3 * att_w + 2 * gm_w].astype(BF16)
        wgate = w[:, 3 * att_w + 2 * gm_w:]
        wg = jnp.zeros((d_model, LANES), F32).at[:, :n_heads * AUG_W].set(
            jnp.repeat(wgate, AUG_W, axis=1)).astype(BF16)
        bfr = jnp.zeros((1, LANES), F32).at[0, :n_heads * AUG_W].set(
            jnp.repeat(b_forget[layer], AUG_W))
        bs = jnp.repeat(b_spatial[layer].T, GROUP_DIM, axis=1)

        qt, k, vt, sg = _proj_call(
            h, norm_mix_g[layer][None, :], wk, wqvt, wuv, wg, bfr, gmlp_norm_g[layer][None, :],
            w_spatial[layer], bs, gsum)
        att = _attn_call(qt, k, vt)

        def chunked(m):
            a = m[..., :d_ff].reshape(m.shape[:-1] + (n_chunks, FF_CHUNK))
            g = m[..., d_ff:].reshape(m.shape[:-1] + (n_chunks, FF_CHUNK))
            return jnp.moveaxis(jnp.concatenate([a, g], axis=-1), -2, 0)
        wup = chunked(w_up[layer]).astype(BF16)
        cw = chunked(conv_w[layer])
        cb = chunked(conv_b[layer][None, :])
        h = _ffn_call(h, att, sg, w_out[layer].astype(BF16), norm_ffn_g[layer][None, :], wup, cw, cb,
                      w_down[layer].astype(BF16), norm_final_g[None, :], layer == depth - 1)
    return h
```

```python
import functools

import jax
import jax.numpy as jnp
from jax import lax
from jax.experimental import pallas as pl
from jax.experimental.pallas import tpu as pltpu

EPS = 1e-6
LOG2E = 1.4426950408889634
HEAD_DIM = 64
GROUP_DIM = 64
CHUNK = 128
CONV_WIDTH = 3

LANES = 128
SUBLANES = 8
MXU_DIM = 256
VMEM_LIMIT_BYTES = 56 * 1024 * 1024

AUG_W = 8
PROJ_TILE = 1024
TOKEN_TILE = 512
ATT_TILE = 512
Q_TILES_PER_STEP = 4
FF_CHUNK = 256

F32 = jnp.float32
BF16 = jnp.bfloat16


def _split3(x):
    hi = x.astype(BF16)
    r1 = x - hi.astype(F32)
    mid = r1.astype(BF16)
    lo = (r1 - mid.astype(F32)).astype(BF16)
    return hi, mid, lo


def _const_spec(shape):
    zeros = (0,) * len(shape)
    return pl.BlockSpec(shape, lambda *_: zeros, pipeline_mode=pl.Buffered(1))


def _proj_kernel(x_ref, g_ref, wk_ref, wqvt_ref, wuv_ref, wg_ref, bf_ref, gn_ref, ws_ref, bs_ref,
                 gsum_ref, qt_ref, k_ref, vt_ref, sg_ref, carry_ref, *, att_w, gm_w):
    tm = x_ref.shape[1]

    @pl.when(pl.program_id(1) == 0)
    def _():
        carry_ref[...] = jnp.zeros_like(carry_ref)

    x = x_ref[0]
    ms = jnp.mean(x * x, axis=-1, keepdims=True)
    xn = ((x * lax.rsqrt(ms + EPS)) * g_ref[...]).astype(BF16)

    n_heads = att_w // HEAD_DIM
    uv = jax.nn.gelu(jnp.dot(xn, wuv_ref[...], preferred_element_type=F32))
    z = jnp.dot(xn, wg_ref[...], preferred_element_type=F32) + bf_ref[...]
    log_f = -(jnp.maximum(-z, 0.0) + jnp.log1p(jnp.exp(-jnp.abs(z))))
    k = jnp.dot(xn, wk_ref[...], preferred_element_type=F32)
    qvt = lax.dot_general(wqvt_ref[...], xn, (((1,), (1,)), ((), ())), preferred_element_type=F32)
    qt = qvt[:att_w] * LOG2E
    vt = qvt[att_w:]
    for h in range(n_heads):
        qt_ref[0, h, :HEAD_DIM, :] = qt[h * HEAD_DIM:(h + 1) * HEAD_DIM].astype(BF16)
        vt_ref[0, h, :HEAD_DIM, :] = vt[h * HEAD_DIM:(h + 1) * HEAD_DIM].astype(BF16)
        vt_ref[0, h, HEAD_DIM:, :] = jnp.ones((LANES - HEAD_DIM, tm), BF16)

    row = lax.broadcasted_iota(jnp.int32, (CHUNK, CHUNK), 0)
    col = lax.broadcasted_iota(jnp.int32, (CHUNK, CHUNK), 1)
    causal = row >= col
    tri = jnp.where(causal, 1.0, 0.0).astype(BF16)
    offset = carry_ref[0:1, :]
    c_blocks = []
    for r in range(tm // CHUNK):
        blk = log_f[r * CHUNK:(r + 1) * CHUNK]
        s = offset
        for piece in _split3(blk):
            s = s + jnp.dot(tri, piece, preferred_element_type=F32)
        c_blocks.append(s)
        offset = s[CHUNK - 1:CHUNK, :]
    carry_ref[0:1, :] = offset
    c = jnp.concatenate(c_blocks, axis=0)

    u = uv[:, :gm_w]
    vg = uv[:, gm_w:]
    sq = (vg * vg).astype(BF16)
    gsum = gsum_ref[...]
    msg = jnp.concatenate(
        [jnp.dot(sq[:, s * MXU_DIM:(s + 1) * MXU_DIM], gsum, preferred_element_type=F32)
         for s in range(gm_w // MXU_DIM)], axis=1)
    vgn = (vg * lax.rsqrt(msg + EPS)) * gn_ref[...]

    groups_per_slab = MXU_DIM // GROUP_DIM
    lane_grp = lax.broadcasted_iota(jnp.int32, (CHUNK, MXU_DIM), 1) // GROUP_DIM
    w_masked = [jnp.where(causal, ws_ref[g], 0.0).astype(BF16) for g in range(ws_ref.shape[0])]
    w_slab = [jnp.concatenate(w_masked[s * groups_per_slab:(s + 1) * groups_per_slab], axis=1)
              for s in range(gm_w // MXU_DIM)]
    mixed_rows = []
    for r in range(tm // CHUNK):
        slabs = []
        for s in range(gm_w // MXU_DIM):
            vs = vgn[r * CHUNK:(r + 1) * CHUNK, s * MXU_DIM:(s + 1) * MXU_DIM]
            stacked = jnp.concatenate(
                [jnp.where(lane_grp == j, vs, 0.0).astype(BF16) for j in range(groups_per_slab)],
                axis=0)
            slabs.append(jnp.dot(w_slab[s], stacked, preferred_element_type=F32))
        mixed_rows.append(jnp.concatenate(slabs, axis=1) + bs_ref[...])
    mixed = jnp.concatenate(mixed_rows, axis=0)
    sg_ref[0] = (u * mixed).astype(BF16)

    c_hi, c_mid, c_lo = [piece.astype(F32) for piece in _split3(c * LOG2E)]
    slot = lax.broadcasted_iota(jnp.int32, (tm, LANES), 1) % AUG_W
    qa = jnp.where(slot == 0, c_hi, jnp.where(slot == 1, c_mid, jnp.where(slot == 2, c_lo,
         jnp.where(slot < 6, 1.0, 0.0))))
    ka = jnp.where(slot < 3, 1.0, jnp.where(slot == 3, -c_hi, jnp.where(slot == 4, -c_mid,
         jnp.where(slot == 5, -c_lo, 0.0))))
    qat = qa.T
    lane = lax.broadcasted_iota(jnp.int32, (tm, LANES), 1)
    pad_rows = 2 * SUBLANES - AUG_W
    for h in range(n_heads):
        decay_rows = jnp.concatenate(
            [qat[h * AUG_W:(h + 1) * AUG_W], jnp.zeros((pad_rows, tm), F32)], axis=0)
        qt_ref[0, h, HEAD_DIM:HEAD_DIM + 2 * SUBLANES, :] = decay_rows.astype(BF16)
        qt_ref[0, h, HEAD_DIM + 2 * SUBLANES:, :] = jnp.zeros(
            (LANES - HEAD_DIM - 2 * SUBLANES, tm), BF16)
        kh = k[:, (h // 2) * LANES:(h // 2 + 1) * LANES]
        if h % 2:
            kh = pltpu.roll(kh, HEAD_DIM, axis=1)
        kah = pltpu.roll(ka, HEAD_DIM - h * AUG_W, axis=1)
        k_ref[0, :, h * LANES:(h + 1) * LANES] = jnp.where(
            lane < HEAD_DIM, kh, jnp.where(lane < HEAD_DIM + AUG_W, kah, 0.0)).astype(BF16)


def _proj_call(x, g, wk, wqvt, wuv, wg, bfr, gn, ws, bs, gsum):
    b, s, d = x.shape
    tm = PROJ_TILE
    att_w = wk.shape[1]
    n_heads = att_w // HEAD_DIM
    gm_w = wuv.shape[1] // 2
    tile = lambda w: pl.BlockSpec((1, tm, w), lambda i, j: (i, j, 0))
    head_t = lambda rows: pl.BlockSpec((1, n_heads, rows, tm), lambda i, j: (i, 0, 0, j))
    out_shape = [jax.ShapeDtypeStruct((b, n_heads, LANES, s), BF16),
                 jax.ShapeDtypeStruct((b, s, n_heads * LANES), BF16),
                 jax.ShapeDtypeStruct((b, n_heads, LANES, s), BF16),
                 jax.ShapeDtypeStruct((b, s, gm_w), BF16)]
    return pl.pallas_call(
        functools.partial(_proj_kernel, att_w=att_w, gm_w=gm_w),
        grid=(b, s // tm),
        in_specs=[tile(d), _const_spec(g.shape), _const_spec(wk.shape), _const_spec(wqvt.shape),
                  _const_spec(wuv.shape), _const_spec(wg.shape), _const_spec(bfr.shape),
                  _const_spec(gn.shape), _const_spec(ws.shape), _const_spec(bs.shape),
                  _const_spec(gsum.shape)],
        out_specs=[head_t(LANES), tile(n_heads * LANES), head_t(LANES), tile(gm_w)],
        out_shape=out_shape,
        scratch_shapes=[pltpu.VMEM((SUBLANES, LANES), F32)],
        compiler_params=pltpu.CompilerParams(
            dimension_semantics=("arbitrary", "arbitrary"), vmem_limit_bytes=VMEM_LIMIT_BYTES),
        name="proj",
    )(x, g, wk, wqvt, wuv, wg, bfr, gn, ws, bs, gsum)


def _attn_kernel(qt_ref, k_ref, vt_ref, o_ref, m_ref, acc_ref, s_ref):
    t = s_ref.shape[2]
    heads = qt_ref.shape[1]
    key = lax.broadcasted_iota(jnp.int32, (t, t), 0)
    qry = lax.broadcasted_iota(jnp.int32, (t, t), 1)
    for w in range(Q_TILES_PER_STEP):
        _attn_query_tile(qt_ref, k_ref, vt_ref, o_ref, m_ref, acc_ref, s_ref, key, qry,
                         pl.program_id(2), w, t, heads)


def _attn_query_tile(qt_ref, k_ref, vt_ref, o_ref, m_ref, acc_ref, s_ref, key, qry, step, w, t,
                     heads):
    qi = Q_TILES_PER_STEP * step + w
    cols = slice(w * t, (w + 1) * t)
    m_ref[...] = jnp.full_like(m_ref, -jnp.inf)
    acc_ref[...] = jnp.zeros_like(acc_ref)

    def scores(tile, par):
        start = pl.multiple_of(tile * t, t)
        for i in range(heads):
            kf = k_ref[0, pl.ds(start, t), i * LANES:(i + 1) * LANES]
            s_ref[par, i] = jnp.dot(kf, qt_ref[0, i, :, cols],
                                    preferred_element_type=F32)

    def softmax_pv(tile, par, diagonal=False):
        start = pl.multiple_of(tile * t, t)
        for i in range(heads):
            vt = vt_ref[0, i, :, pl.ds(start, t)]
            s = s_ref[par, i]
            if diagonal:
                s = jnp.where(qry >= key, s, -jnp.inf)
            m_prev = m_ref[i]
            m_next = jnp.maximum(m_prev, jnp.max(s, axis=0, keepdims=True))
            alpha = jnp.exp2(m_prev - m_next)
            p = jnp.exp2(s - m_next).astype(BF16)
            pv = jnp.dot(vt, p, preferred_element_type=F32)
            acc_ref[i] = acc_ref[i] * alpha + pv
            m_ref[i] = m_next

    scores(0, 0)

    def tile_pair(i):
        scores(2 * i + 1, 1)
        softmax_pv(2 * i, 0)
        scores(2 * i + 2, 0)
        softmax_pv(2 * i + 1, 1)

    assert Q_TILES_PER_STEP == 4
    def body(i, carry):
        tile_pair(2 * i)
        tile_pair(2 * i + 1)
        return carry

    lax.fori_loop(0, step, body, 0)
    if w // 2:
        tile_pair(2 * step)

    if w % 2:
        scores(qi, 1)
        softmax_pv(qi - 1, 0)
        softmax_pv(qi, 1, diagonal=True)
    else:
        softmax_pv(qi, 0, diagonal=True)

    outs = [acc_ref[i, :HEAD_DIM, :] / acc_ref[i, HEAD_DIM:, :] for i in range(heads)]
    o_ref[0, cols, :] = jnp.concatenate(outs, axis=0).T.astype(o_ref.dtype)


def _attn_call(qt, k, vt):
    b, n_heads, _, s = qt.shape
    t = ATT_TILE
    heads = LANES // HEAD_DIM
    tq = Q_TILES_PER_STEP * t
    qspec = pl.BlockSpec((1, heads, LANES, tq), lambda i, p, j: (i, p, 0, j))
    kspec = pl.BlockSpec((1, s, heads * LANES), lambda i, p, j: (i, 0, p))
    vspec = pl.BlockSpec((1, heads, LANES, s), lambda i, p, j: (i, p, 0, 0))
    return pl.pallas_call(
        _attn_kernel,
        grid=(b, n_heads // heads, s // tq),
        in_specs=[qspec, kspec, vspec],
        out_specs=pl.BlockSpec((1, tq, LANES), lambda i, p, j: (i, j, p)),
        out_shape=jax.ShapeDtypeStruct((b, s, n_heads * HEAD_DIM), BF16),
        scratch_shapes=[pltpu.VMEM((heads, 1, t), F32),
                        pltpu.VMEM((heads, LANES, t), F32),
                        pltpu.VMEM((2, heads, t, t), F32)],
        compiler_params=pltpu.CompilerParams(
            dimension_semantics=("arbitrary", "arbitrary", "arbitrary"),
            vmem_limit_bytes=VMEM_LIMIT_BYTES),
        name="attn",
    )(qt, k, vt)


def _store_lane_tiles(ref, value):
    for c in range(ref.shape[0]):
        ref[c] = value[:, c * LANES:(c + 1) * LANES]


def _load_row_groups(ref, starts_strides):
    return jnp.concatenate(
        [jnp.concatenate([ref[c, pl.ds(start, SUBLANES, stride=stride), :]
                          for c in range(ref.shape[0])], axis=1)
         for start, stride in starts_strides], axis=0)


def _ffn_kernel(x_ref, att_ref, sg_ref, wo_ref, g2_ref, wup_ref, cw_ref, cb_ref, wdn_ref, g3_ref,
                o_ref, h1_ref, hnat_ref, hn_ref, hbuf_ref, act_ref, res_ref, carry_ref, *,
                final_norm):
    tm = x_ref.shape[1]
    n_chunks = wup_ref.shape[0]
    groups = tm // SUBLANES
    halo = (CONV_WIDTH - 1) * SUBLANES

    @pl.when(pl.program_id(1) == 0)
    def _():
        carry_ref[...] = jnp.zeros_like(carry_ref)

    mix = jnp.concatenate([att_ref[0], sg_ref[0]], axis=1)
    h1 = x_ref[0] + jnp.dot(mix, wo_ref[...], preferred_element_type=F32)
    h1_ref[...] = h1
    ms = jnp.mean(h1 * h1, axis=-1, keepdims=True)
    _store_lane_tiles(hnat_ref, (h1 * lax.rsqrt(ms + EPS)) * g2_ref[...])

    hn_ref[...] = _load_row_groups(hnat_ref, [(k, groups) for k in range(groups)]).astype(BF16)
    first_sublane = lax.broadcasted_iota(jnp.int32, (SUBLANES, 2 * FF_CHUNK), 0) == 0

    def up(c):
        hbuf = hbuf_ref.at[c % 2]
        hbuf[halo:, :] = jnp.dot(hn_ref[...], wup_ref[c], preferred_element_type=F32)
        for j in range(CONV_WIDTH - 1):
            prev_tile = carry_ref[c, j * SUBLANES:(j + 1) * SUBLANES, :]
            this_tile = hbuf[tm + j * SUBLANES:tm + (j + 1) * SUBLANES, :]
            hbuf[j * SUBLANES:(j + 1) * SUBLANES, :] = jnp.where(
                first_sublane, pltpu.roll(prev_tile, 1, axis=0), pltpu.roll(this_tile, 1, axis=0))
        carry_ref[c] = hbuf[tm:tm + halo, :]

    def gate(c):
        hbuf = hbuf_ref.at[c % 2]
        cw = cw_ref[c]
        y = cb_ref[c]
        for tap in range(CONV_WIDTH):
            lo = tap * SUBLANES
            y = y + hbuf[lo:lo + tm, :] * cw[tap:tap + 1, :]
        a = y[:, :FF_CHUNK]
        g = y[:, FF_CHUNK:]
        act_ref[:, c * FF_CHUNK:(c + 1) * FF_CHUNK] = (jax.nn.silu(g) * a).astype(BF16)

    up(0)
    for c in range(n_chunks):
        if c + 1 < n_chunks:
            up(c + 1)
        gate(c)

    _store_lane_tiles(res_ref, jnp.dot(act_ref[...], wdn_ref[...], preferred_element_type=F32))
    ffn = _load_row_groups(
        res_ref, [(SUBLANES * ((SUBLANES * k) % groups) + (SUBLANES * k) // groups, SUBLANES)
                  for k in range(groups)])
    h2 = h1_ref[...] + ffn
    if final_norm:
        ms2 = jnp.mean(h2 * h2, axis=-1, keepdims=True)
        h2 = (h2 * lax.rsqrt(ms2 + EPS)) * g3_ref[...]
    o_ref[0] = h2


def _ffn_call(x, att, sg, wo, g2, wup, cw, cb, wdn, g3, final_norm):
    b, s, d = x.shape
    tm = TOKEN_TILE
    n_chunks = wup.shape[0]
    d_ff = wdn.shape[0]
    halo = (CONV_WIDTH - 1) * SUBLANES
    tile = lambda w: pl.BlockSpec((1, tm, w), lambda i, j: (i, j, 0))
    return pl.pallas_call(
        functools.partial(_ffn_kernel, final_norm=final_norm),
        grid=(b, s // tm),
        in_specs=[tile(d), tile(att.shape[2]), tile(sg.shape[2]), _const_spec(wo.shape),
                  _const_spec(g2.shape), _const_spec(wup.shape), _const_spec(cw.shape),
                  _const_spec(cb.shape), _const_spec(wdn.shape), _const_spec(g3.shape)],
        out_specs=tile(d),
        out_shape=jax.ShapeDtypeStruct((b, s, d), F32),
        scratch_shapes=[pltpu.VMEM((tm, d), F32),
                        pltpu.VMEM((d // LANES, tm, LANES), F32),
                        pltpu.VMEM((tm, d), BF16),
                        pltpu.VMEM((2, tm + halo, 2 * FF_CHUNK), F32),
                        pltpu.VMEM((tm, d_ff), BF16),
                        pltpu.VMEM((d // LANES, tm, LANES), F32),
                        pltpu.VMEM((n_chunks, halo, 2 * FF_CHUNK), F32)],
        compiler_params=pltpu.CompilerParams(
            dimension_semantics=("arbitrary", "arbitrary"), vmem_limit_bytes=VMEM_LIMIT_BYTES),
        name="ffn",
    )(x, att, sg, wo, g2, wup, cw, cb, wdn, g3)


def kernel(x, norm_mix_g, w_in, b_forget, gmlp_norm_g, w_spatial, b_spatial, w_out, norm_ffn_g,
           w_up, conv_w, conv_b, w_down, norm_final_g):
    depth, d_model, _ = w_in.shape
    n_heads = b_forget.shape[1]
    att_w = n_heads * HEAD_DIM
    gm_w = gmlp_norm_g.shape[1]
    n_groups = w_spatial.shape[1]
    d_ff = w_down.shape[1]
    assert gm_w == n_groups * GROUP_DIM and w_spatial.shape[2] == CHUNK
    assert n_heads * AUG_W <= LANES and att_w % LANES == 0 and gm_w % MXU_DIM == 0
    assert d_ff % FF_CHUNK == 0 and x.shape[1] % TOKEN_TILE == 0 and x.shape[1] % PROJ_TILE == 0
    assert x.shape[1] % (Q_TILES_PER_STEP * ATT_TILE) == 0
    n_chunks = d_ff // FF_CHUNK
    scale = HEAD_DIM ** -0.5

    gi = jnp.arange(MXU_DIM) // GROUP_DIM
    gsum = jnp.where(gi[:, None] == gi[None, :], 1.0 / GROUP_DIM, 0.0).astype(BF16)

    h = x
    for layer in range(depth):
        w = w_in[layer]
        wq = w[:, :att_w] * scale
        wk = w[:, att_w:2 * att_w].astype(BF16)
        wqvt = jnp.concatenate([wq, w[:, 2 * att_w:3 * att_w]], axis=1).T.astype(BF16)
        wuv = w[:, 3 * att_w:3 * att_w + 2 * gm_w].astype(BF16)
        wgate = w[:, 3 * att_w + 2 * gm_w:]
        wg = jnp.zeros((d_model, LANES), F32).at[:, :n_heads * AUG_W].set(
            jnp.repeat(wgate, AUG_W, axis=1)).astype(BF16)
        bfr = jnp.zeros((1, LANES), F32).at[0, :n_heads * AUG_W].set(
            jnp.repeat(b_forget[layer], AUG_W))
        bs = jnp.repeat(b_spatial[layer].T, GROUP_DIM, axis=1)

        qt, k, vt, sg = _proj_call(
            h, norm_mix_g[layer][None, :], wk, wqvt, wuv, wg, bfr, gmlp_norm_g[layer][None, :],
            w_spatial[layer], bs, gsum)
        att = _attn_call(qt, k, vt)

        def chunked(m):
            a = m[..., :d_ff].reshape(m.shape[:-1] + (n_chunks, FF_CHUNK))
            g = m[..., d_ff:].reshape(m.shape[:-1] + (n_chunks, FF_CHUNK))
            return jnp.moveaxis(jnp.concatenate([a, g], axis=-1), -2, 0)
        wup = chunked(w_up[layer]).astype(BF16)
        cw = chunked(conv_w[layer])
        cb = chunked(conv_b[layer][None, :])
        h = _ffn_call(h, att, sg, w_out[layer].astype(BF16), norm_ffn_g[layer][None, :], wup, cw, cb,
                      w_down[layer].astype(BF16), norm_final_g[None, :], layer == depth - 1)
    return h
```

```python
import functools

import jax
import jax.numpy as jnp
from jax import lax
from jax.experimental import pallas as pl
from jax.experimental.pallas import tpu as pltpu

EPS = 1e-6
LOG2E = 1.4426950408889634
HEAD_DIM = 64
GROUP_DIM = 64
CHUNK = 128
CONV_WIDTH = 3

LANES = 128
SUBLANES = 8
MXU_DIM = 256
VMEM_LIMIT_BYTES = 56 * 1024 * 1024

AUG_W = 8
PROJ_TILE = 1024
TOKEN_TILE = 512
ATT_TILE = 512
Q_TILES_PER_STEP = 8
FF_CHUNK = 256

F32 = jnp.float32
BF16 = jnp.bfloat16


def _split3(x):
    hi = x.astype(BF16)
    r1 = x - hi.astype(F32)
    mid = r1.astype(BF16)
    lo = (r1 - mid.astype(F32)).astype(BF16)
    return hi, mid, lo


def _const_spec(shape):
    zeros = (0,) * len(shape)
    return pl.BlockSpec(shape, lambda *_: zeros, pipeline_mode=pl.Buffered(1))


def _proj_kernel(x_ref, g_ref, wk_ref, wqvt_ref, wuv_ref, wg_ref, bf_ref, gn_ref, ws_ref, bs_ref,
                 gsum_ref, qt_ref, k_ref, vt_ref, sg_ref, carry_ref, *, att_w, gm_w):
    tm = x_ref.shape[1]

    @pl.when(pl.program_id(1) == 0)
    def _():
        carry_ref[...] = jnp.zeros_like(carry_ref)

    x = x_ref[0]
    ms = jnp.mean(x * x, axis=-1, keepdims=True)
    xn = ((x * lax.rsqrt(ms + EPS)) * g_ref[...]).astype(BF16)

    n_heads = att_w // HEAD_DIM
    uv = jax.nn.gelu(jnp.dot(xn, wuv_ref[...], preferred_element_type=F32))
    z = jnp.dot(xn, wg_ref[...], preferred_element_type=F32) + bf_ref[...]
    log_f = -(jnp.maximum(-z, 0.0) + jnp.log1p(jnp.exp(-jnp.abs(z))))
    k = jnp.dot(xn, wk_ref[...], preferred_element_type=F32)
    qvt = lax.dot_general(wqvt_ref[...], xn, (((1,), (1,)), ((), ())), preferred_element_type=F32)
    qt = qvt[:att_w] * LOG2E
    vt = qvt[att_w:]
    for h in range(n_heads):
        qt_ref[0, h, :HEAD_DIM, :] = qt[h * HEAD_DIM:(h + 1) * HEAD_DIM].astype(BF16)
        vt_ref[0, h, :HEAD_DIM, :] = vt[h * HEAD_DIM:(h + 1) * HEAD_DIM].astype(BF16)
        vt_ref[0, h, HEAD_DIM:, :] = jnp.ones((LANES - HEAD_DIM, tm), BF16)

    row = lax.broadcasted_iota(jnp.int32, (CHUNK, CHUNK), 0)
    col = lax.broadcasted_iota(jnp.int32, (CHUNK, CHUNK), 1)
    causal = row >= col
    tri = jnp.where(causal, 1.0, 0.0).astype(BF16)
    offset = carry_ref[0:1, :]
    c_blocks = []
    for r in range(tm // CHUNK):
        blk = log_f[r * CHUNK:(r + 1) * CHUNK]
        s = offset
        for piece in _split3(blk):
            s = s + jnp.dot(tri, piece, preferred_element_type=F32)
        c_blocks.append(s)
        offset = s[CHUNK - 1:CHUNK, :]
    carry_ref[0:1, :] = offset
    c = jnp.concatenate(c_blocks, axis=0)

    u = uv[:, :gm_w]
    vg = uv[:, gm_w:]
    sq = (vg * vg).astype(BF16)
    gsum = gsum_ref[...]
    msg = jnp.concatenate(
        [jnp.dot(sq[:, s * MXU_DIM:(s + 1) * MXU_DIM], gsum, preferred_element_type=F32)
         for s in range(gm_w // MXU_DIM)], axis=1)
    vgn = (vg * lax.rsqrt(msg + EPS)) * gn_ref[...]

    groups_per_slab = MXU_DIM // GROUP_DIM
    lane_grp = lax.broadcasted_iota(jnp.int32, (CHUNK, MXU_DIM), 1) // GROUP_DIM
    w_masked = [jnp.where(causal, ws_ref[g], 0.0).astype(BF16) for g in range(ws_ref.shape[0])]
    w_slab = [jnp.concatenate(w_masked[s * groups_per_slab:(s + 1) * groups_per_slab], axis=1)
              for s in range(gm_w // MXU_DIM)]
    mixed_rows = []
    for r in range(tm // CHUNK):
        slabs = []
        for s in range(gm_w // MXU_DIM):
            vs = vgn[r * CHUNK:(r + 1) * CHUNK, s * MXU_DIM:(s + 1) * MXU_DIM]
            stacked = jnp.concatenate(
                [jnp.where(lane_grp == j, vs, 0.0).astype(BF16) for j in range(groups_per_slab)],
                axis=0)
            slabs.append(jnp.dot(w_slab[s], stacked, preferred_element_type=F32))
        mixed_rows.append(jnp.concatenate(slabs, axis=1) + bs_ref[...])
    mixed = jnp.concatenate(mixed_rows, axis=0)
    sg_ref[0] = (u * mixed).astype(BF16)

    pieces = [piece.astype(F32) for piece in _split3(c * LOG2E)]
    n_pieces = len(pieces)
    assert 2 * n_pieces <= AUG_W
    slot = lax.broadcasted_iota(jnp.int32, (tm, LANES), 1) % AUG_W
    qa = jnp.where(slot < 2 * n_pieces, 1.0, 0.0)
    ka = qa
    for j, piece in enumerate(pieces):
        qa = jnp.where(slot == j, piece, qa)
        ka = jnp.where(slot == n_pieces + j, -piece, ka)
    qat = qa.T
    lane = lax.broadcasted_iota(jnp.int32, (tm, LANES), 1)
    pad_rows = 2 * SUBLANES - AUG_W
    for h in range(n_heads):
        decay_rows = jnp.concatenate(
            [qat[h * AUG_W:(h + 1) * AUG_W], jnp.zeros((pad_rows, tm), F32)], axis=0)
        qt_ref[0, h, HEAD_DIM:HEAD_DIM + 2 * SUBLANES, :] = decay_rows.astype(BF16)
        qt_ref[0, h, HEAD_DIM + 2 * SUBLANES:, :] = jnp.zeros(
            (LANES - HEAD_DIM - 2 * SUBLANES, tm), BF16)
        kh = k[:, (h // 2) * LANES:(h // 2 + 1) * LANES]
        if h % 2:
            kh = pltpu.roll(kh, HEAD_DIM, axis=1)
        kah = pltpu.roll(ka, HEAD_DIM - h * AUG_W, axis=1)
        k_ref[0, :, h * LANES:(h + 1) * LANES] = jnp.where(
            lane < HEAD_DIM, kh, jnp.where(lane < HEAD_DIM + AUG_W, kah, 0.0)).astype(BF16)


def _proj_call(x, g, wk, wqvt, wuv, wg, bfr, gn, ws, bs, gsum):
    b, s, d = x.shape
    tm = PROJ_TILE
    att_w = wk.shape[1]
    n_heads = att_w // HEAD_DIM
    gm_w = wuv.shape[1] // 2
    tile = lambda w: pl.BlockSpec((1, tm, w), lambda i, j: (i, j, 0))
    head_t = lambda rows: pl.BlockSpec((1, n_heads, rows, tm), lambda i, j: (i, 0, 0, j))
    out_shape = [jax.ShapeDtypeStruct((b, n_heads, LANES, s), BF16),
                 jax.ShapeDtypeStruct((b, s, n_heads * LANES), BF16),
                 jax.ShapeDtypeStruct((b, n_heads, LANES, s), BF16),
                 jax.ShapeDtypeStruct((b, s, gm_w), BF16)]
    return pl.pallas_call(
        functools.partial(_proj_kernel, att_w=att_w, gm_w=gm_w),
        grid=(b, s // tm),
        in_specs=[tile(d), _const_spec(g.shape), _const_spec(wk.shape), _const_spec(wqvt.shape),
                  _const_spec(wuv.shape), _const_spec(wg.shape), _const_spec(bfr.shape),
                  _const_spec(gn.shape), _const_spec(ws.shape), _const_spec(bs.shape),
                  _const_spec(gsum.shape)],
        out_specs=[head_t(LANES), tile(n_heads * LANES), head_t(LANES), tile(gm_w)],
        out_shape=out_shape,
        scratch_shapes=[pltpu.VMEM((SUBLANES, LANES), F32)],
        compiler_params=pltpu.CompilerParams(
            dimension_semantics=("arbitrary", "arbitrary"), vmem_limit_bytes=VMEM_LIMIT_BYTES),
        name="proj",
    )(x, g, wk, wqvt, wuv, wg, bfr, gn, ws, bs, gsum)


def _attn_kernel(qt_ref, k_ref, vt_ref, o_ref, m_ref, acc_ref, s_ref):
    t = s_ref.shape[2]
    heads = qt_ref.shape[1]
    key = lax.broadcasted_iota(jnp.int32, (t, t), 0)
    qry = lax.broadcasted_iota(jnp.int32, (t, t), 1)
    for w in range(Q_TILES_PER_STEP):
        _attn_query_tile(qt_ref, k_ref, vt_ref, o_ref, m_ref, acc_ref, s_ref, key, qry,
                         pl.program_id(2), w, t, heads)


def _attn_query_tile(qt_ref, k_ref, vt_ref, o_ref, m_ref, acc_ref, s_ref, key, qry, step, w, t,
                     heads):
    qi = Q_TILES_PER_STEP * step + w
    cols = slice(w * t, (w + 1) * t)
    m_ref[...] = jnp.full_like(m_ref, -jnp.inf)
    acc_ref[...] = jnp.zeros_like(acc_ref)

    def scores(tile, par):
        start = pl.multiple_of(tile * t, t)
        for i in range(heads):
            kf = k_ref[0, pl.ds(start, t), i * LANES:(i + 1) * LANES]
            s_ref[par, i] = jnp.dot(kf, qt_ref[0, i, :, cols],
                                    preferred_element_type=F32)

    def softmax_pv(tile, par, diagonal=False):
        start = pl.multiple_of(tile * t, t)
        for i in range(heads):
            vt = vt_ref[0, i, :, pl.ds(start, t)]
            s = s_ref[par, i]
            if diagonal:
                s = jnp.where(qry >= key, s, -jnp.inf)
            m_prev = m_ref[i]
            m_next = jnp.maximum(m_prev, jnp.max(s, axis=0, keepdims=True))
            alpha = jnp.exp2(m_prev - m_next)
            p = jnp.exp2(s - m_next).astype(BF16)
            pv = jnp.dot(vt, p, preferred_element_type=F32)
            acc_ref[i] = acc_ref[i] * alpha + pv
            m_ref[i] = m_next

    scores(0, 0)

    def tile_pair(i):
        scores(2 * i + 1, 1)
        softmax_pv(2 * i, 0)
        scores(2 * i + 2, 0)
        softmax_pv(2 * i + 1, 1)

    def body(i, carry):
        tile_pair(2 * i)
        tile_pair(2 * i + 1)
        return carry

    trips = (Q_TILES_PER_STEP // 4) * step + (w // 2) // 2
    lax.fori_loop(0, trips, body, 0)
    if (w // 2) % 2:
        tile_pair(2 * trips)

    if w % 2:
        scores(qi, 1)
        softmax_pv(qi - 1, 0)
        softmax_pv(qi, 1, diagonal=True)
    else:
        softmax_pv(qi, 0, diagonal=True)

    outs = [acc_ref[i, :HEAD_DIM, :] / acc_ref[i, HEAD_DIM:, :] for i in range(heads)]
    o_ref[0, cols, :] = jnp.concatenate(outs, axis=0).T.astype(o_ref.dtype)


def _attn_call(qt, k, vt):
    b, n_heads, _, s = qt.shape
    t = ATT_TILE
    heads = LANES // HEAD_DIM
    tq = Q_TILES_PER_STEP * t
    qspec = pl.BlockSpec((1, heads, LANES, tq), lambda i, p, j: (i, p, 0, j))
    kspec = pl.BlockSpec((1, s, heads * LANES), lambda i, p, j: (i, 0, p))
    vspec = pl.BlockSpec((1, heads, LANES, s), lambda i, p, j: (i, p, 0, 0))
    return pl.pallas_call(
        _attn_kernel,
        grid=(b, n_heads // heads, s // tq),
        in_specs=[qspec, kspec, vspec],
        out_specs=pl.BlockSpec((1, tq, LANES), lambda i, p, j: (i, j, p)),
        out_shape=jax.ShapeDtypeStruct((b, s, n_heads * HEAD_DIM), BF16),
        scratch_shapes=[pltpu.VMEM((heads, 1, t), F32),
                        pltpu.VMEM((heads, LANES, t), F32),
                        pltpu.VMEM((2, heads, t, t), F32)],
        compiler_params=pltpu.CompilerParams(
            dimension_semantics=("arbitrary", "arbitrary", "arbitrary"),
            vmem_limit_bytes=VMEM_LIMIT_BYTES),
        name="attn",
    )(qt, k, vt)


def _store_lane_tiles(ref, value):
    for c in range(ref.shape[0]):
        ref[c] = value[:, c * LANES:(c + 1) * LANES]


def _load_row_groups(ref, starts_strides):
    return jnp.concatenate(
        [jnp.concatenate([ref[c, pl.ds(start, SUBLANES, stride=stride), :]
                          for c in range(ref.shape[0])], axis=1)
         for start, stride in starts_strides], axis=0)


def _ffn_kernel(x_ref, att_ref, sg_ref, wo_ref, g2_ref, wup_ref, cw_ref, cb_ref, wdn_ref, g3_ref,
                o_ref, h1_ref, hnat_ref, hn_ref, hbuf_ref, act_ref, res_ref, carry_ref, *,
                final_norm):
    tm = x_ref.shape[1]
    n_chunks = wup_ref.shape[0]
    groups = tm // SUBLANES
    halo = (CONV_WIDTH - 1) * SUBLANES

    @pl.when(pl.program_id(1) == 0)
    def _():
        carry_ref[...] = jnp.zeros_like(carry_ref)

    mix = jnp.concatenate([att_ref[0], sg_ref[0]], axis=1)
    h1 = x_ref[0] + jnp.dot(mix, wo_ref[...], preferred_element_type=F32)
    h1_ref[...] = h1
    ms = jnp.mean(h1 * h1, axis=-1, keepdims=True)
    _store_lane_tiles(hnat_ref, (h1 * lax.rsqrt(ms + EPS)) * g2_ref[...])

    hn_ref[...] = _load_row_groups(hnat_ref, [(k, groups) for k in range(groups)]).astype(BF16)
    first_sublane = lax.broadcasted_iota(jnp.int32, (SUBLANES, 2 * FF_CHUNK), 0) == 0

    def up(c):
        hbuf = hbuf_ref.at[c % 2]
        hbuf[halo:, :] = jnp.dot(hn_ref[...], wup_ref[c], preferred_element_type=F32)
        for j in range(CONV_WIDTH - 1):
            prev_tile = carry_ref[c, j * SUBLANES:(j + 1) * SUBLANES, :]
            this_tile = hbuf[tm + j * SUBLANES:tm + (j + 1) * SUBLANES, :]
            hbuf[j * SUBLANES:(j + 1) * SUBLANES, :] = jnp.where(
                first_sublane, pltpu.roll(prev_tile, 1, axis=0), pltpu.roll(this_tile, 1, axis=0))
        carry_ref[c] = hbuf[tm:tm + halo, :]

    def gate(c):
        hbuf = hbuf_ref.at[c % 2]
        cw = cw_ref[c]
        y = cb_ref[c]
        for tap in range(CONV_WIDTH):
            lo = tap * SUBLANES
            y = y + hbuf[lo:lo + tm, :] * cw[tap:tap + 1, :]
        a = y[:, :FF_CHUNK]
        g = y[:, FF_CHUNK:]
        act_ref[:, c * FF_CHUNK:(c + 1) * FF_CHUNK] = (jax.nn.silu(g) * a).astype(BF16)

    up(0)
    for c in range(n_chunks):
        if c + 1 < n_chunks:
            up(c + 1)
        gate(c)

    _store_lane_tiles(res_ref, jnp.dot(act_ref[...], wdn_ref[...], preferred_element_type=F32))
    ffn = _load_row_groups(
        res_ref, [(SUBLANES * ((SUBLANES * k) % groups) + (SUBLANES * k) // groups, SUBLANES)
                  for k in range(groups)])
    h2 = h1_ref[...] + ffn
    if final_norm:
        ms2 = jnp.mean(h2 * h2, axis=-1, keepdims=True)
        h2 = (h2 * lax.rsqrt(ms2 + EPS)) * g3_ref[...]
    o_ref[0] = h2


def _ffn_call(x, att, sg, wo, g2, wup, cw, cb, wdn, g3, final_norm):
    b, s, d = x.shape
    tm = TOKEN_TILE
    n_chunks = wup.shape[0]
    d_ff = wdn.shape[0]
    halo = (CONV_WIDTH - 1) * SUBLANES
    tile = lambda w: pl.BlockSpec((1, tm, w), lambda i, j: (i, j, 0))
    return pl.pallas_call(
        functools.partial(_ffn_kernel, final_norm=final_norm),
        grid=(b, s // tm),
        in_specs=[tile(d), tile(att.shape[2]), tile(sg.shape[2]), _const_spec(wo.shape),
                  _const_spec(g2.shape), _const_spec(wup.shape), _const_spec(cw.shape),
                  _const_spec(cb.shape), _const_spec(wdn.shape), _const_spec(g3.shape)],
        out_specs=tile(d),
        out_shape=jax.ShapeDtypeStruct((b, s, d), F32),
        scratch_shapes=[pltpu.VMEM((tm, d), F32),
                        pltpu.VMEM((d // LANES, tm, LANES), F32),
                        pltpu.VMEM((tm, d), BF16),
                        pltpu.VMEM((2, tm + halo, 2 * FF_CHUNK), F32),
                        pltpu.VMEM((tm, d_ff), BF16),
                        pltpu.VMEM((d // LANES, tm, LANES), F32),
                        pltpu.VMEM((n_chunks, halo, 2 * FF_CHUNK), F32)],
        compiler_params=pltpu.CompilerParams(
            dimension_semantics=("arbitrary", "arbitrary"), vmem_limit_bytes=VMEM_LIMIT_BYTES),
        name="ffn",
    )(x, att, sg, wo, g2, wup, cw, cb, wdn, g3)


def kernel(x, norm_mix_g, w_in, b_forget, gmlp_norm_g, w_spatial, b_spatial, w_out, norm_ffn_g,
           w_up, conv_w, conv_b, w_down, norm_final_g):
    depth, d_model, _ = w_in.shape
    n_heads = b_forget.shape[1]
    att_w = n_heads * HEAD_DIM
    gm_w = gmlp_norm_g.shape[1]
    n_groups = w_spatial.shape[1]
    d_ff = w_down.shape[1]
    assert gm_w == n_groups * GROUP_DIM and w_spatial.shape[2] == CHUNK
    assert n_heads * AUG_W <= LANES and att_w % LANES == 0 and gm_w % MXU_DIM == 0
    assert d_ff % FF_CHUNK == 0 and x.shape[1] % TOKEN_TILE == 0 and x.shape[1] % PROJ_TILE == 0
    assert Q_TILES_PER_STEP % 4 == 0 and x.shape[1] % (Q_TILES_PER_STEP * ATT_TILE) == 0
    n_chunks = d_ff // FF_CHUNK
    scale = HEAD_DIM ** -0.5

    gi = jnp.arange(MXU_DIM) // GROUP_DIM
    gsum = jnp.where(gi[:, None] == gi[None, :], 1.0 / GROUP_DIM, 0.0).astype(BF16)

    h = x
    for layer in range(depth):
        w = w_in[layer]
        wq = w[:, :att_w] * scale
        wk = w[:, att_w:2 * att_w].astype(BF16)
        wqvt = jnp.concatenate([wq, w[:, 2 * att_w:3 * att_w]], axis=1).T.astype(BF16)
        wuv = w[:, 3 * att_w:3 * att_w + 2 * gm_w].astype(BF16)
        wgate = w[:, 3 * att_w + 2 * gm_w:]
        wg = jnp.zeros((d_model, LANES), F32).at[:, :n_heads * AUG_W].set(
            jnp.repeat(wgate, AUG_W, axis=1)).astype(BF16)
        bfr = jnp.zeros((1, LANES), F32).at[0, :n_heads * AUG_W].set(
            jnp.repeat(b_forget[layer], AUG_W))
        bs = jnp.repeat(b_spatial[layer].T, GROUP_DIM, axis=1)

        qt, k, vt, sg = _proj_call(
            h, norm_mix_g[layer][None, :], wk, wqvt, wuv, wg, bfr, gmlp_norm_g[layer][None, :],
            w_spatial[layer], bs, gsum)
        att = _attn_call(qt, k, vt)

        def chunked(m):
            halves = m.reshape(m.shape[:-1] + (2, n_chunks, FF_CHUNK))
            return jnp.moveaxis(halves, -2, 0).reshape((n_chunks,) + m.shape[:-1] + (2 * FF_CHUNK,))
        wup = chunked(w_up[layer].astype(BF16))
        cw = chunked(conv_w[layer])
        cb = chunked(conv_b[layer][None, :])
        h = _ffn_call(h, att, sg, w_out[layer].astype(BF16), norm_ffn_g[layer][None, :], wup, cw, cb,
                      w_down[layer].astype(BF16), norm_final_g[None, :], layer == depth - 1)
    return h
```

```python
import functools

import jax
import jax.numpy as jnp
from jax import lax
from jax.experimental import pallas as pl
from jax.experimental.pallas import tpu as pltpu

EPS = 1e-6
LOG2E = 1.4426950408889634
HEAD_DIM = 64
GROUP_DIM = 64
CHUNK = 128
CONV_WIDTH = 3

LANES = 128
SUBLANES = 8
MXU_DIM = 256
VMEM_LIMIT_BYTES = 56 * 1024 * 1024

AUG_W = 8
PROJ_TILE = 1024
TOKEN_TILE = 512
ATT_TILE = 512
Q_TILES_PER_STEP = 4
FF_CHUNK = 256

F32 = jnp.float32
BF16 = jnp.bfloat16


def _split3(x):
    hi = x.astype(BF16)
    r1 = x - hi.astype(F32)
    mid = r1.astype(BF16)
    lo = (r1 - mid.astype(F32)).astype(BF16)
    return hi, mid, lo


def _const_spec(shape):
    zeros = (0,) * len(shape)
    return pl.BlockSpec(shape, lambda *_: zeros, pipeline_mode=pl.Buffered(1))


def _proj_kernel(x_ref, g_ref, wk_ref, wqvt_ref, wuv_ref, wg_ref, bf_ref, gn_ref, ws_ref, bs_ref,
                 gsum_ref, qt_ref, k_ref, vt_ref, sg_ref, carry_ref, *, att_w, gm_w):
    tm = x_ref.shape[1]

    @pl.when(pl.program_id(1) == 0)
    def _():
        carry_ref[...] = jnp.zeros_like(carry_ref)

    x = x_ref[0]
    ms = jnp.mean(x * x, axis=-1, keepdims=True)
    xn = ((x * lax.rsqrt(ms + EPS)) * g_ref[...]).astype(BF16)

    n_heads = att_w // HEAD_DIM
    uv = jax.nn.gelu(jnp.dot(xn, wuv_ref[...], preferred_element_type=F32))
    z = jnp.dot(xn, wg_ref[...], preferred_element_type=F32) + bf_ref[...]
    log_f = -(jnp.maximum(-z, 0.0) + jnp.log1p(jnp.exp(-jnp.abs(z))))
    k = jnp.dot(xn, wk_ref[...], preferred_element_type=F32)
    qvt = lax.dot_general(wqvt_ref[...], xn, (((1,), (1,)), ((), ())), preferred_element_type=F32)
    qt = qvt[:att_w] * LOG2E
    vt = qvt[att_w:]
    for h in range(n_heads):
        qt_ref[0, h, :HEAD_DIM, :] = qt[h * HEAD_DIM:(h + 1) * HEAD_DIM].astype(BF16)
        vt_ref[0, h, :HEAD_DIM, :] = vt[h * HEAD_DIM:(h + 1) * HEAD_DIM].astype(BF16)
        vt_ref[0, h, HEAD_DIM:, :] = jnp.ones((LANES - HEAD_DIM, tm), BF16)

    row = lax.broadcasted_iota(jnp.int32, (CHUNK, CHUNK), 0)
    col = lax.broadcasted_iota(jnp.int32, (CHUNK, CHUNK), 1)
    causal = row >= col
    tri = jnp.where(causal, 1.0, 0.0).astype(BF16)
    offset = carry_ref[0:1, :]
    c_blocks = []
    for r in range(tm // CHUNK):
        blk = log_f[r * CHUNK:(r + 1) * CHUNK]
        s = offset
        for piece in _split3(blk):
            s = s + jnp.dot(tri, piece, preferred_element_type=F32)
        c_blocks.append(s)
        offset = s[CHUNK - 1:CHUNK, :]
    carry_ref[0:1, :] = offset
    c = jnp.concatenate(c_blocks, axis=0)

    u = uv[:, :gm_w]
    vg = uv[:, gm_w:]
    sq = (vg * vg).astype(BF16)
    gsum = gsum_ref[...]
    msg = jnp.concatenate(
        [jnp.dot(sq[:, s * MXU_DIM:(s + 1) * MXU_DIM], gsum, preferred_element_type=F32)
         for s in range(gm_w // MXU_DIM)], axis=1)
    vgn = (vg * lax.rsqrt(msg + EPS)) * gn_ref[...]

    groups_per_slab = MXU_DIM // GROUP_DIM
    lane_grp = lax.broadcasted_iota(jnp.int32, (CHUNK, MXU_DIM), 1) // GROUP_DIM
    w_masked = [jnp.where(causal, ws_ref[g], 0.0).astype(BF16) for g in range(ws_ref.shape[0])]
    w_slab = [jnp.concatenate(w_masked[s * groups_per_slab:(s + 1) * groups_per_slab], axis=1)
              for s in range(gm_w // MXU_DIM)]
    mixed_rows = []
    for r in range(tm // CHUNK):
        slabs = []
        for s in range(gm_w // MXU_DIM):
            vs = vgn[r * CHUNK:(r + 1) * CHUNK, s * MXU_DIM:(s + 1) * MXU_DIM]
            stacked = jnp.concatenate(
                [jnp.where(lane_grp == j, vs, 0.0).astype(BF16) for j in range(groups_per_slab)],
                axis=0)
            slabs.append(jnp.dot(w_slab[s], stacked, preferred_element_type=F32))
        mixed_rows.append(jnp.concatenate(slabs, axis=1) + bs_ref[...])
    mixed = jnp.concatenate(mixed_rows, axis=0)
    sg_ref[0] = (u * mixed).astype(BF16)

    pieces = [piece.astype(F32) for piece in _split3(c * LOG2E)]
    n_pieces = len(pieces)
    assert 2 * n_pieces <= AUG_W
    slot = lax.broadcasted_iota(jnp.int32, (tm, LANES), 1) % AUG_W
    qa = jnp.where(slot < 2 * n_pieces, 1.0, 0.0)
    ka = qa
    for j, piece in enumerate(pieces):
        qa = jnp.where(slot == j, piece, qa)
        ka = jnp.where(slot == n_pieces + j, -piece, ka)
    qat = qa.T
    lane = lax.broadcasted_iota(jnp.int32, (tm, LANES), 1)
    pad_rows = 2 * SUBLANES - AUG_W
    for h in range(n_heads):
        decay_rows = jnp.concatenate(
            [qat[h * AUG_W:(h + 1) * AUG_W], jnp.zeros((pad_rows, tm), F32)], axis=0)
        qt_ref[0, h, HEAD_DIM:HEAD_DIM + 2 * SUBLANES, :] = decay_rows.astype(BF16)
        qt_ref[0, h, HEAD_DIM + 2 * SUBLANES:, :] = jnp.zeros(
            (LANES - HEAD_DIM - 2 * SUBLANES, tm), BF16)
        kh = k[:, (h // 2) * LANES:(h // 2 + 1) * LANES]
        if h % 2:
            kh = pltpu.roll(kh, HEAD_DIM, axis=1)
        kah = pltpu.roll(ka, HEAD_DIM - h * AUG_W, axis=1)
        k_ref[0, :, h * LANES:(h + 1) * LANES] = jnp.where(
            lane < HEAD_DIM, kh, jnp.where(lane < HEAD_DIM + AUG_W, kah, 0.0)).astype(BF16)


def _proj_call(x, g, wk, wqvt, wuv, wg, bfr, gn, ws, bs, gsum):
    b, s, d = x.shape
    tm = PROJ_TILE
    att_w = wk.shape[1]
    n_heads = att_w // HEAD_DIM
    gm_w = wuv.shape[1] // 2
    tile = lambda w: pl.BlockSpec((1, tm, w), lambda i, j: (i, j, 0))
    head_t = lambda rows: pl.BlockSpec((1, n_heads, rows, tm), lambda i, j: (i, 0, 0, j))
    out_shape = [jax.ShapeDtypeStruct((b, n_heads, LANES, s), BF16),
                 jax.ShapeDtypeStruct((b, s, n_heads * LANES), BF16),
                 jax.ShapeDtypeStruct((b, n_heads, LANES, s), BF16),
                 jax.ShapeDtypeStruct((b, s, gm_w), BF16)]
    return pl.pallas_call(
        functools.partial(_proj_kernel, att_w=att_w, gm_w=gm_w),
        grid=(b, s // tm),
        in_specs=[tile(d), _const_spec(g.shape), _const_spec(wk.shape), _const_spec(wqvt.shape),
                  _const_spec(wuv.shape), _const_spec(wg.shape), _const_spec(bfr.shape),
                  _const_spec(gn.shape), _const_spec(ws.shape), _const_spec(bs.shape),
                  _const_spec(gsum.shape)],
        out_specs=[head_t(LANES), tile(n_heads * LANES), head_t(LANES), tile(gm_w)],
        out_shape=out_shape,
        scratch_shapes=[pltpu.VMEM((SUBLANES, LANES), F32)],
        compiler_params=pltpu.CompilerParams(
            dimension_semantics=("arbitrary", "arbitrary"), vmem_limit_bytes=VMEM_LIMIT_BYTES),
        name="proj",
    )(x, g, wk, wqvt, wuv, wg, bfr, gn, ws, bs, gsum)


def _attn_kernel(qt_ref, k_ref, vt_ref, o_ref, m_ref, acc_ref, s_ref):
    t = s_ref.shape[2]
    heads = qt_ref.shape[1]
    key = lax.broadcasted_iota(jnp.int32, (t, t), 0)
    qry = lax.broadcasted_iota(jnp.int32, (t, t), 1)
    for w in range(Q_TILES_PER_STEP):
        _attn_query_tile(qt_ref, k_ref, vt_ref, o_ref, m_ref, acc_ref, s_ref, key, qry,
                         pl.program_id(2), w, t, heads)


def _attn_query_tile(qt_ref, k_ref, vt_ref, o_ref, m_ref, acc_ref, s_ref, key, qry, step, w, t,
                     heads):
    qi = Q_TILES_PER_STEP * step + w
    cols = slice(w * t, (w + 1) * t)
    m_ref[...] = jnp.full_like(m_ref, -jnp.inf)
    acc_ref[...] = jnp.zeros_like(acc_ref)

    def scores(tile, par):
        start = pl.multiple_of(tile * t, t)
        for i in range(heads):
            kf = k_ref[0, pl.ds(start, t), i * LANES:(i + 1) * LANES]
            s_ref[par, i] = jnp.dot(kf, qt_ref[0, i, :, cols],
                                    preferred_element_type=F32)

    def softmax_pv(tile, par, diagonal=False):
        start = pl.multiple_of(tile * t, t)
        for i in range(heads):
            vt = vt_ref[0, i, :, pl.ds(start, t)]
            s = s_ref[par, i]
            if diagonal:
                s = jnp.where(qry >= key, s, -jnp.inf)
            m_prev = m_ref[i]
            m_next = jnp.maximum(m_prev, jnp.max(s, axis=0, keepdims=True))
            alpha = jnp.exp2(m_prev - m_next)
            p = jnp.exp2(s - m_next).astype(BF16)
            pv = jnp.dot(vt, p, preferred_element_type=F32)
            acc_ref[i] = acc_ref[i] * alpha + pv
            m_ref[i] = m_next

    scores(0, 0)

    def tile_pair(i):
        scores(2 * i + 1, 1)
        softmax_pv(2 * i, 0)
        scores(2 * i + 2, 0)
        softmax_pv(2 * i + 1, 1)

    def body(i, carry):
        tile_pair(2 * i)
        tile_pair(2 * i + 1)
        return carry

    trips = (Q_TILES_PER_STEP // 4) * step + (w // 2) // 2
    lax.fori_loop(0, trips, body, 0)
    if (w // 2) % 2:
        tile_pair(2 * trips)

    if w % 2:
        scores(qi, 1)
        softmax_pv(qi - 1, 0)
        softmax_pv(qi, 1, diagonal=True)
    else:
        softmax_pv(qi, 0, diagonal=True)

    outs = [acc_ref[i, :HEAD_DIM, :] / acc_ref[i, HEAD_DIM:, :] for i in range(heads)]
    o_ref[0, cols, :] = jnp.concatenate(outs, axis=0).T.astype(o_ref.dtype)


def _attn_call(qt, k, vt):
    b, n_heads, _, s = qt.shape
    t = ATT_TILE
    heads = LANES // HEAD_DIM
    tq = Q_TILES_PER_STEP * t
    qspec = pl.BlockSpec((1, heads, LANES, tq), lambda i, p, j: (i, p, 0, j))
    kspec = pl.BlockSpec((1, s, heads * LANES), lambda i, p, j: (i, 0, p))
    vspec = pl.BlockSpec((1, heads, LANES, s), lambda i, p, j: (i, p, 0, 0))
    return pl.pallas_call(
        _attn_kernel,
        grid=(b, n_heads // heads, s // tq),
        in_specs=[qspec, kspec, vspec],
        out_specs=pl.BlockSpec((1, tq, LANES), lambda i, p, j: (i, j, p)),
        out_shape=jax.ShapeDtypeStruct((b, s, n_heads * HEAD_DIM), BF16),
        scratch_shapes=[pltpu.VMEM((heads, 1, t), F32),
                        pltpu.VMEM((heads, LANES, t), F32),
                        pltpu.VMEM((2, heads, t, t), F32)],
        compiler_params=pltpu.CompilerParams(
            dimension_semantics=("arbitrary", "arbitrary", "arbitrary"),
            vmem_limit_bytes=VMEM_LIMIT_BYTES),
        name="attn",
    )(qt, k, vt)


def _store_lane_tiles(ref, value):
    for c in range(ref.shape[0]):
        ref[c] = value[:, c * LANES:(c + 1) * LANES]


def _load_row_groups(ref, starts_strides):
    return jnp.concatenate(
        [jnp.concatenate([ref[c, pl.ds(start, SUBLANES, stride=stride), :]
                          for c in range(ref.shape[0])], axis=1)
         for start, stride in starts_strides], axis=0)


def _ffn_kernel(x_ref, att_ref, sg_ref, wo_ref, g2_ref, wup_ref, cw_ref, cb_ref, wdn_ref, g3_ref,
                o_ref, h1_ref, hnat_ref, hn_ref, hbuf_ref, act_ref, res_ref, carry_ref, *,
                final_norm):
    tm = x_ref.shape[1]
    n_chunks = wup_ref.shape[0]
    groups = tm // SUBLANES
    halo = (CONV_WIDTH - 1) * SUBLANES

    @pl.when(pl.program_id(1) == 0)
    def _():
        carry_ref[...] = jnp.zeros_like(carry_ref)

    mix = jnp.concatenate([att_ref[0], sg_ref[0]], axis=1)
    h1 = x_ref[0] + jnp.dot(mix, wo_ref[...], preferred_element_type=F32)
    h1_ref[...] = h1
    ms = jnp.mean(h1 * h1, axis=-1, keepdims=True)
    _store_lane_tiles(hnat_ref, (h1 * lax.rsqrt(ms + EPS)) * g2_ref[...])

    hn_ref[...] = _load_row_groups(hnat_ref, [(k, groups) for k in range(groups)]).astype(BF16)
    first_sublane = lax.broadcasted_iota(jnp.int32, (SUBLANES, 2 * FF_CHUNK), 0) == 0

    def up(c):
        hbuf = hbuf_ref.at[c % 2]
        hbuf[halo:, :] = jnp.dot(hn_ref[...], wup_ref[c], preferred_element_type=F32)
        for j in range(CONV_WIDTH - 1):
            prev_tile = carry_ref[c, j * SUBLANES:(j + 1) * SUBLANES, :]
            this_tile = hbuf[tm + j * SUBLANES:tm + (j + 1) * SUBLANES, :]
            hbuf[j * SUBLANES:(j + 1) * SUBLANES, :] = jnp.where(
                first_sublane, pltpu.roll(prev_tile, 1, axis=0), pltpu.roll(this_tile, 1, axis=0))
        carry_ref[c] = hbuf[tm:tm + halo, :]

    def gate(c):
        hbuf = hbuf_ref.at[c % 2]
        cw = cw_ref[c]
        y = cb_ref[c]
        for tap in range(CONV_WIDTH):
            lo = tap * SUBLANES
            y = y + hbuf[lo:lo + tm, :] * cw[tap:tap + 1, :]
        a = y[:, :FF_CHUNK]
        g = y[:, FF_CHUNK:]
        act_ref[:, c * FF_CHUNK:(c + 1) * FF_CHUNK] = (jax.nn.silu(g) * a).astype(BF16)

    up(0)
    for c in range(n_chunks):
        if c + 1 < n_chunks:
            up(c + 1)
        gate(c)

    _store_lane_tiles(res_ref, jnp.dot(act_ref[...], wdn_ref[...], preferred_element_type=F32))
    ffn = _load_row_groups(
        res_ref, [(SUBLANES * ((SUBLANES * k) % groups) + (SUBLANES * k) // groups, SUBLANES)
                  for k in range(groups)])
    h2 = h1_ref[...] + ffn
    if final_norm:
        ms2 = jnp.mean(h2 * h2, axis=-1, keepdims=True)
        h2 = (h2 * lax.rsqrt(ms2 + EPS)) * g3_ref[...]
    o_ref[0] = h2


def _ffn_call(x, att, sg, wo, g2, wup, cw, cb, wdn, g3, final_norm):
    b, s, d = x.shape
    tm = TOKEN_TILE
    n_chunks = wup.shape[0]
    d_ff = wdn.shape[0]
    halo = (CONV_WIDTH - 1) * SUBLANES
    tile = lambda w: pl.BlockSpec((1, tm, w), lambda i, j: (i, j, 0))
    return pl.pallas_call(
        functools.partial(_ffn_kernel, final_norm=final_norm),
        grid=(b, s // tm),
        in_specs=[tile(d), tile(att.shape[2]), tile(sg.shape[2]), _const_spec(wo.shape),
                  _const_spec(g2.shape), _const_spec(wup.shape), _const_spec(cw.shape),
                  _const_spec(cb.shape), _const_spec(wdn.shape), _const_spec(g3.shape)],
        out_specs=tile(d),
        out_shape=jax.ShapeDtypeStruct((b, s, d), F32),
        scratch_shapes=[pltpu.VMEM((tm, d), F32),
                        pltpu.VMEM((d // LANES, tm, LANES), F32),
                        pltpu.VMEM((tm, d), BF16),
                        pltpu.VMEM((2, tm + halo, 2 * FF_CHUNK), F32),
                        pltpu.VMEM((tm, d_ff), BF16),
                        pltpu.VMEM((d // LANES, tm, LANES), F32),
                        pltpu.VMEM((n_chunks, halo, 2 * FF_CHUNK), F32)],
        compiler_params=pltpu.CompilerParams(
            dimension_semantics=("arbitrary", "arbitrary"), vmem_limit_bytes=VMEM_LIMIT_BYTES),
        name="ffn",
    )(x, att, sg, wo, g2, wup, cw, cb, wdn, g3)


def kernel(x, norm_mix_g, w_in, b_forget, gmlp_norm_g, w_spatial, b_spatial, w_out, norm_ffn_g,
           w_up, conv_w, conv_b, w_down, norm_final_g):
    depth, d_model, _ = w_in.shape
    n_heads = b_forget.shape[1]
    att_w = n_heads * HEAD_DIM
    gm_w = gmlp_norm_g.shape[1]
    n_groups = w_spatial.shape[1]
    d_ff = w_down.shape[1]
    assert gm_w == n_groups * GROUP_DIM and w_spatial.shape[2] == CHUNK
    assert n_heads * AUG_W <= LANES and att_w % LANES == 0 and gm_w % MXU_DIM == 0
    assert d_ff % FF_CHUNK == 0 and x.shape[1] % TOKEN_TILE == 0 and x.shape[1] % PROJ_TILE == 0
    assert Q_TILES_PER_STEP % 4 == 0 and x.shape[1] % (Q_TILES_PER_STEP * ATT_TILE) == 0
    n_chunks = d_ff // FF_CHUNK
    scale = HEAD_DIM ** -0.5

    gi = jnp.arange(MXU_DIM) // GROUP_DIM
    gsum = jnp.where(gi[:, None] == gi[None, :], 1.0 / GROUP_DIM, 0.0).astype(BF16)

    h = x
    for layer in range(depth):
        w = w_in[layer]
        wq = w[:, :att_w] * scale
        wk = w[:, att_w:2 * att_w].astype(BF16)
        wqvt = jnp.concatenate([wq, w[:, 2 * att_w:3 * att_w]], axis=1).T.astype(BF16)
        wuv = w[:, 3 * att_w:3 * att_w + 2 * gm_w].astype(BF16)
        wgate = w[:, 3 * att_w + 2 * gm_w:]
        wg = jnp.zeros((d_model, LANES), F32).at[:, :n_heads * AUG_W].set(
            jnp.repeat(wgate, AUG_W, axis=1)).astype(BF16)
        bfr = jnp.zeros((1, LANES), F32).at[0, :n_heads * AUG_W].set(
            jnp.repeat(b_forget[layer], AUG_W))
        bs = jnp.repeat(b_spatial[layer].T, GROUP_DIM, axis=1)

        qt, k, vt, sg = _proj_call(
            h, norm_mix_g[layer][None, :], wk, wqvt, wuv, wg, bfr, gmlp_norm_g[layer][None, :],
            w_spatial[layer], bs, gsum)
        att = _attn_call(qt, k, vt)

        def chunked(m):
            a = m[..., :d_ff].reshape(m.shape[:-1] + (n_chunks, FF_CHUNK))
            g = m[..., d_ff:].reshape(m.shape[:-1] + (n_chunks, FF_CHUNK))
            return jnp.moveaxis(jnp.concatenate([a, g], axis=-1), -2, 0)
        wup = chunked(w_up[layer]).astype(BF16)
        cw = chunked(conv_w[layer])
        cb = chunked(conv_b[layer][None, :])
        h = _ffn_call(h, att, sg, w_out[layer].astype(BF16), norm_ffn_g[layer][None, :], wup, cw, cb,
                      w_down[layer].astype(BF16), norm_final_g[None, :], layer == depth - 1)
    return h
```

```python
import functools

import jax
import jax.numpy as jnp
from jax import lax
from jax.experimental import pallas as pl
from jax.experimental.pallas import tpu as pltpu

EPS = 1e-6
LOG2E = 1.4426950408889634
HEAD_DIM = 64
GROUP_DIM = 64
CHUNK = 128
CONV_WIDTH = 3

LANES = 128
SUBLANES = 8
MXU_DIM = 256
VMEM_LIMIT_BYTES = 56 * 1024 * 1024

AUG_W = 8
PROJ_TILE = 1024
TOKEN_TILE = 512
ATT_TILE = 512
Q_TILES_PER_STEP = 4
FF_CHUNK = 256

F32 = jnp.float32
BF16 = jnp.bfloat16


def _split3(x):
    hi = x.astype(BF16)
    r1 = x - hi.astype(F32)
    mid = r1.astype(BF16)
    lo = (r1 - mid.astype(F32)).astype(BF16)
    return hi, mid, lo


def _const_spec(shape):
    zeros = (0,) * len(shape)
    return pl.BlockSpec(shape, lambda *_: zeros, pipeline_mode=pl.Buffered(1))


def _proj_kernel(x_ref, g_ref, wk_ref, wqvt_ref, wuv_ref, wg_ref, bf_ref, gn_ref, ws_ref, bs_ref,
                 gsum_ref, qt_ref, k_ref, vt_ref, sg_ref, carry_ref, *, att_w, gm_w):
    tm = x_ref.shape[1]

    @pl.when(pl.program_id(1) == 0)
    def _():
        carry_ref[...] = jnp.zeros_like(carry_ref)

    x = x_ref[0]
    ms = jnp.mean(x * x, axis=-1, keepdims=True)
    xn = ((x * lax.rsqrt(ms + EPS)) * g_ref[...]).astype(BF16)

    n_heads = att_w // HEAD_DIM
    uv = jax.nn.gelu(jnp.dot(xn, wuv_ref[...], preferred_element_type=F32))
    z = jnp.dot(xn, wg_ref[...], preferred_element_type=F32) + bf_ref[...]
    log_f = -(jnp.maximum(-z, 0.0) + jnp.log1p(jnp.exp(-jnp.abs(z))))
    k = jnp.dot(xn, wk_ref[...], preferred_element_type=F32)
    qvt = lax.dot_general(wqvt_ref[...], xn, (((1,), (1,)), ((), ())), preferred_element_type=F32)
    qt = qvt[:att_w] * LOG2E
    vt = qvt[att_w:]
    for h in range(n_heads):
        qt_ref[0, h, :HEAD_DIM, :] = qt[h * HEAD_DIM:(h + 1) * HEAD_DIM].astype(BF16)
        vt_ref[0, h, :HEAD_DIM, :] = vt[h * HEAD_DIM:(h + 1) * HEAD_DIM].astype(BF16)
        vt_ref[0, h, HEAD_DIM:, :] = jnp.ones((LANES - HEAD_DIM, tm), BF16)

    row = lax.broadcasted_iota(jnp.int32, (CHUNK, CHUNK), 0)
    col = lax.broadcasted_iota(jnp.int32, (CHUNK, CHUNK), 1)
    causal = row >= col
    tri = jnp.where(causal, 1.0, 0.0).astype(BF16)
    offset = carry_ref[0:1, :]
    c_blocks = []
    for r in range(tm // CHUNK):
        blk = log_f[r * CHUNK:(r + 1) * CHUNK]
        s = offset
        for piece in _split3(blk):
            s = s + jnp.dot(tri, piece, preferred_element_type=F32)
        c_blocks.append(s)
        offset = s[CHUNK - 1:CHUNK, :]
    carry_ref[0:1, :] = offset
    c = jnp.concatenate(c_blocks, axis=0)

    u = uv[:, :gm_w]
    vg = uv[:, gm_w:]
    sq = (vg * vg).astype(BF16)
    gsum = gsum_ref[...]
    msg = jnp.concatenate(
        [jnp.dot(sq[:, s * MXU_DIM:(s + 1) * MXU_DIM], gsum, preferred_element_type=F32)
         for s in range(gm_w // MXU_DIM)], axis=1)
    vgn = (vg * lax.rsqrt(msg + EPS)) * gn_ref[...]

    groups_per_slab = MXU_DIM // GROUP_DIM
    lane_grp = lax.broadcasted_iota(jnp.int32, (CHUNK, MXU_DIM), 1) // GROUP_DIM
    w_masked = [jnp.where(causal, ws_ref[g], 0.0).astype(BF16) for g in range(ws_ref.shape[0])]
    w_slab = [jnp.concatenate(w_masked[s * groups_per_slab:(s + 1) * groups_per_slab], axis=1)
              for s in range(gm_w // MXU_DIM)]
    mixed_rows = []
    for r in range(tm // CHUNK):
        slabs = []
        for s in range(gm_w // MXU_DIM):
            vs = vgn[r * CHUNK:(r + 1) * CHUNK, s * MXU_DIM:(s + 1) * MXU_DIM]
            stacked = jnp.concatenate(
                [jnp.where(lane_grp == j, vs, 0.0).astype(BF16) for j in range(groups_per_slab)],
                axis=0)
            slabs.append(jnp.dot(w_slab[s], stacked, preferred_element_type=F32))
        mixed_rows.append(jnp.concatenate(slabs, axis=1) + bs_ref[...])
    mixed = jnp.concatenate(mixed_rows, axis=0)
    sg_ref[0] = (u * mixed).astype(BF16)

    pieces = [piece.astype(F32) for piece in _split3(c * LOG2E)]
    n_pieces = len(pieces)
    assert 2 * n_pieces <= AUG_W
    slot = lax.broadcasted_iota(jnp.int32, (tm, LANES), 1) % AUG_W
    qa = jnp.where(slot < 2 * n_pieces, 1.0, 0.0)
    ka = qa
    for j, piece in enumerate(pieces):
        qa = jnp.where(slot == j, piece, qa)
        ka = jnp.where(slot == n_pieces + j, -piece, ka)
    qat = qa.T
    lane = lax.broadcasted_iota(jnp.int32, (tm, LANES), 1)
    pad_rows = 2 * SUBLANES - AUG_W
    for h in range(n_heads):
        decay_rows = jnp.concatenate(
            [qat[h * AUG_W:(h + 1) * AUG_W], jnp.zeros((pad_rows, tm), F32)], axis=0)
        qt_ref[0, h, HEAD_DIM:HEAD_DIM + 2 * SUBLANES, :] = decay_rows.astype(BF16)
        qt_ref[0, h, HEAD_DIM + 2 * SUBLANES:, :] = jnp.zeros(
            (LANES - HEAD_DIM - 2 * SUBLANES, tm), BF16)
        kh = k[:, (h // 2) * LANES:(h // 2 + 1) * LANES]
        if h % 2:
            kh = pltpu.roll(kh, HEAD_DIM, axis=1)
        kah = pltpu.roll(ka, HEAD_DIM - h * AUG_W, axis=1)
        k_ref[0, :, h * LANES:(h + 1) * LANES] = jnp.where(
            lane < HEAD_DIM, kh, jnp.where(lane < HEAD_DIM + AUG_W, kah, 0.0)).astype(BF16)


def _proj_call(x, g, wk, wqvt, wuv, wg, bfr, gn, ws, bs, gsum):
    b, s, d = x.shape
    tm = PROJ_TILE
    att_w = wk.shape[1]
    n_heads = att_w // HEAD_DIM
    gm_w = wuv.shape[1] // 2
    tile = lambda w: pl.BlockSpec((1, tm, w), lambda i, j: (i, j, 0))
    head_t = lambda rows: pl.BlockSpec((1, n_heads, rows, tm), lambda i, j: (i, 0, 0, j))
    out_shape = [jax.ShapeDtypeStruct((b, n_heads, LANES, s), BF16),
                 jax.ShapeDtypeStruct((b, s, n_heads * LANES), BF16),
                 jax.ShapeDtypeStruct((b, n_heads, LANES, s), BF16),
                 jax.ShapeDtypeStruct((b, s, gm_w), BF16)]
    return pl.pallas_call(
        functools.partial(_proj_kernel, att_w=att_w, gm_w=gm_w),
        grid=(b, s // tm),
        in_specs=[tile(d), _const_spec(g.shape), _const_spec(wk.shape), _const_spec(wqvt.shape),
                  _const_spec(wuv.shape), _const_spec(wg.shape), _const_spec(bfr.shape),
                  _const_spec(gn.shape), _const_spec(ws.shape), _const_spec(bs.shape),
                  _const_spec(gsum.shape)],
        out_specs=[head_t(LANES), tile(n_heads * LANES), head_t(LANES), tile(gm_w)],
        out_shape=out_shape,
        scratch_shapes=[pltpu.VMEM((SUBLANES, LANES), F32)],
        compiler_params=pltpu.CompilerParams(
            dimension_semantics=("arbitrary", "arbitrary"), vmem_limit_bytes=VMEM_LIMIT_BYTES),
        name="proj",
    )(x, g, wk, wqvt, wuv, wg, bfr, gn, ws, bs, gsum)


def _attn_kernel(qt_ref, k_ref, vt_ref, o_ref, m_ref, acc_ref, s_ref):
    t = s_ref.shape[2]
    heads = qt_ref.shape[1]
    key = lax.broadcasted_iota(jnp.int32, (t, t), 0)
    qry = lax.broadcasted_iota(jnp.int32, (t, t), 1)
    for w in range(Q_TILES_PER_STEP):
        _attn_query_tile(qt_ref, k_ref, vt_ref, o_ref, m_ref, acc_ref, s_ref, key, qry,
                         pl.program_id(2), w, t, heads)


def _attn_query_tile(qt_ref, k_ref, vt_ref, o_ref, m_ref, acc_ref, s_ref, key, qry, step, w, t,
                     heads):
    qi = Q_TILES_PER_STEP * step + w
    cols = slice(w * t, (w + 1) * t)
    m_ref[...] = jnp.full_like(m_ref, -jnp.inf)
    acc_ref[...] = jnp.zeros_like(acc_ref)

    def scores(tile, par, only=None):
        start = pl.multiple_of(tile * t, t)
        for i in (range(heads) if only is None else [only]):
            kf = k_ref[0, pl.ds(start, t), i * LANES:(i + 1) * LANES]
            s_ref[par, i] = jnp.dot(kf, qt_ref[0, i, :, cols],
                                    preferred_element_type=F32)

    def softmax_pv(tile, par, diagonal=False, only=None):
        start = pl.multiple_of(tile * t, t)
        for i in (range(heads) if only is None else [only]):
            vt = vt_ref[0, i, :, pl.ds(start, t)]
            s = s_ref[par, i]
            if diagonal:
                s = jnp.where(qry >= key, s, -jnp.inf)
            m_prev = m_ref[i]
            m_next = jnp.maximum(m_prev, jnp.max(s, axis=0, keepdims=True))
            alpha = jnp.exp2(m_prev - m_next)
            p = jnp.exp2(s - m_next).astype(BF16)
            pv = jnp.dot(vt, p, preferred_element_type=F32)
            acc_ref[i] = acc_ref[i] * alpha + pv
            m_ref[i] = m_next

    scores(0, 0)

    def tile_pair(i):
        for h in range(heads):
            scores(2 * i + 1, 1, only=h)
            softmax_pv(2 * i, 0, only=h)
        for h in range(heads):
            scores(2 * i + 2, 0, only=h)
            softmax_pv(2 * i + 1, 1, only=h)

    def body(i, carry):
        tile_pair(2 * i)
        tile_pair(2 * i + 1)
        return carry

    trips = (Q_TILES_PER_STEP // 4) * step + (w // 2) // 2
    lax.fori_loop(0, trips, body, 0)
    if (w // 2) % 2:
        tile_pair(2 * trips)

    if w % 2:
        scores(qi, 1)
        softmax_pv(qi - 1, 0)
        softmax_pv(qi, 1, diagonal=True)
    else:
        softmax_pv(qi, 0, diagonal=True)

    outs = [acc_ref[i, :HEAD_DIM, :] / acc_ref[i, HEAD_DIM:, :] for i in range(heads)]
    o_ref[0, cols, :] = jnp.concatenate(outs, axis=0).T.astype(o_ref.dtype)


def _attn_call(qt, k, vt):
    b, n_heads, _, s = qt.shape
    t = ATT_TILE
    heads = LANES // HEAD_DIM
    tq = Q_TILES_PER_STEP * t
    qspec = pl.BlockSpec((1, heads, LANES, tq), lambda i, p, j: (i, p, 0, j))
    kspec = pl.BlockSpec((1, s, heads * LANES), lambda i, p, j: (i, 0, p))
    vspec = pl.BlockSpec((1, heads, LANES, s), lambda i, p, j: (i, p, 0, 0))
    return pl.pallas_call(
        _attn_kernel,
        grid=(b, n_heads // heads, s // tq),
        in_specs=[qspec, kspec, vspec],
        out_specs=pl.BlockSpec((1, tq, LANES), lambda i, p, j: (i, j, p)),
        out_shape=jax.ShapeDtypeStruct((b, s, n_heads * HEAD_DIM), BF16),
        scratch_shapes=[pltpu.VMEM((heads, 1, t), F32),
                        pltpu.VMEM((heads, LANES, t), F32),
                        pltpu.VMEM((2, heads, t, t), F32)],
        compiler_params=pltpu.CompilerParams(
            dimension_semantics=("arbitrary", "arbitrary", "arbitrary"),
            vmem_limit_bytes=VMEM_LIMIT_BYTES),
        name="attn",
    )(qt, k, vt)


def _store_lane_tiles(ref, value):
    for c in range(ref.shape[0]):
        ref[c] = value[:, c * LANES:(c + 1) * LANES]


def _load_row_groups(ref, starts_strides):
    return jnp.concatenate(
        [jnp.concatenate([ref[c, pl.ds(start, SUBLANES, stride=stride), :]
                          for c in range(ref.shape[0])], axis=1)
         for start, stride in starts_strides], axis=0)


def _ffn_kernel(x_ref, att_ref, sg_ref, wo_ref, g2_ref, wup_ref, cw_ref, cb_ref, wdn_ref, g3_ref,
                o_ref, h1_ref, hnat_ref, hn_ref, hbuf_ref, act_ref, res_ref, carry_ref, *,
                final_norm):
    tm = x_ref.shape[1]
    n_chunks = wup_ref.shape[0]
    groups = tm // SUBLANES
    halo = (CONV_WIDTH - 1) * SUBLANES

    @pl.when(pl.program_id(1) == 0)
    def _():
        carry_ref[...] = jnp.zeros_like(carry_ref)

    mix = jnp.concatenate([att_ref[0], sg_ref[0]], axis=1)
    h1 = x_ref[0] + jnp.dot(mix, wo_ref[...], preferred_element_type=F32)
    h1_ref[...] = h1
    ms = jnp.mean(h1 * h1, axis=-1, keepdims=True)
    _store_lane_tiles(hnat_ref, (h1 * lax.rsqrt(ms + EPS)) * g2_ref[...])

    hn_ref[...] = _load_row_groups(hnat_ref, [(k, groups) for k in range(groups)]).astype(BF16)
    first_sublane = lax.broadcasted_iota(jnp.int32, (SUBLANES, 2 * FF_CHUNK), 0) == 0

    def up(c):
        hbuf = hbuf_ref.at[c % 2]
        hbuf[halo:, :] = jnp.dot(hn_ref[...], wup_ref[c], preferred_element_type=F32)
        for j in range(CONV_WIDTH - 1):
            prev_tile = carry_ref[c, j * SUBLANES:(j + 1) * SUBLANES, :]
            this_tile = hbuf[tm + j * SUBLANES:tm + (j + 1) * SUBLANES, :]
            hbuf[j * SUBLANES:(j + 1) * SUBLANES, :] = jnp.where(
                first_sublane, pltpu.roll(prev_tile, 1, axis=0), pltpu.roll(this_tile, 1, axis=0))
        carry_ref[c] = hbuf[tm:tm + halo, :]

    def gate(c):
        hbuf = hbuf_ref.at[c % 2]
        cw = cw_ref[c]
        y = cb_ref[c]
        for tap in range(CONV_WIDTH):
            lo = tap * SUBLANES
            y = y + hbuf[lo:lo + tm, :] * cw[tap:tap + 1, :]
        a = y[:, :FF_CHUNK]
        g = y[:, FF_CHUNK:]
        act_ref[:, c * FF_CHUNK:(c + 1) * FF_CHUNK] = (jax.nn.silu(g) * a).astype(BF16)

    up(0)
    for c in range(n_chunks):
        if c + 1 < n_chunks:
            up(c + 1)
        gate(c)

    _store_lane_tiles(res_ref, jnp.dot(act_ref[...], wdn_ref[...], preferred_element_type=F32))
    ffn = _load_row_groups(
        res_ref, [(SUBLANES * ((SUBLANES * k) % groups) + (SUBLANES * k) // groups, SUBLANES)
                  for k in range(groups)])
    h2 = h1_ref[...] + ffn
    if final_norm:
        ms2 = jnp.mean(h2 * h2, axis=-1, keepdims=True)
        h2 = (h2 * lax.rsqrt(ms2 + EPS)) * g3_ref[...]
    o_ref[0] = h2


def _ffn_call(x, att, sg, wo, g2, wup, cw, cb, wdn, g3, final_norm):
    b, s, d = x.shape
    tm = TOKEN_TILE
    n_chunks = wup.shape[0]
    d_ff = wdn.shape[0]
    halo = (CONV_WIDTH - 1) * SUBLANES
    tile = lambda w: pl.BlockSpec((1, tm, w), lambda i, j: (i, j, 0))
    return pl.pallas_call(
        functools.partial(_ffn_kernel, final_norm=final_norm),
        grid=(b, s // tm),
        in_specs=[tile(d), tile(att.shape[2]), tile(sg.shape[2]), _const_spec(wo.shape),
                  _const_spec(g2.shape), _const_spec(wup.shape), _const_spec(cw.shape),
                  _const_spec(cb.shape), _const_spec(wdn.shape), _const_spec(g3.shape)],
        out_specs=tile(d),
        out_shape=jax.ShapeDtypeStruct((b, s, d), F32),
        scratch_shapes=[pltpu.VMEM((tm, d), F32),
                        pltpu.VMEM((d // LANES, tm, LANES), F32),
                        pltpu.VMEM((tm, d), BF16),
                        pltpu.VMEM((2, tm + halo, 2 * FF_CHUNK), F32),
                        pltpu.VMEM((tm, d_ff), BF16),
                        pltpu.VMEM((d // LANES, tm, LANES), F32),
                        pltpu.VMEM((n_chunks, halo, 2 * FF_CHUNK), F32)],
        compiler_params=pltpu.CompilerParams(
            dimension_semantics=("arbitrary", "arbitrary"), vmem_limit_bytes=VMEM_LIMIT_BYTES),
        name="ffn",
    )(x, att, sg, wo, g2, wup, cw, cb, wdn, g3)


def kernel(x, norm_mix_g, w_in, b_forget, gmlp_norm_g, w_spatial, b_spatial, w_out, norm_ffn_g,
           w_up, conv_w, conv_b, w_down, norm_final_g):
    depth, d_model, _ = w_in.shape
    n_heads = b_forget.shape[1]
    att_w = n_heads * HEAD_DIM
    gm_w = gmlp_norm_g.shape[1]
    n_groups = w_spatial.shape[1]
    d_ff = w_down.shape[1]
    assert gm_w == n_groups * GROUP_DIM and w_spatial.shape[2] == CHUNK
    assert n_heads * AUG_W <= LANES and att_w % LANES == 0 and gm_w % MXU_DIM == 0
    assert d_ff % FF_CHUNK == 0 and x.shape[1] % TOKEN_TILE == 0 and x.shape[1] % PROJ_TILE == 0
    assert Q_TILES_PER_STEP % 4 == 0 and x.shape[1] % (Q_TILES_PER_STEP * ATT_TILE) == 0
    n_chunks = d_ff // FF_CHUNK
    scale = HEAD_DIM ** -0.5

    gi = jnp.arange(MXU_DIM) // GROUP_DIM
    gsum = jnp.where(gi[:, None] == gi[None, :], 1.0 / GROUP_DIM, 0.0).astype(BF16)

    h = x
    for layer in range(depth):
        w = w_in[layer]
        wq = w[:, :att_w] * scale
        wk = w[:, att_w:2 * att_w].astype(BF16)
        wqvt = jnp.concatenate([wq, w[:, 2 * att_w:3 * att_w]], axis=1).T.astype(BF16)
        wuv = w[:, 3 * att_w:3 * att_w + 2 * gm_w].astype(BF16)
        wgate = w[:, 3 * att_w + 2 * gm_w:]
        wg = jnp.zeros((d_model, LANES), F32).at[:, :n_heads * AUG_W].set(
            jnp.repeat(wgate, AUG_W, axis=1)).astype(BF16)
        bfr = jnp.zeros((1, LANES), F32).at[0, :n_heads * AUG_W].set(
            jnp.repeat(b_forget[layer], AUG_W))
        bs = jnp.repeat(b_spatial[layer].T, GROUP_DIM, axis=1)

        qt, k, vt, sg = _proj_call(
            h, norm_mix_g[layer][None, :], wk, wqvt, wuv, wg, bfr, gmlp_norm_g[layer][None, :],
            w_spatial[layer], bs, gsum)
        att = _attn_call(qt, k, vt)

        def chunked(m):
            a = m[..., :d_ff].reshape(m.shape[:-1] + (n_chunks, FF_CHUNK))
            g = m[..., d_ff:].reshape(m.shape[:-1] + (n_chunks, FF_CHUNK))
            return jnp.moveaxis(jnp.concatenate([a, g], axis=-1), -2, 0)
        wup = chunked(w_up[layer]).astype(BF16)
        cw = chunked(conv_w[layer])
        cb = chunked(conv_b[layer][None, :])
        h = _ffn_call(h, att, sg, w_out[layer].astype(BF16), norm_ffn_g[layer][None, :], wup, cw, cb,
                      w_down[layer].astype(BF16), norm_final_g[None, :], layer == depth - 1)
    return h
```

```python
import functools

import jax
import jax.numpy as jnp
from jax import lax
from jax.experimental import pallas as pl
from jax.experimental.pallas import tpu as pltpu

EPS = 1e-6
LOG2E = 1.4426950408889634
HEAD_DIM = 64
GROUP_DIM = 64
CHUNK = 128
CONV_WIDTH = 3

LANES = 128
SUBLANES = 8
MXU_DIM = 256
VMEM_LIMIT_BYTES = 56 * 1024 * 1024

AUG_W = 8
PROJ_TILE = 1024
TOKEN_TILE = 512
ATT_TILE = 512
Q_TILES_PER_STEP = 4
FF_CHUNK = 256

F32 = jnp.float32
BF16 = jnp.bfloat16


def _split3(x):
    hi = x.astype(BF16)
    r1 = x - hi.astype(F32)
    mid = r1.astype(BF16)
    lo = (r1 - mid.astype(F32)).astype(BF16)
    return hi, mid, lo


def _const_spec(shape):
    zeros = (0,) * len(shape)
    return pl.BlockSpec(shape, lambda *_: zeros, pipeline_mode=pl.Buffered(1))


def _proj_kernel(x_ref, g_ref, wk_ref, wqvt_ref, wuv_ref, wg_ref, bf_ref, gn_ref, ws_ref, bs_ref,
                 gsum_ref, qt_ref, k_ref, vt_ref, sg_ref, carry_ref, *, att_w, gm_w):
    tm = x_ref.shape[1]

    @pl.when(pl.program_id(1) == 0)
    def _():
        carry_ref[...] = jnp.zeros_like(carry_ref)

    x = x_ref[0]
    ms = jnp.mean(x * x, axis=-1, keepdims=True)
    xn = ((x * lax.rsqrt(ms + EPS)) * g_ref[...]).astype(BF16)

    n_heads = att_w // HEAD_DIM
    uv = jax.nn.gelu(jnp.dot(xn, wuv_ref[...], preferred_element_type=F32))
    z = jnp.dot(xn, wg_ref[...], preferred_element_type=F32) + bf_ref[...]
    log_f = -(jnp.maximum(-z, 0.0) + jnp.log1p(jnp.exp(-jnp.abs(z))))
    k = jnp.dot(xn, wk_ref[...], preferred_element_type=F32)
    qvt = lax.dot_general(wqvt_ref[...], xn, (((1,), (1,)), ((), ())), preferred_element_type=F32)
    qt = qvt[:att_w] * LOG2E
    vt = qvt[att_w:]
    for h in range(n_heads):
        qt_ref[0, h, :HEAD_DIM, :] = qt[h * HEAD_DIM:(h + 1) * HEAD_DIM].astype(BF16)
        vt_ref[0, h, :HEAD_DIM, :] = vt[h * HEAD_DIM:(h + 1) * HEAD_DIM].astype(BF16)
        vt_ref[0, h, HEAD_DIM:, :] = jnp.ones((LANES - HEAD_DIM, tm), BF16)

    row = lax.broadcasted_iota(jnp.int32, (CHUNK, CHUNK), 0)
    col = lax.broadcasted_iota(jnp.int32, (CHUNK, CHUNK), 1)
    causal = row >= col
    tri = jnp.where(causal, 1.0, 0.0).astype(BF16)
    offset = carry_ref[0:1, :]
    c_blocks = []
    for r in range(tm // CHUNK):
        blk = log_f[r * CHUNK:(r + 1) * CHUNK]
        s = offset
        for piece in _split3(blk):
            s = s + jnp.dot(tri, piece, preferred_element_type=F32)
        c_blocks.append(s)
        offset = s[CHUNK - 1:CHUNK, :]
    carry_ref[0:1, :] = offset
    c = jnp.concatenate(c_blocks, axis=0)

    u = uv[:, :gm_w]
    vg = uv[:, gm_w:]
    sq = (vg * vg).astype(BF16)
    gsum = gsum_ref[...]
    msg = jnp.concatenate(
        [jnp.dot(sq[:, s * MXU_DIM:(s + 1) * MXU_DIM], gsum, preferred_element_type=F32)
         for s in range(gm_w // MXU_DIM)], axis=1)
    vgn = (vg * lax.rsqrt(msg + EPS)) * gn_ref[...]

    groups_per_slab = MXU_DIM // GROUP_DIM
    lane_grp = lax.broadcasted_iota(jnp.int32, (CHUNK, MXU_DIM), 1) // GROUP_DIM
    w_masked = [jnp.where(causal, ws_ref[g], 0.0).astype(BF16) for g in range(ws_ref.shape[0])]
    w_slab = [jnp.concatenate(w_masked[s * groups_per_slab:(s + 1) * groups_per_slab], axis=1)
              for s in range(gm_w // MXU_DIM)]
    mixed_rows = []
    for r in range(tm // CHUNK):
        slabs = []
        for s in range(gm_w // MXU_DIM):
            vs = vgn[r * CHUNK:(r + 1) * CHUNK, s * MXU_DIM:(s + 1) * MXU_DIM]
            stacked = jnp.concatenate(
                [jnp.where(lane_grp == j, vs, 0.0).astype(BF16) for j in range(groups_per_slab)],
                axis=0)
            slabs.append(jnp.dot(w_slab[s], stacked, preferred_element_type=F32))
        mixed_rows.append(jnp.concatenate(slabs, axis=1) + bs_ref[...])
    mixed = jnp.concatenate(mixed_rows, axis=0)
    sg_ref[0] = (u * mixed).astype(BF16)

    pieces = [piece.astype(F32) for piece in _split3(c * LOG2E)]
    n_pieces = len(pieces)
    assert 2 * n_pieces <= AUG_W
    slot = lax.broadcasted_iota(jnp.int32, (tm, LANES), 1) % AUG_W
    qa = jnp.where(slot < 2 * n_pieces, 1.0, 0.0)
    ka = qa
    for j, piece in enumerate(pieces):
        qa = jnp.where(slot == j, piece, qa)
        ka = jnp.where(slot == n_pieces + j, -piece, ka)
    qat = qa.T
    lane = lax.broadcasted_iota(jnp.int32, (tm, LANES), 1)
    pad_rows = 2 * SUBLANES - AUG_W
    for h in range(n_heads):
        decay_rows = jnp.concatenate(
            [qat[h * AUG_W:(h + 1) * AUG_W], jnp.zeros((pad_rows, tm), F32)], axis=0)
        qt_ref[0, h, HEAD_DIM:HEAD_DIM + 2 * SUBLANES, :] = decay_rows.astype(BF16)
        qt_ref[0, h, HEAD_DIM + 2 * SUBLANES:, :] = jnp.zeros(
            (LANES - HEAD_DIM - 2 * SUBLANES, tm), BF16)
        kh = k[:, (h // 2) * LANES:(h // 2 + 1) * LANES]
        if h % 2:
            kh = pltpu.roll(kh, HEAD_DIM, axis=1)
        kah = pltpu.roll(ka, HEAD_DIM - h * AUG_W, axis=1)
        k_ref[0, :, h * LANES:(h + 1) * LANES] = jnp.where(
            lane < HEAD_DIM, kh, jnp.where(lane < HEAD_DIM + AUG_W, kah, 0.0)).astype(BF16)


def _proj_call(x, g, wk, wqvt, wuv, wg, bfr, gn, ws, bs, gsum):
    b, s, d = x.shape
    tm = PROJ_TILE
    att_w = wk.shape[1]
    n_heads = att_w // HEAD_DIM
    gm_w = wuv.shape[1] // 2
    tile = lambda w: pl.BlockSpec((1, tm, w), lambda i, j: (i, j, 0))
    head_t = lambda rows: pl.BlockSpec((1, n_heads, rows, tm), lambda i, j: (i, 0, 0, j))
    out_shape = [jax.ShapeDtypeStruct((b, n_heads, LANES, s), BF16),
                 jax.ShapeDtypeStruct((b, s, n_heads * LANES), BF16),
                 jax.ShapeDtypeStruct((b, n_heads, LANES, s), BF16),
                 jax.ShapeDtypeStruct((b, s, gm_w), BF16)]
    return pl.pallas_call(
        functools.partial(_proj_kernel, att_w=att_w, gm_w=gm_w),
        grid=(b, s // tm),
        in_specs=[tile(d), _const_spec(g.shape), _const_spec(wk.shape), _const_spec(wqvt.shape),
                  _const_spec(wuv.shape), _const_spec(wg.shape), _const_spec(bfr.shape),
                  _const_spec(gn.shape), _const_spec(ws.shape), _const_spec(bs.shape),
                  _const_spec(gsum.shape)],
        out_specs=[head_t(LANES), tile(n_heads * LANES), head_t(LANES), tile(gm_w)],
        out_shape=out_shape,
        scratch_shapes=[pltpu.VMEM((SUBLANES, LANES), F32)],
        compiler_params=pltpu.CompilerParams(
            dimension_semantics=("arbitrary", "arbitrary"), vmem_limit_bytes=VMEM_LIMIT_BYTES),
        name="proj",
    )(x, g, wk, wqvt, wuv, wg, bfr, gn, ws, bs, gsum)


def _attn_kernel(qt_ref, k_ref, vt_ref, o_ref, m_all, acc_all, s_all):
    t = s_all.shape[3]
    heads = qt_ref.shape[1]
    step = pl.program_id(2)
    key = lax.broadcasted_iota(jnp.int32, (t, t), 0)
    qry = lax.broadcasted_iota(jnp.int32, (t, t), 1)
    tiles = [_attn_query_tile(qt_ref, k_ref, vt_ref, o_ref, m_all.at[w], acc_all.at[w],
                              s_all.at[w], key, qry, step, w, t, heads)
             for w in range(Q_TILES_PER_STEP)]
    for start, _, _ in tiles:
        start()

    def body(i, carry):
        for pair in (2 * i, 2 * i + 1):
            for _, tile_pair, _ in tiles:
                tile_pair(pair)
        return carry

    lax.fori_loop(0, (Q_TILES_PER_STEP // 4) * step, body, 0)
    for _, _, finish in tiles:
        finish()


def _attn_query_tile(qt_ref, k_ref, vt_ref, o_ref, m_ref, acc_ref, s_ref, key, qry, step, w, t,
                     heads):
    qi = Q_TILES_PER_STEP * step + w
    cols = slice(w * t, (w + 1) * t)

    def scores(tile, par, only=None):
        start = pl.multiple_of(tile * t, t)
        for i in (range(heads) if only is None else [only]):
            kf = k_ref[0, pl.ds(start, t), i * LANES:(i + 1) * LANES]
            s_ref[par, i] = jnp.dot(kf, qt_ref[0, i, :, cols],
                                    preferred_element_type=F32)

    def softmax_pv(tile, par, diagonal=False, only=None):
        start = pl.multiple_of(tile * t, t)
        for i in (range(heads) if only is None else [only]):
            vt = vt_ref[0, i, :, pl.ds(start, t)]
            s = s_ref[par, i]
            if diagonal:
                s = jnp.where(qry >= key, s, -jnp.inf)
            m_prev = m_ref[i]
            m_next = jnp.maximum(m_prev, jnp.max(s, axis=0, keepdims=True))
            alpha = jnp.exp2(m_prev - m_next)
            p = jnp.exp2(s - m_next).astype(BF16)
            pv = jnp.dot(vt, p, preferred_element_type=F32)
            acc_ref[i] = acc_ref[i] * alpha + pv
            m_ref[i] = m_next

    def start():
        m_ref[...] = jnp.full_like(m_ref, -jnp.inf)
        acc_ref[...] = jnp.zeros_like(acc_ref)
        scores(0, 0)

    def tile_pair(i):
        for h in range(heads):
            scores(2 * i + 1, 1, only=h)
            softmax_pv(2 * i, 0, only=h)
        for h in range(heads):
            scores(2 * i + 2, 0, only=h)
            softmax_pv(2 * i + 1, 1, only=h)

    def finish():
        done = 2 * (Q_TILES_PER_STEP // 4) * step
        for extra in range(w // 2):
            tile_pair(done + extra)
        if w % 2:
            scores(qi, 1)
            softmax_pv(qi - 1, 0)
            softmax_pv(qi, 1, diagonal=True)
        else:
            softmax_pv(qi, 0, diagonal=True)
        outs = [acc_ref[i, :HEAD_DIM, :] / acc_ref[i, HEAD_DIM:, :] for i in range(heads)]
        o_ref[0, cols, :] = jnp.concatenate(outs, axis=0).T.astype(o_ref.dtype)

    return start, tile_pair, finish


def _attn_call(qt, k, vt):
    b, n_heads, _, s = qt.shape
    t = ATT_TILE
    heads = LANES // HEAD_DIM
    tq = Q_TILES_PER_STEP * t
    qspec = pl.BlockSpec((1, heads, LANES, tq), lambda i, p, j: (i, p, 0, j))
    kspec = pl.BlockSpec((1, s, heads * LANES), lambda i, p, j: (i, 0, p))
    vspec = pl.BlockSpec((1, heads, LANES, s), lambda i, p, j: (i, p, 0, 0))
    return pl.pallas_call(
        _attn_kernel,
        grid=(b, n_heads // heads, s // tq),
        in_specs=[qspec, kspec, vspec],
        out_specs=pl.BlockSpec((1, tq, LANES), lambda i, p, j: (i, j, p)),
        out_shape=jax.ShapeDtypeStruct((b, s, n_heads * HEAD_DIM), BF16),
        scratch_shapes=[
            pltpu.VMEM((Q_TILES_PER_STEP, heads, 1, t), F32),
            pltpu.VMEM((Q_TILES_PER_STEP, heads, LANES, t), F32),
            pltpu.VMEM((Q_TILES_PER_STEP, 2, heads, t, t), F32)],
        compiler_params=pltpu.CompilerParams(
            dimension_semantics=("arbitrary", "arbitrary", "arbitrary"),
            vmem_limit_bytes=VMEM_LIMIT_BYTES),
        name="attn",
    )(qt, k, vt)


def _store_lane_tiles(ref, value):
    for c in range(ref.shape[0]):
        ref[c] = value[:, c * LANES:(c + 1) * LANES]


def _load_row_groups(ref, starts_strides):
    return jnp.concatenate(
        [jnp.concatenate([ref[c, pl.ds(start, SUBLANES, stride=stride), :]
                          for c in range(ref.shape[0])], axis=1)
         for start, stride in starts_strides], axis=0)


def _ffn_kernel(x_ref, att_ref, sg_ref, wo_ref, g2_ref, wup_ref, cw_ref, cb_ref, wdn_ref, g3_ref,
                o_ref, h1_ref, hnat_ref, hn_ref, hbuf_ref, act_ref, res_ref, carry_ref, *,
                final_norm):
    tm = x_ref.shape[1]
    n_chunks = wup_ref.shape[0]
    groups = tm // SUBLANES
    halo = (CONV_WIDTH - 1) * SUBLANES

    @pl.when(pl.program_id(1) == 0)
    def _():
        carry_ref[...] = jnp.zeros_like(carry_ref)

    mix = jnp.concatenate([att_ref[0], sg_ref[0]], axis=1)
    h1 = x_ref[0] + jnp.dot(mix, wo_ref[...], preferred_element_type=F32)
    h1_ref[...] = h1
    ms = jnp.mean(h1 * h1, axis=-1, keepdims=True)
    _store_lane_tiles(hnat_ref, (h1 * lax.rsqrt(ms + EPS)) * g2_ref[...])

    hn_ref[...] = _load_row_groups(hnat_ref, [(k, groups) for k in range(groups)]).astype(BF16)
    first_sublane = lax.broadcasted_iota(jnp.int32, (SUBLANES, 2 * FF_CHUNK), 0) == 0

    def up(c):
        hbuf = hbuf_ref.at[c % 2]
        hbuf[halo:, :] = jnp.dot(hn_ref[...], wup_ref[c], preferred_element_type=F32)
        for j in range(CONV_WIDTH - 1):
            prev_tile = carry_ref[c, j * SUBLANES:(j + 1) * SUBLANES, :]
            this_tile = hbuf[tm + j * SUBLANES:tm + (j + 1) * SUBLANES, :]
            hbuf[j * SUBLANES:(j + 1) * SUBLANES, :] = jnp.where(
                first_sublane, pltpu.roll(prev_tile, 1, axis=0), pltpu.roll(this_tile, 1, axis=0))
        carry_ref[c] = hbuf[tm:tm + halo, :]

    def gate(c):
        hbuf = hbuf_ref.at[c % 2]
        cw = cw_ref[c]
        y = cb_ref[c]
        for tap in range(CONV_WIDTH):
            lo = tap * SUBLANES
            y = y + hbuf[lo:lo + tm, :] * cw[tap:tap + 1, :]
        a = y[:, :FF_CHUNK]
        g = y[:, FF_CHUNK:]
        act_ref[:, c * FF_CHUNK:(c + 1) * FF_CHUNK] = (jax.nn.silu(g) * a).astype(BF16)

    up(0)
    for c in range(n_chunks):
        if c + 1 < n_chunks:
            up(c + 1)
        gate(c)

    _store_lane_tiles(res_ref, jnp.dot(act_ref[...], wdn_ref[...], preferred_element_type=F32))
    ffn = _load_row_groups(
        res_ref, [(SUBLANES * ((SUBLANES * k) % groups) + (SUBLANES * k) // groups, SUBLANES)
                  for k in range(groups)])
    h2 = h1_ref[...] + ffn
    if final_norm:
        ms2 = jnp.mean(h2 * h2, axis=-1, keepdims=True)
        h2 = (h2 * lax.rsqrt(ms2 + EPS)) * g3_ref[...]
    o_ref[0] = h2


def _ffn_call(x, att, sg, wo, g2, wup, cw, cb, wdn, g3, final_norm):
    b, s, d = x.shape
    tm = TOKEN_TILE
    n_chunks = wup.shape[0]
    d_ff = wdn.shape[0]
    halo = (CONV_WIDTH - 1) * SUBLANES
    tile = lambda w: pl.BlockSpec((1, tm, w), lambda i, j: (i, j, 0))
    return pl.pallas_call(
        functools.partial(_ffn_kernel, final_norm=final_norm),
        grid=(b, s // tm),
        in_specs=[tile(d), tile(att.shape[2]), tile(sg.shape[2]), _const_spec(wo.shape),
                  _const_spec(g2.shape), _const_spec(wup.shape), _const_spec(cw.shape),
                  _const_spec(cb.shape), _const_spec(wdn.shape), _const_spec(g3.shape)],
        out_specs=tile(d),
        out_shape=jax.ShapeDtypeStruct((b, s, d), F32),
        scratch_shapes=[pltpu.VMEM((tm, d), F32),
                        pltpu.VMEM((d // LANES, tm, LANES), F32),
                        pltpu.VMEM((tm, d), BF16),
                        pltpu.VMEM((2, tm + halo, 2 * FF_CHUNK), F32),
                        pltpu.VMEM((tm, d_ff), BF16),
                        pltpu.VMEM((d // LANES, tm, LANES), F32),
                        pltpu.VMEM((n_chunks, halo, 2 * FF_CHUNK), F32)],
        compiler_params=pltpu.CompilerParams(
            dimension_semantics=("arbitrary", "arbitrary"), vmem_limit_bytes=VMEM_LIMIT_BYTES),
        name="ffn",
    )(x, att, sg, wo, g2, wup, cw, cb, wdn, g3)


def kernel(x, norm_mix_g, w_in, b_forget, gmlp_norm_g, w_spatial, b_spatial, w_out, norm_ffn_g,
           w_up, conv_w, conv_b, w_down, norm_final_g):
    depth, d_model, _ = w_in.shape
    n_heads = b_forget.shape[1]
    att_w = n_heads * HEAD_DIM
    gm_w = gmlp_norm_g.shape[1]
    n_groups = w_spatial.shape[1]
    d_ff = w_down.shape[1]
    assert gm_w == n_groups * GROUP_DIM and w_spatial.shape[2] == CHUNK
    assert n_heads * AUG_W <= LANES and att_w % LANES == 0 and gm_w % MXU_DIM == 0
    assert d_ff % FF_CHUNK == 0 and x.shape[1] % TOKEN_TILE == 0 and x.shape[1] % PROJ_TILE == 0
    assert Q_TILES_PER_STEP % 4 == 0 and x.shape[1] % (Q_TILES_PER_STEP * ATT_TILE) == 0
    n_chunks = d_ff // FF_CHUNK
    scale = HEAD_DIM ** -0.5

    gi = jnp.arange(MXU_DIM) // GROUP_DIM
    gsum = jnp.where(gi[:, None] == gi[None, :], 1.0 / GROUP_DIM, 0.0).astype(BF16)

    h = x
    for layer in range(depth):
        w = w_in[layer]
        wq = w[:, :att_w] * scale
        wk = w[:, att_w:2 * att_w].astype(BF16)
        wqvt = jnp.concatenate([wq, w[:, 2 * att_w:3 * att_w]], axis=1).T.astype(BF16)
        wuv = w[:, 3 * att_w:3 * att_w + 2 * gm_w].astype(BF16)
        wgate = w[:, 3 * att_w + 2 * gm_w:]
        wg = jnp.zeros((d_model, LANES), F32).at[:, :n_heads * AUG_W].set(
            jnp.repeat(wgate, AUG_W, axis=1)).astype(BF16)
        bfr = jnp.zeros((1, LANES), F32).at[0, :n_heads * AUG_W].set(
            jnp.repeat(b_forget[layer], AUG_W))
        bs = jnp.repeat(b_spatial[layer].T, GROUP_DIM, axis=1)

        qt, k, vt, sg = _proj_call(
            h, norm_mix_g[layer][None, :], wk, wqvt, wuv, wg, bfr, gmlp_norm_g[layer][None, :],
            w_spatial[layer], bs, gsum)
        att = _attn_call(qt, k, vt)

        def chunked(m):
            a = m[..., :d_ff].reshape(m.shape[:-1] + (n_chunks, FF_CHUNK))
            g = m[..., d_ff:].reshape(m.shape[:-1] + (n_chunks, FF_CHUNK))
            return jnp.moveaxis(jnp.concatenate([a, g], axis=-1), -2, 0)
        wup = chunked(w_up[layer]).astype(BF16)
        cw = chunked(conv_w[layer])
        cb = chunked(conv_b[layer][None, :])
        h = _ffn_call(h, att, sg, w_out[layer].astype(BF16), norm_ffn_g[layer][None, :], wup, cw, cb,
                      w_down[layer].astype(BF16), norm_final_g[None, :], layer == depth - 1)
    return h
```

```python
import functools

import jax
import jax.numpy as jnp
from jax import lax
from jax.experimental import pallas as pl
from jax.experimental.pallas import tpu as pltpu

EPS = 1e-6
LOG2E = 1.4426950408889634
HEAD_DIM = 64
GROUP_DIM = 64
CHUNK = 128
CONV_WIDTH = 3

LANES = 128
SUBLANES = 8
MXU_DIM = 256
VMEM_LIMIT_BYTES = 56 * 1024 * 1024

AUG_W = 8
PROJ_TILE = 1024
TOKEN_TILE = 512
FFN_TILES_PER_STEP = 2
FFN_VMEM_LIMIT_BYTES = 60 * 1024 * 1024
ATT_TILE = 512
Q_TILES_PER_STEP = 4
FF_CHUNK = 256

F32 = jnp.float32
BF16 = jnp.bfloat16


def _split3(x):
    hi = x.astype(BF16)
    r1 = x - hi.astype(F32)
    mid = r1.astype(BF16)
    lo = (r1 - mid.astype(F32)).astype(BF16)
    return hi, mid, lo


def _const_spec(shape):
    zeros = (0,) * len(shape)
    return pl.BlockSpec(shape, lambda *_: zeros, pipeline_mode=pl.Buffered(1))


def _proj_kernel(x_ref, g_ref, wk_ref, wqvt_ref, wuv_ref, wg_ref, bf_ref, gn_ref, ws_ref, bs_ref,
                 gsum_ref, qt_ref, k_ref, vt_ref, sg_ref, carry_ref, *, att_w, gm_w):
    tm = x_ref.shape[1]

    @pl.when(pl.program_id(1) == 0)
    def _():
        carry_ref[...] = jnp.zeros_like(carry_ref)

    x = x_ref[0]
    ms = jnp.mean(x * x, axis=-1, keepdims=True)
    xn = ((x * lax.rsqrt(ms + EPS)) * g_ref[...]).astype(BF16)

    n_heads = att_w // HEAD_DIM
    uv = jax.nn.gelu(jnp.dot(xn, wuv_ref[...], preferred_element_type=F32))
    z = jnp.dot(xn, wg_ref[...], preferred_element_type=F32) + bf_ref[...]
    log_f = -(jnp.maximum(-z, 0.0) + jnp.log1p(jnp.exp(-jnp.abs(z))))
    k = jnp.dot(xn, wk_ref[...], preferred_element_type=F32)
    qvt = lax.dot_general(wqvt_ref[...], xn, (((1,), (1,)), ((), ())), preferred_element_type=F32)
    qt = qvt[:att_w] * LOG2E
    vt = qvt[att_w:]
    for h in range(n_heads):
        qt_ref[0, h, :HEAD_DIM, :] = qt[h * HEAD_DIM:(h + 1) * HEAD_DIM].astype(BF16)
        vt_ref[0, h, :HEAD_DIM, :] = vt[h * HEAD_DIM:(h + 1) * HEAD_DIM].astype(BF16)
        vt_ref[0, h, HEAD_DIM:, :] = jnp.ones((LANES - HEAD_DIM, tm), BF16)

    row = lax.broadcasted_iota(jnp.int32, (CHUNK, CHUNK), 0)
    col = lax.broadcasted_iota(jnp.int32, (CHUNK, CHUNK), 1)
    causal = row >= col
    tri = jnp.where(causal, 1.0, 0.0).astype(BF16)
    offset = carry_ref[0:1, :]
    c_blocks = []
    for r in range(tm // CHUNK):
        blk = log_f[r * CHUNK:(r + 1) * CHUNK]
        s = offset
        for piece in _split3(blk):
            s = s + jnp.dot(tri, piece, preferred_element_type=F32)
        c_blocks.append(s)
        offset = s[CHUNK - 1:CHUNK, :]
    carry_ref[0:1, :] = offset
    c = jnp.concatenate(c_blocks, axis=0)

    u = uv[:, :gm_w]
    vg = uv[:, gm_w:]
    sq = (vg * vg).astype(BF16)
    gsum = gsum_ref[...]
    msg = jnp.concatenate(
        [jnp.dot(sq[:, s * MXU_DIM:(s + 1) * MXU_DIM], gsum, preferred_element_type=F32)
         for s in range(gm_w // MXU_DIM)], axis=1)
    vgn = (vg * lax.rsqrt(msg + EPS)) * gn_ref[...]

    groups_per_slab = MXU_DIM // GROUP_DIM
    lane_grp = lax.broadcasted_iota(jnp.int32, (CHUNK, MXU_DIM), 1) // GROUP_DIM
    w_masked = [jnp.where(causal, ws_ref[g], 0.0).astype(BF16) for g in range(ws_ref.shape[0])]
    w_slab = [jnp.concatenate(w_masked[s * groups_per_slab:(s + 1) * groups_per_slab], axis=1)
              for s in range(gm_w // MXU_DIM)]
    mixed_rows = []
    for r in range(tm // CHUNK):
        slabs = []
        for s in range(gm_w // MXU_DIM):
            vs = vgn[r * CHUNK:(r + 1) * CHUNK, s * MXU_DIM:(s + 1) * MXU_DIM]
            stacked = jnp.concatenate(
                [jnp.where(lane_grp == j, vs, 0.0).astype(BF16) for j in range(groups_per_slab)],
                axis=0)
            slabs.append(jnp.dot(w_slab[s], stacked, preferred_element_type=F32))
        mixed_rows.append(jnp.concatenate(slabs, axis=1) + bs_ref[...])
    mixed = jnp.concatenate(mixed_rows, axis=0)
    sg_ref[0] = (u * mixed).astype(BF16)

    pieces = [piece.astype(F32) for piece in _split3(c * LOG2E)]
    n_pieces = len(pieces)
    assert 2 * n_pieces <= AUG_W
    slot = lax.broadcasted_iota(jnp.int32, (tm, LANES), 1) % AUG_W
    qa = jnp.where(slot < 2 * n_pieces, 1.0, 0.0)
    ka = qa
    for j, piece in enumerate(pieces):
        qa = jnp.where(slot == j, piece, qa)
        ka = jnp.where(slot == n_pieces + j, -piece, ka)
    qat = qa.T
    lane = lax.broadcasted_iota(jnp.int32, (tm, LANES), 1)
    pad_rows = 2 * SUBLANES - AUG_W
    for h in range(n_heads):
        decay_rows = jnp.concatenate(
            [qat[h * AUG_W:(h + 1) * AUG_W], jnp.zeros((pad_rows, tm), F32)], axis=0)
        qt_ref[0, h, HEAD_DIM:HEAD_DIM + 2 * SUBLANES, :] = decay_rows.astype(BF16)
        qt_ref[0, h, HEAD_DIM + 2 * SUBLANES:, :] = jnp.zeros(
            (LANES - HEAD_DIM - 2 * SUBLANES, tm), BF16)
        kh = k[:, (h // 2) * LANES:(h // 2 + 1) * LANES]
        if h % 2:
            kh = pltpu.roll(kh, HEAD_DIM, axis=1)
        kah = pltpu.roll(ka, HEAD_DIM - h * AUG_W, axis=1)
        k_ref[0, :, h * LANES:(h + 1) * LANES] = jnp.where(
            lane < HEAD_DIM, kh, jnp.where(lane < HEAD_DIM + AUG_W, kah, 0.0)).astype(BF16)


def _proj_call(x, g, wk, wqvt, wuv, wg, bfr, gn, ws, bs, gsum):
    b, s, d = x.shape
    tm = PROJ_TILE
    att_w = wk.shape[1]
    n_heads = att_w // HEAD_DIM
    gm_w = wuv.shape[1] // 2
    tile = lambda w: pl.BlockSpec((1, tm, w), lambda i, j: (i, j, 0))
    head_t = lambda rows: pl.BlockSpec((1, n_heads, rows, tm), lambda i, j: (i, 0, 0, j))
    out_shape = [jax.ShapeDtypeStruct((b, n_heads, LANES, s), BF16),
                 jax.ShapeDtypeStruct((b, s, n_heads * LANES), BF16),
                 jax.ShapeDtypeStruct((b, n_heads, LANES, s), BF16),
                 jax.ShapeDtypeStruct((b, s, gm_w), BF16)]
    return pl.pallas_call(
        functools.partial(_proj_kernel, att_w=att_w, gm_w=gm_w),
        grid=(b, s // tm),
        in_specs=[tile(d), _const_spec(g.shape), _const_spec(wk.shape), _const_spec(wqvt.shape),
                  _const_spec(wuv.shape), _const_spec(wg.shape), _const_spec(bfr.shape),
                  _const_spec(gn.shape), _const_spec(ws.shape), _const_spec(bs.shape),
                  _const_spec(gsum.shape)],
        out_specs=[head_t(LANES), tile(n_heads * LANES), head_t(LANES), tile(gm_w)],
        out_shape=out_shape,
        scratch_shapes=[pltpu.VMEM((SUBLANES, LANES), F32)],
        compiler_params=pltpu.CompilerParams(
            dimension_semantics=("arbitrary", "arbitrary"), vmem_limit_bytes=VMEM_LIMIT_BYTES),
        name="proj",
    )(x, g, wk, wqvt, wuv, wg, bfr, gn, ws, bs, gsum)


def _attn_kernel(qt_ref, k_ref, vt_ref, o_ref, m_all, acc_all, s_all):
    t = s_all.shape[3]
    heads = qt_ref.shape[1]
    step = pl.program_id(2)
    key = lax.broadcasted_iota(jnp.int32, (t, t), 0)
    qry = lax.broadcasted_iota(jnp.int32, (t, t), 1)
    tiles = [_attn_query_tile(qt_ref, k_ref, vt_ref, o_ref, m_all.at[w], acc_all.at[w],
                              s_all.at[w], key, qry, step, w, t, heads)
             for w in range(Q_TILES_PER_STEP)]
    for start, _, _ in tiles:
        start()

    def body(i, carry):
        for pair in (2 * i, 2 * i + 1):
            for _, tile_pair, _ in tiles:
                tile_pair(pair)
        return carry

    lax.fori_loop(0, (Q_TILES_PER_STEP // 4) * step, body, 0)
    for _, _, finish in tiles:
        finish()


def _attn_query_tile(qt_ref, k_ref, vt_ref, o_ref, m_ref, acc_ref, s_ref, key, qry, step, w, t,
                     heads):
    qi = Q_TILES_PER_STEP * step + w
    cols = slice(w * t, (w + 1) * t)

    def scores(tile, par, only=None):
        start = pl.multiple_of(tile * t, t)
        for i in (range(heads) if only is None else [only]):
            kf = k_ref[0, pl.ds(start, t), i * LANES:(i + 1) * LANES]
            s_ref[par, i] = jnp.dot(kf, qt_ref[0, i, :, cols],
                                    preferred_element_type=F32)

    def softmax_pv(tile, par, diagonal=False, only=None):
        start = pl.multiple_of(tile * t, t)
        for i in (range(heads) if only is None else [only]):
            vt = vt_ref[0, i, :, pl.ds(start, t)]
            s = s_ref[par, i]
            if diagonal:
                s = jnp.where(qry >= key, s, -jnp.inf)
            m_prev = m_ref[i]
            m_next = jnp.maximum(m_prev, jnp.max(s, axis=0, keepdims=True))
            alpha = jnp.exp2(m_prev - m_next)
            p = jnp.exp2(s - m_next).astype(BF16)
            pv = jnp.dot(vt, p, preferred_element_type=F32)
            acc_ref[i] = acc_ref[i] * alpha + pv
            m_ref[i] = m_next

    def start():
        m_ref[...] = jnp.full_like(m_ref, -jnp.inf)
        acc_ref[...] = jnp.zeros_like(acc_ref)
        scores(0, 0)

    def tile_pair(i):
        for h in range(heads):
            scores(2 * i + 1, 1, only=h)
            softmax_pv(2 * i, 0, only=h)
        for h in range(heads):
            scores(2 * i + 2, 0, only=h)
            softmax_pv(2 * i + 1, 1, only=h)

    def finish():
        done = 2 * (Q_TILES_PER_STEP // 4) * step
        for extra in range(w // 2):
            tile_pair(done + extra)
        if w % 2:
            scores(qi, 1)
            softmax_pv(qi - 1, 0)
            softmax_pv(qi, 1, diagonal=True)
        else:
            softmax_pv(qi, 0, diagonal=True)
        outs = [acc_ref[i, :HEAD_DIM, :] / acc_ref[i, HEAD_DIM:, :] for i in range(heads)]
        o_ref[0, cols, :] = jnp.concatenate(outs, axis=0).T.astype(o_ref.dtype)

    return start, tile_pair, finish


def _attn_call(qt, k, vt):
    b, n_heads, _, s = qt.shape
    t = ATT_TILE
    heads = LANES // HEAD_DIM
    tq = Q_TILES_PER_STEP * t
    qspec = pl.BlockSpec((1, heads, LANES, tq), lambda i, p, j: (i, p, 0, j))
    kspec = pl.BlockSpec((1, s, heads * LANES), lambda i, p, j: (i, 0, p))
    vspec = pl.BlockSpec((1, heads, LANES, s), lambda i, p, j: (i, p, 0, 0))
    return pl.pallas_call(
        _attn_kernel,
        grid=(b, n_heads // heads, s // tq),
        in_specs=[qspec, kspec, vspec],
        out_specs=pl.BlockSpec((1, tq, LANES), lambda i, p, j: (i, j, p)),
        out_shape=jax.ShapeDtypeStruct((b, s, n_heads * HEAD_DIM), BF16),
        scratch_shapes=[
            pltpu.VMEM((Q_TILES_PER_STEP, heads, 1, t), F32),
            pltpu.VMEM((Q_TILES_PER_STEP, heads, LANES, t), F32),
            pltpu.VMEM((Q_TILES_PER_STEP, 2, heads, t, t), F32)],
        compiler_params=pltpu.CompilerParams(
            dimension_semantics=("arbitrary", "arbitrary", "arbitrary"),
            vmem_limit_bytes=VMEM_LIMIT_BYTES),
        name="attn",
    )(qt, k, vt)


def _store_lane_tiles(ref, value):
    for c in range(ref.shape[0]):
        ref[c] = value[:, c * LANES:(c + 1) * LANES]


def _load_row_groups(ref, starts_strides):
    return jnp.concatenate(
        [jnp.concatenate([ref[c, pl.ds(start, SUBLANES, stride=stride), :]
                          for c in range(ref.shape[0])], axis=1)
         for start, stride in starts_strides], axis=0)


def _ffn_kernel(x_ref, att_ref, sg_ref, wo_ref, g2_ref, wup_ref, cw_ref, cb_ref, wdn_ref, g3_ref,
                o_ref, h1_ref, hnat_ref, hn_ref, hbuf_ref, act_ref, res_ref, carry_ref, *,
                final_norm):
    tm = x_ref.shape[1] // FFN_TILES_PER_STEP
    n_chunks = wup_ref.shape[0]
    groups = tm // SUBLANES
    halo = (CONV_WIDTH - 1) * SUBLANES

    @pl.when(pl.program_id(1) == 0)
    def _():
        carry_ref[...] = jnp.zeros_like(carry_ref)

    def prepare(u):
        rows = slice(u * tm, (u + 1) * tm)
        mix = jnp.concatenate([att_ref[0, rows, :], sg_ref[0, rows, :]], axis=1)
        h1 = x_ref[0, rows, :] + jnp.dot(mix, wo_ref[...], preferred_element_type=F32)
        h1_ref[u] = h1
        ms = jnp.mean(h1 * h1, axis=-1, keepdims=True)
        _store_lane_tiles(hnat_ref, (h1 * lax.rsqrt(ms + EPS)) * g2_ref[...])
        hn_ref[u] = _load_row_groups(hnat_ref, [(k, groups) for k in range(groups)]).astype(BF16)

    first_sublane = lax.broadcasted_iota(jnp.int32, (SUBLANES, 2 * FF_CHUNK), 0) == 0
    prepare(0)
    for u in range(FFN_TILES_PER_STEP):
        _ffn_sub_tile(u, tm, n_chunks, groups, halo, first_sublane, prepare, x_ref, o_ref, h1_ref,
                      hn_ref, hbuf_ref, act_ref, res_ref, carry_ref, wup_ref, cw_ref, cb_ref,
                      wdn_ref, g3_ref, final_norm)


def _ffn_sub_tile(u, tm, n_chunks, groups, halo, first_sublane, prepare, x_ref, o_ref, h1_ref,
                  hn_ref, hbuf_ref, act_ref, res_ref, carry_ref, wup_ref, cw_ref, cb_ref, wdn_ref,
                  g3_ref, final_norm):
    def up(c):
        hbuf = hbuf_ref.at[c % 2]
        hbuf[halo:, :] = jnp.dot(hn_ref[u], wup_ref[c], preferred_element_type=F32)
        for j in range(CONV_WIDTH - 1):
            prev_tile = carry_ref[c, j * SUBLANES:(j + 1) * SUBLANES, :]
            this_tile = hbuf[tm + j * SUBLANES:tm + (j + 1) * SUBLANES, :]
            hbuf[j * SUBLANES:(j + 1) * SUBLANES, :] = jnp.where(
                first_sublane, pltpu.roll(prev_tile, 1, axis=0), pltpu.roll(this_tile, 1, axis=0))
        carry_ref[c] = hbuf[tm:tm + halo, :]

    def gate(c):
        hbuf = hbuf_ref.at[c % 2]
        cw = cw_ref[c]
        y = cb_ref[c]
        for tap in range(CONV_WIDTH):
            lo = tap * SUBLANES
            y = y + hbuf[lo:lo + tm, :] * cw[tap:tap + 1, :]
        a = y[:, :FF_CHUNK]
        g = y[:, FF_CHUNK:]
        act_ref[:, c * FF_CHUNK:(c + 1) * FF_CHUNK] = (jax.nn.silu(g) * a).astype(BF16)

    up(0)
    for c in range(n_chunks):
        if c + 1 < n_chunks:
            up(c + 1)
        gate(c)

    if u + 1 < FFN_TILES_PER_STEP:
        prepare(u + 1)

    _store_lane_tiles(res_ref, jnp.dot(act_ref[...], wdn_ref[...], preferred_element_type=F32))
    ffn = _load_row_groups(
        res_ref, [(SUBLANES * ((SUBLANES * k) % groups) + (SUBLANES * k) // groups, SUBLANES)
                  for k in range(groups)])
    h2 = h1_ref[u] + ffn
    if final_norm:
        ms2 = jnp.mean(h2 * h2, axis=-1, keepdims=True)
        h2 = (h2 * lax.rsqrt(ms2 + EPS)) * g3_ref[...]
    o_ref[0, u * tm:(u + 1) * tm, :] = h2


def _ffn_call(x, att, sg, wo, g2, wup, cw, cb, wdn, g3, final_norm):
    b, s, d = x.shape
    tm = TOKEN_TILE
    n_chunks = wup.shape[0]
    d_ff = wdn.shape[0]
    halo = (CONV_WIDTH - 1) * SUBLANES
    rows = FFN_TILES_PER_STEP * tm
    tile = lambda w: pl.BlockSpec((1, rows, w), lambda i, j: (i, j, 0))
    return pl.pallas_call(
        functools.partial(_ffn_kernel, final_norm=final_norm),
        grid=(b, s // rows),
        in_specs=[tile(d), tile(att.shape[2]), tile(sg.shape[2]), _const_spec(wo.shape),
                  _const_spec(g2.shape), _const_spec(wup.shape), _const_spec(cw.shape),
                  _const_spec(cb.shape), _const_spec(wdn.shape), _const_spec(g3.shape)],
        out_specs=tile(d),
        out_shape=jax.ShapeDtypeStruct((b, s, d), F32),
        scratch_shapes=[pltpu.VMEM((FFN_TILES_PER_STEP, tm, d), F32),
                        pltpu.VMEM((d // LANES, tm, LANES), F32),
                        pltpu.VMEM((FFN_TILES_PER_STEP, tm, d), BF16),
                        pltpu.VMEM((2, tm + halo, 2 * FF_CHUNK), F32),
                        pltpu.VMEM((tm, d_ff), BF16),
                        pltpu.VMEM((d // LANES, tm, LANES), F32),
                        pltpu.VMEM((n_chunks, halo, 2 * FF_CHUNK), F32)],
        compiler_params=pltpu.CompilerParams(
            dimension_semantics=("arbitrary", "arbitrary"),
            vmem_limit_bytes=FFN_VMEM_LIMIT_BYTES),
        name="ffn",
    )(x, att, sg, wo, g2, wup, cw, cb, wdn, g3)


def kernel(x, norm_mix_g, w_in, b_forget, gmlp_norm_g, w_spatial, b_spatial, w_out, norm_ffn_g,
           w_up, conv_w, conv_b, w_down, norm_final_g):
    depth, d_model, _ = w_in.shape
    n_heads = b_forget.shape[1]
    att_w = n_heads * HEAD_DIM
    gm_w = gmlp_norm_g.shape[1]
    n_groups = w_spatial.shape[1]
    d_ff = w_down.shape[1]
    assert gm_w == n_groups * GROUP_DIM and w_spatial.shape[2] == CHUNK
    assert n_heads * AUG_W <= LANES and att_w % LANES == 0 and gm_w % MXU_DIM == 0
    assert d_ff % FF_CHUNK == 0 and x.shape[1] % PROJ_TILE == 0
    assert x.shape[1] % (FFN_TILES_PER_STEP * TOKEN_TILE) == 0
    assert Q_TILES_PER_STEP % 4 == 0 and x.shape[1] % (Q_TILES_PER_STEP * ATT_TILE) == 0
    n_chunks = d_ff // FF_CHUNK
    scale = HEAD_DIM ** -0.5

    gi = jnp.arange(MXU_DIM) // GROUP_DIM
    gsum = jnp.where(gi[:, None] == gi[None, :], 1.0 / GROUP_DIM, 0.0).astype(BF16)

    h = x
    for layer in range(depth):
        w = w_in[layer]
        wq = w[:, :att_w] * scale
        wk = w[:, att_w:2 * att_w].astype(BF16)
        wqvt = jnp.concatenate([wq, w[:, 2 * att_w:3 * att_w]], axis=1).T.astype(BF16)
        wuv = w[:, 3 * att_w:3 * att_w + 2 * gm_w].astype(BF16)
        wgate = w[:, 3 * att_w + 2 * gm_w:]
        wg = jnp.zeros((d_model, LANES), F32).at[:, :n_heads * AUG_W].set(
            jnp.repeat(wgate, AUG_W, axis=1)).astype(BF16)
        bfr = jnp.zeros((1, LANES), F32).at[0, :n_heads * AUG_W].set(
            jnp.repeat(b_forget[layer], AUG_W))
        bs = jnp.repeat(b_spatial[layer].T, GROUP_DIM, axis=1)

        qt, k, vt, sg = _proj_call(
            h, norm_mix_g[layer][None, :], wk, wqvt, wuv, wg, bfr, gmlp_norm_g[layer][None, :],
            w_spatial[layer], bs, gsum)
        att = _attn_call(qt, k, vt)

        def chunked(m):
            a = m[..., :d_ff].reshape(m.shape[:-1] + (n_chunks, FF_CHUNK))
            g = m[..., d_ff:].reshape(m.shape[:-1] + (n_chunks, FF_CHUNK))
            return jnp.moveaxis(jnp.concatenate([a, g], axis=-1), -2, 0)
        wup = chunked(w_up[layer]).astype(BF16)
        cw = chunked(conv_w[layer])
        cb = chunked(conv_b[layer][None, :])
        h = _ffn_call(h, att, sg, w_out[layer].astype(BF16), norm_ffn_g[layer][None, :], wup, cw, cb,
                      w_down[layer].astype(BF16), norm_final_g[None, :], layer == depth - 1)
    return h
```
